```python
import jax
import jax.numpy as jnp
from jax import lax

D_MODEL = 1024
BATCH = 4
SEQ = 8192
DEPTH = 1

PLE_DIM = 256
EPS = 1e-6

GLA_HEADS = 4
GLA_DK = 128
GLA_DV = 256
GLA_QK_W = GLA_HEADS * GLA_DK
GLA_V_W = GLA_HEADS * GLA_DV
GLA_RANK = 16
GLA_TAU = 16.0
GLA_CHUNK = 64

ATT_GROUPS = ((128, 1), (512, 4), (2048, 16))
ATT_HEADS_PER_GROUP = 4
ATT_HEAD_DIM = 64
ATT_HEADS = ATT_HEADS_PER_GROUP * len(ATT_GROUPS)
ATT_W = ATT_HEADS * ATT_HEAD_DIM
ATT_OUT_W = ATT_HEADS_PER_GROUP * ATT_HEAD_DIM
ROT_DIM = ATT_HEAD_DIM // 4
ROPE_THETA = 500000.0

N_EXPERTS = 32
TOP_K = 4
D_EXPERT = D_MODEL
SWIGLU_ALPHA = 1.702
SWIGLU_LIMIT = 7.0

IN_SPLITS = (GLA_QK_W, GLA_QK_W, GLA_V_W, GLA_V_W, 2 * GLA_RANK, ATT_W, ATT_W, ATT_W, D_MODEL, D_MODEL)
IN_COLS = sum(IN_SPLITS)

kernel_name = 'hybrid_gla_dilated_moe_encoder'


def rmsnorm(x, g):
    xf = x.astype(jnp.float32)
    y = xf * lax.rsqrt(jnp.mean(xf * xf, axis=-1, keepdims=True) + EPS)
    return (y * g.astype(jnp.float32)).astype(x.dtype)


def partial_rotary(t, positions):
    half = ROT_DIM // 2
    inv_freq = ROPE_THETA ** (-jnp.arange(0, ROT_DIM, 2, dtype=jnp.float32) / ROT_DIM)
    ang = positions.astype(jnp.float32)[..., None] * inv_freq
    cos = jnp.cos(ang)[:, :, None, :]
    sin = jnp.sin(ang)[:, :, None, :]
    tr = t[..., :ROT_DIM].astype(jnp.float32)
    t1, t2 = tr[..., :half], tr[..., half:]
    rot = jnp.concatenate([t1 * cos - t2 * sin, t2 * cos + t1 * sin], axis=-1)
    return jnp.concatenate([rot.astype(t.dtype), t[..., ROT_DIM:]], axis=-1)


def gla_chunked(q, k, v, g, strict):
    B, S, H, DK = q.shape
    DV = v.shape[-1]
    C = GLA_CHUNK
    N = S // C
    f32 = jnp.float32
    q = q.astype(f32).reshape(B, N, C, H, DK) * (DK ** -0.5)
    k = k.astype(f32).reshape(B, N, C, H, DK)
    v = v.astype(f32).reshape(B, N, C, H, DV)
    b = jnp.cumsum(g.astype(f32).reshape(B, N, C, H, DK), axis=2)
    b_ref = b[:, :, C // 2 - 1:C // 2]
    b_last = b[:, :, C - 1:C]
    scores = jnp.einsum('bnihd,bnjhd->bnhij', q * jnp.exp(b - b_ref), k * jnp.exp(b_ref - b))
    mask = jnp.tril(jnp.ones((C, C), dtype=bool), -1 if strict else 0)
    scores = jnp.where(mask, scores, 0.0)
    o_intra = jnp.einsum('bnhij,bnjhv->bnihv', scores, v)
    q_in = jnp.moveaxis(q * jnp.exp(b), 1, 0)
    k_st = jnp.moveaxis(k * jnp.exp(b_last - b), 1, 0)
    v_st = jnp.moveaxis(v, 1, 0)
    decay = jnp.moveaxis(jnp.exp(b_last[:, :, 0]), 1, 0)

    def step(state, xs):
        qc, kc, vc, dc = xs
        o = jnp.einsum('bihd,bhdv->bihv', qc, state)
        state = dc[..., None] * state + jnp.einsum('bjhd,bjhv->bhdv', kc, vc)
        return state, o

    state0 = jnp.zeros((B, H, DK, DV), f32)
    _, o_inter = lax.scan(step, state0, (q_in, k_st, v_st, decay))
    o = o_intra + jnp.moveaxis(o_inter, 0, 1)
    return o.reshape(B, S, H, DV)


def gla_branch(q, k, v, r, a_low, fgate_up, fgate_bias, out_norm):
    B, S, _ = q.shape
    q = q.reshape(B, S, GLA_HEADS, GLA_DK)
    k = k.reshape(B, S, GLA_HEADS, GLA_DK)
    v = v.reshape(B, S, GLA_HEADS, GLA_DV)
    a_low = a_low.reshape(B, S, 2, GLA_RANK)
    z = jnp.einsum('bsdr,drk->bsdk', a_low, fgate_up) + fgate_bias
    g = (jax.nn.log_sigmoid(z.astype(jnp.float32)) / GLA_TAU).reshape(B, S, 2, GLA_HEADS, GLA_DK)
    o_fwd = gla_chunked(q, k, v, g[:, :, 0], strict=False)
    o_bwd = jnp.flip(gla_chunked(jnp.flip(q, 1), jnp.flip(k, 1), jnp.flip(v, 1),
                                 jnp.flip(g[:, :, 1], 1), strict=True), 1)
    o = o_fwd + o_bwd
    o = o * lax.rsqrt(jnp.mean(o * o, axis=-1, keepdims=True) + EPS) * out_norm.astype(jnp.float32)
    y = o * jax.nn.silu(r.reshape(B, S, GLA_HEADS, GLA_DV).astype(jnp.float32))
    return y.reshape(B, S, GLA_V_W).astype(r.dtype)


def banded_attention(q, k, v, half):
    N, L, H, hd = q.shape
    nb = -(-L // half)
    Lp = nb * half
    padw = lambda t, lo, hi: jnp.pad(t, ((0, 0), (lo, hi), (0, 0), (0, 0)))
    qb = padw(q, 0, Lp - L).reshape(N, nb, half, H, hd).astype(jnp.float32)
    kp = padw(k, half, Lp - L + half).reshape(N, nb + 2, half, H, hd).astype(jnp.float32)
    vp = padw(v, half, Lp - L + half).reshape(N, nb + 2, half, H, hd).astype(jnp.float32)
    kb = jnp.concatenate([kp[:, :-2], kp[:, 1:-1], kp[:, 2:]], axis=2)
    vb = jnp.concatenate([vp[:, :-2], vp[:, 1:-1], vp[:, 2:]], axis=2)
    s = jnp.einsum('nbihd,nbjhd->nbhij', qb, kb) * (hd ** -0.5)
    qpos = jnp.arange(nb)[:, None] * half + jnp.arange(half)[None, :]
    kpos = jnp.arange(nb)[:, None] * half - half + jnp.arange(3 * half)[None, :]
    valid = ((jnp.abs(qpos[:, :, None] - kpos[:, None, :]) <= half)
             & (kpos[:, None, :] >= 0) & (kpos[:, None, :] < L))
    s = jnp.where(valid[None, :, None], s, -jnp.inf)
    m = jnp.max(s, axis=-1, keepdims=True)
    pr = jnp.exp(s - m)
    l = jnp.sum(pr, axis=-1, keepdims=True)
    o = jnp.einsum('nbhij,nbjhd->nbihd', pr / l, vb).reshape(N, Lp, H, hd)[:, :L]
    lse = jnp.transpose((m + jnp.log(l))[..., 0], (0, 1, 3, 2)).reshape(N, Lp, H)[:, :L]
    return o, lse


def dilated_group(q, k, v, window, dilation):
    B, S, H, hd = q.shape
    L = S // dilation
    half = window // (2 * dilation)
    to_sub = lambda t: t.reshape(B, L, dilation, H, hd).transpose(0, 2, 1, 3, 4).reshape(B * dilation, L, H, hd)
    o, lse = banded_attention(to_sub(q), to_sub(k), to_sub(v), half)
    o = o.reshape(B, dilation, L, H, hd).transpose(0, 2, 1, 3, 4).reshape(B, S, H, hd)
    lse = lse.reshape(B, dilation, L, H).transpose(0, 2, 1, 3).reshape(B, S, H)
    return o, lse


def dilated_branch(q, k, v, positions):
    B, S, _ = q.shape
    shp = (B, S, ATT_HEADS, ATT_HEAD_DIM)
    q = partial_rotary(q.reshape(shp), positions)
    k = partial_rotary(k.reshape(shp), positions)
    v = v.reshape(shp)
    outs, lses = [], []
    for gi, (window, dilation) in enumerate(ATT_GROUPS):
        hs = slice(gi * ATT_HEADS_PER_GROUP, (gi + 1) * ATT_HEADS_PER_GROUP)
        o, lse = dilated_group(q[:, :, hs], k[:, :, hs], v[:, :, hs], window, dilation)
        outs.append(o)
        lses.append(lse)
    w = jax.nn.softmax(jnp.stack(lses, axis=0), axis=0)
    o = jnp.sum(w[..., None] * jnp.stack(outs, axis=0), axis=0)
    return o.reshape(B, S, ATT_OUT_W).astype(q.dtype)


def moe(h, router_w, router_b, w1, b1, w2, b2):
    B, S, D = h.shape
    t = h.reshape(B * S, D)
    logits = (t @ router_w + router_b).astype(jnp.float32)
    top_val, top_idx = lax.top_k(logits, TOP_K)
    top_w = jax.nn.softmax(top_val, axis=-1)
    gates = jnp.sum(jax.nn.one_hot(top_idx, N_EXPERTS, dtype=jnp.float32) * top_w[..., None], axis=1)
    y = jnp.zeros((B * S, D), jnp.float32)
    for e in range(N_EXPERTS):
        hid = t @ w1[e] + b1[e]
        gate = jnp.minimum(hid[:, 0::2], SWIGLU_LIMIT)
        up = jnp.clip(hid[:, 1::2], -SWIGLU_LIMIT, SWIGLU_LIMIT)
        glu = gate * jax.nn.sigmoid(gate * SWIGLU_ALPHA)
        out = ((up + 1.0) * glu) @ w2[e] + b2[e]
        y = y + gates[:, e:e + 1] * out.astype(jnp.float32)
    return y.reshape(B, S, D).astype(h.dtype)


def setup_inputs(seed: int = 0) -> dict:
    key = jax.random.key(seed)
    ks = jax.random.split(key, 24)
    f32 = jnp.float32
    nrm = lambda kk, shape, scale: jax.random.normal(kk, shape, f32) * scale
    gain = lambda kk, shape: 1.0 + 0.02 * jax.random.normal(kk, shape, f32)
    return {
        'x': nrm(ks[0], (BATCH, SEQ, D_MODEL), 1.0),
        'p': nrm(ks[1], (DEPTH, BATCH, SEQ, PLE_DIM), 1.0),
        'positions': jnp.tile(jnp.arange(SEQ, dtype=jnp.int32)[None, :], (BATCH, 1)),
        'norm_mix': gain(ks[2], (DEPTH, D_MODEL)),
        'w_in': nrm(ks[3], (DEPTH, D_MODEL, IN_COLS), D_MODEL ** -0.5),
        'gla_fgate_up': nrm(ks[4], (DEPTH, 2, GLA_RANK, GLA_QK_W), GLA_RANK ** -0.5),
        'gla_fgate_bias': nrm(ks[5], (DEPTH, 2, GLA_QK_W), 0.1),
        'gla_out_norm': gain(ks[6], (DEPTH, GLA_HEADS, GLA_DV)),
        'w_branch_gla': nrm(ks[7], (DEPTH, GLA_V_W, D_MODEL), GLA_V_W ** -0.5),
        'w_branch_attn': nrm(ks[8], (DEPTH, ATT_OUT_W, D_MODEL), ATT_OUT_W ** -0.5),
        'w_out': nrm(ks[9], (DEPTH, D_MODEL, D_MODEL), D_MODEL ** -0.5),
        'norm_ffn': gain(ks[10], (DEPTH, D_MODEL)),
        'router_w': nrm(ks[11], (DEPTH, D_MODEL, N_EXPERTS), D_MODEL ** -0.5),
        'router_b': nrm(ks[12], (DEPTH, N_EXPERTS), 0.01),
        'expert_w1': nrm(ks[13], (DEPTH, N_EXPERTS, D_MODEL, 2 * D_EXPERT), D_MODEL ** -0.5),
        'expert_b1': nrm(ks[14], (DEPTH, N_EXPERTS, 2 * D_EXPERT), 0.01),
        'expert_w2': nrm(ks[15], (DEPTH, N_EXPERTS, D_EXPERT, D_MODEL), D_EXPERT ** -0.5),
        'expert_b2': nrm(ks[16], (DEPTH, N_EXPERTS, D_MODEL), 0.01),
        'norm_ple': gain(ks[17], (DEPTH, D_MODEL)),
        'ple_gate_w': nrm(ks[18], (DEPTH, D_MODEL, D_MODEL), D_MODEL ** -0.5),
        'ple_proj': nrm(ks[19], (DEPTH, PLE_DIM, D_MODEL), PLE_DIM ** -0.5),
        'norm_final': gain(ks[20], (D_MODEL,)),
    }


def reference(x, p, positions, norm_mix, w_in, gla_fgate_up, gla_fgate_bias, gla_out_norm,
              w_branch_gla, w_branch_attn, w_out, norm_ffn, router_w, router_b,
              expert_w1, expert_b1, expert_w2, expert_b2, norm_ple, ple_gate_w, ple_proj, norm_final):
    split_at = []
    acc = 0
    for width in IN_SPLITS[:-1]:
        acc += width
        split_at.append(acc)
    for i in range(DEPTH):
        h = rmsnorm(x, norm_mix[i])
        proj = h @ w_in[i]
        q_g, k_g, v_g, r_g, a_low, q_a, k_a, v_a, gate_a, gate_b = jnp.split(proj, split_at, axis=-1)
        y_gla = gla_branch(q_g, k_g, v_g, r_g, a_low, gla_fgate_up[i], gla_fgate_bias[i], gla_out_norm[i])
        y_att = dilated_branch(q_a, k_a, v_a, positions)
        merged = (jax.nn.sigmoid(gate_a) * (y_gla @ w_branch_gla[i])
                  + jax.nn.sigmoid(gate_b) * (y_att @ w_branch_attn[i]))
        x = x + merged @ w_out[i]
        x = x + moe(rmsnorm(x, norm_ffn[i]), router_w[i], router_b[i],
                    expert_w1[i], expert_b1[i], expert_w2[i], expert_b2[i])
        x = x + jax.nn.sigmoid(rmsnorm(x, norm_ple[i]) @ ple_gate_w[i]) * (p[i] @ ple_proj[i])
    return rmsnorm(x, norm_final)
```

```python
import functools

import jax
import jax.numpy as jnp
from jax import lax
from jax.experimental import pallas as pl
from jax.experimental.pallas import tpu as pltpu

F32 = jnp.float32
BF16 = jnp.bfloat16
I32 = jnp.int32

EPS = 1e-6
GLA_HEADS = 4
GLA_DK = 128
GLA_DV = 256
GLA_RANK = 16
GLA_TAU = 16.0
GLA_CHUNK = 64
ATT_GROUPS = ((128, 1), (512, 4), (2048, 16))
ATT_HPG = 4
ATT_HD = 64
ATT_GW = ATT_HPG * ATT_HD
ROT_DIM = 16
ROPE_THETA = 500000.0
N_EXPERTS = 32
TOP_K = 4
SWIGLU_ALPHA = 1.702
SWIGLU_LIMIT = 7.0

LANES = 128
MIB = 1024 * 1024

SEG_WIDTHS = (512, 512, 1024, 1024, LANES, 768, 768, 768, 1024, 1024)
SEG_OFFS = tuple(sum(SEG_WIDTHS[:i]) for i in range(len(SEG_WIDTHS) + 1))


def _cparams(sem, vmem_mib):
    return pltpu.CompilerParams(dimension_semantics=sem, vmem_limit_bytes=vmem_mib * MIB)


def _inproj_kernel(x_ref, pos_ref, g_ref, tab_ref, w_ref,
                   qg_ref, kg_ref, vg_ref, rg_ref, al_ref, qa_ref, ka_ref, va_ref, ga_ref, gb_ref):
    x = x_ref[...]
    h = (x * lax.rsqrt(jnp.mean(x * x, axis=-1, keepdims=True) + EPS) * g_ref[...]).astype(BF16)

    def proj(seg):
        return jnp.dot(h, w_ref[:, SEG_OFFS[seg]:SEG_OFFS[seg + 1]], preferred_element_type=F32)

    qg_ref[...] = proj(0).astype(BF16)
    kg_ref[...] = proj(1).astype(BF16)
    vg_ref[...] = proj(2).astype(BF16)
    rg_ref[...] = proj(3).astype(BF16)
    al_ref[...] = proj(4).astype(BF16)

    ang = pos_ref[...].astype(F32) * tab_ref[0:1, :]
    cs = jnp.cos(ang)
    sn = jnp.sin(ang)
    lane = lax.broadcasted_iota(I32, (1, LANES), 1) % ATT_HD
    c_mul = jnp.where(lane < ROT_DIM, cs, 1.0)
    s_next = jnp.where(lane < ROT_DIM // 2, -sn, 0.0)
    s_prev = jnp.where((lane >= ROT_DIM // 2) & (lane < ROT_DIM), sn, 0.0)

    def rot(seg, o_ref):
        y = proj(seg)
        for j in range(SEG_WIDTHS[seg] // LANES):
            t = y[:, j * LANES:(j + 1) * LANES]
            r = (t * c_mul + pltpu.roll(t, LANES - ROT_DIM // 2, axis=1) * s_next
                 + pltpu.roll(t, ROT_DIM // 2, axis=1) * s_prev)
            o_ref[:, j * LANES:(j + 1) * LANES] = r.astype(BF16)

    rot(5, qa_ref)
    rot(6, ka_ref)
    va_ref[...] = proj(7).astype(BF16)
    ga_ref[...] = proj(8).astype(BF16)
    gb_ref[...] = proj(9).astype(BF16)


def _inproj(x2, pos2, gain, tab, w_cat, tm=512):
    T, D = x2.shape
    outs = [jax.ShapeDtypeStruct((T, w), BF16) for w in SEG_WIDTHS]
    return pl.pallas_call(
        _inproj_kernel,
        grid=(T // tm,),
        in_specs=[
            pl.BlockSpec((tm, D), lambda i: (i, 0)),
            pl.BlockSpec((tm, 1), lambda i: (i, 0)),
            pl.BlockSpec((1, D), lambda i: (0, 0)),
            pl.BlockSpec((8, LANES), lambda i: (0, 0)),
            pl.BlockSpec((D, SEG_OFFS[-1]), lambda i: (0, 0), pipeline_mode=pl.Buffered(1)),
        ],
        out_specs=[pl.BlockSpec((tm, w), lambda i: (i, 0)) for w in SEG_WIDTHS],
        out_shape=outs,
        compiler_params=_cparams(("arbitrary",), 56),
        name="inproj",
    )(x2, pos2, gain, tab, w_cat)


def _split3(a):
    a1 = a.astype(BF16)
    r1 = a - a1.astype(F32)
    a2 = r1.astype(BF16)
    a3 = (r1 - a2.astype(F32)).astype(BF16)
    return a1, a2, a3


def _gla_dir(q_ref, k_ref, v_ref, a_ref, u_ref, bias_ref, o_ref, st_ref, backward, n_chunks):
    C = GLA_CHUNK
    z = jnp.dot(a_ref[...], u_ref[...], preferred_element_type=F32) + bias_ref[...]
    g = (jnp.minimum(z, 0.0) - jnp.log1p(jnp.exp(-jnp.abs(z)))) * (1.0 / GLA_TAU)
    row = lax.broadcasted_iota(I32, (C, C), 0)
    col = lax.broadcasted_iota(I32, (C, C), 1)
    if backward:
        tri = (col >= row).astype(BF16)
        keep = col > row
        ref_row, last_row = C // 2, 0
    else:
        tri = (col <= row).astype(BF16)
        keep = col <= row
        ref_row, last_row = C // 2 - 1, C - 1
    order = range(n_chunks - 1, -1, -1) if backward else range(n_chunks)
    state = st_ref[...]
    for c in order:
        sl = slice(c * C, (c + 1) * C)
        g1, g2, g3 = _split3(g[sl])
        b = (jnp.dot(tri, g1, preferred_element_type=F32) + jnp.dot(tri, g2, preferred_element_type=F32)
             + jnp.dot(tri, g3, preferred_element_type=F32))
        b_ref = b[ref_row:ref_row + 1]
        b_last = b[last_row:last_row + 1]
        q = q_ref[sl, :].astype(F32) * (GLA_DK ** -0.5)
        k = k_ref[sl, :].astype(F32)
        v = v_ref[sl, :]
        qa = (q * jnp.exp(b - b_ref)).astype(BF16)
        ka = (k * jnp.exp(b_ref - b)).astype(BF16)
        s = lax.dot_general(qa, ka, (((1,), (1,)), ((), ())), preferred_element_type=F32)
        s = jnp.where(keep, s, 0.0).astype(BF16)
        o = jnp.dot(s, v, preferred_element_type=F32)
        qi = (q * jnp.exp(b)).astype(BF16)
        o = o + jnp.dot(qi, state.astype(BF16), preferred_element_type=F32)
        o_ref[sl, :] = o.astype(BF16)
        ks = (k * jnp.exp(b_last - b)).T.astype(BF16)
        kv = jnp.dot(ks, v, preferred_element_type=F32)
        dec = jnp.broadcast_to(jnp.exp(b_last), (8, GLA_DK)).T[:, 0:1]
        state = dec * state + kv
    st_ref[...] = state


def _gla_kernel(qf_ref, kf_ref, vf_ref, af_ref, qb_ref, kb_ref, vb_ref, ab_ref, u_ref, bias_ref,
                of_ref, ob_ref, st_ref, *, n_chunks):
    @pl.when(pl.program_id(2) == 0)
    def _():
        st_ref[...] = jnp.zeros_like(st_ref)

    _gla_dir(qf_ref, kf_ref, vf_ref, af_ref, u_ref.at[0], bias_ref.at[0], of_ref, st_ref.at[0], False, n_chunks)
    _gla_dir(qb_ref, kb_ref, vb_ref, ab_ref, u_ref.at[1], bias_ref.at[1], ob_ref, st_ref.at[1], True, n_chunks)


def _gla(qg, kg, vg, al, u_pad, bias, n_chunks=8):
    B, S, _ = qg.shape
    rows = n_chunks * GLA_CHUNK
    NB = S // rows
    fwd = lambda b, h, n: (b, n, h)
    bwd = lambda b, h, n: (b, NB - 1 - n, h)
    fwd0 = lambda b, h, n: (b, n, 0)
    bwd0 = lambda b, h, n: (b, NB - 1 - n, 0)

    def specs(im, im0):
        return [pl.BlockSpec((None, rows, GLA_DK), im), pl.BlockSpec((None, rows, GLA_DK), im),
                pl.BlockSpec((None, rows, GLA_DV), im), pl.BlockSpec((None, rows, LANES), im0)]

    out = jax.ShapeDtypeStruct((B, S, GLA_HEADS * GLA_DV), BF16)
    return pl.pallas_call(
        functools.partial(_gla_kernel, n_chunks=n_chunks),
        grid=(B, GLA_HEADS, NB),
        in_specs=specs(fwd, fwd0) + specs(bwd, bwd0) + [
            pl.BlockSpec((2, LANES, GLA_DK), lambda b, h, n: (0, 0, h)),
            pl.BlockSpec((2, 1, GLA_DK), lambda b, h, n: (0, 0, h)),
        ],
        out_specs=[pl.BlockSpec((None, rows, GLA_DV), fwd), pl.BlockSpec((None, rows, GLA_DV), bwd)],
        out_shape=[out, out],
        scratch_shapes=[pltpu.VMEM((2, GLA_DK, GLA_DV), F32)],
        compiler_params=_cparams(("arbitrary", "arbitrary", "arbitrary"), 32),
        name="gla",
    )(qg, kg, vg, al, qg, kg, vg, al, u_pad, bias)


def _attn_kernel(q_ref, k_ref, v_ref, o_ref, lse_ref, *, lq, win, half):
    L = k_ref.shape[0]
    q0 = pl.program_id(2) * lq
    k0 = pl.multiple_of(jnp.clip(q0 - half, 0, L - win), half)
    qpos = q0 + lax.broadcasted_iota(I32, (lq, 1), 0)
    kpos = k0 + lax.broadcasted_iota(I32, (1, win), 1)
    valid = jnp.abs(qpos - kpos) <= half
    q = q_ref[...]
    k = k_ref[pl.ds(k0, win), :]
    v = v_ref[pl.ds(k0, win), :]
    for h in range(ATT_HPG):
        hs = slice(h * ATT_HD, (h + 1) * ATT_HD)
        s = lax.dot_general(q[:, hs], k[:, hs], (((1,), (1,)), ((), ())), preferred_element_type=F32)
        s = jnp.where(valid, s, -jnp.inf)
        m = jnp.max(s, axis=-1, keepdims=True)
        pr = jnp.exp(s - m)
        l = jnp.sum(pr, axis=-1, keepdims=True)
        o = jnp.dot(pr.astype(BF16), v[:, hs], preferred_element_type=F32) / l
        o_ref[:, hs] = o.astype(BF16)
        lse_ref[:, hs] = jnp.broadcast_to(m + jnp.log(l), (lq, ATT_HD))


def _attn_group(qa, ka, va, gi, dilation, window, lq=256):
    B, S, W = qa.shape
    d = dilation
    L = S // d
    half = window // (2 * d)
    lq = min(lq, L)
    win = min(lq + 2 * half, L)
    ncol = W // ATT_GW
    view = lambda t: t.reshape(B, L, d * W)
    qmap = lambda b, r, i: (b, i, r * ncol + gi)
    kmap = lambda b, r, i: (b, 0, r * ncol + gi)
    omap = lambda b, r, i: (b, i, r)
    o, lse = pl.pallas_call(
        functools.partial(_attn_kernel, lq=lq, win=win, half=half),
        grid=(B, d, L // lq),
        in_specs=[pl.BlockSpec((None, lq, ATT_GW), qmap),
                  pl.BlockSpec((None, L, ATT_GW), kmap),
                  pl.BlockSpec((None, L, ATT_GW), kmap)],
        out_specs=[pl.BlockSpec((None, lq, ATT_GW), omap), pl.BlockSpec((None, lq, ATT_GW), omap)],
        out_shape=[jax.ShapeDtypeStruct((B, L, d * ATT_GW), BF16), jax.ShapeDtypeStruct((B, L, d * ATT_GW), F32)],
        compiler_params=_cparams(("arbitrary", "arbitrary", "arbitrary"), 40),
        name=f"attn_d{d}",
    )(view(qa), view(ka), view(va))
    return o.reshape(B * S, ATT_GW), lse.reshape(B * S, ATT_GW)


def _sigmoid(t):
    return 1.0 / (1.0 + jnp.exp(-t))


def _merge_kernel(of_ref, ob_ref, rg_ref, gn_ref, a0_ref, a1_ref, a2_ref, l0_ref, l1_ref, l2_ref,
                  ga_ref, gb_ref, x_ref, wa_ref, wb_ref, wo_ref, nf_ref, rw_ref, rb_ref,
                  x1_ref, h2_ref, ti_ref, tw_ref):
    o = of_ref[...].astype(F32) + ob_ref[...].astype(F32)
    parts = []
    for h in range(GLA_HEADS):
        oh = o[:, h * GLA_DV:(h + 1) * GLA_DV]
        parts.append(oh * lax.rsqrt(jnp.mean(oh * oh, axis=-1, keepdims=True) + EPS))
    r = rg_ref[...].astype(F32)
    y_gla = jnp.concatenate(parts, axis=-1) * gn_ref[...] * (r * _sigmoid(r))

    l0, l1, l2 = l0_ref[...], l1_ref[...], l2_ref[...]
    m = jnp.maximum(jnp.maximum(l0, l1), l2)
    e0, e1, e2 = jnp.exp(l0 - m), jnp.exp(l1 - m), jnp.exp(l2 - m)
    y_att = (e0 * a0_ref[...].astype(F32) + e1 * a1_ref[...].astype(F32) + e2 * a2_ref[...].astype(F32)) / (e0 + e1 + e2)

    t_gla = jnp.dot(y_gla.astype(BF16), wa_ref[...], preferred_element_type=F32)
    t_att = jnp.dot(y_att.astype(BF16), wb_ref[...], preferred_element_type=F32)
    merged = _sigmoid(ga_ref[...].astype(F32)) * t_gla + _sigmoid(gb_ref[...].astype(F32)) * t_att
    x1 = x_ref[...] + jnp.dot(merged.astype(BF16), wo_ref[...], preferred_element_type=F32)
    x1_ref[...] = x1
    h2 = x1 * lax.rsqrt(jnp.mean(x1 * x1, axis=-1, keepdims=True) + EPS) * nf_ref[...]
    h2_ref[...] = h2

    logits = jnp.dot(h2, rw_ref[...], preferred_element_type=F32, precision=lax.Precision.HIGHEST) + rb_ref[...]
    lane = lax.broadcasted_iota(I32, logits.shape, 1).astype(F32)
    vals, idxs = [], []
    for _ in range(TOP_K):
        mx = jnp.max(logits, axis=-1, keepdims=True)
        ix = jnp.min(jnp.where(logits == mx, lane, float(LANES)), axis=-1, keepdims=True)
        vals.append(mx)
        idxs.append(ix)
        logits = jnp.where(lane == ix, -jnp.inf, logits)
    es = [jnp.exp(vk - vals[0]) for vk in vals]
    den = es[0] + es[1] + es[2] + es[3]
    ti = jnp.zeros(lane.shape, F32)
    tw = jnp.zeros(lane.shape, F32)
    for kk in range(TOP_K):
        ti = jnp.where(lane == float(kk), idxs[kk], ti)
        tw = jnp.where(lane == float(kk), es[kk] / den, tw)
    ti_ref[...] = ti.astype(I32)
    tw_ref[...] = tw


def _merge(o_f, o_b, rg, gn, atts, lses, ga, gb, x2, wa, wb, wo, nf, rw, rb, tm=256):
    T, D = x2.shape
    row = lambda w: pl.BlockSpec((tm, w), lambda i: (i, 0))
    const = lambda a: pl.BlockSpec(a.shape, lambda i: (0, 0))
    return pl.pallas_call(
        _merge_kernel,
        grid=(T // tm,),
        in_specs=[row(D), row(D), row(D), const(gn)] + [row(ATT_GW)] * 6 + [row(D), row(D), row(D),
                  const(wa), const(wb), const(wo), const(nf), const(rw), const(rb)],
        out_specs=[row(D), row(D), row(LANES), row(LANES)],
        out_shape=[jax.ShapeDtypeStruct((T, D), F32), jax.ShapeDtypeStruct((T, D), F32),
                   jax.ShapeDtypeStruct((T, LANES), I32), jax.ShapeDtypeStruct((T, LANES), F32)],
        compiler_params=_cparams(("arbitrary",), 48),
        name="merge_router",
    )(o_f, o_b, rg, gn, *atts, *lses, ga, gb, x2, wa, wb, wo, nf, rw, rb)


def _experts_kernel(te_ref, nu_ref, dst_ref, h_hbm, w1g_ref, w1u_ref, b1g_ref, b1u_ref, w2_ref, b2_ref,
                    y_hbm, xbuf, obuf, sem_g, sem_s, *, tm, n_tok):
    i = pl.program_id(0)
    n_used = nu_ref[0]
    slot = i % 2

    def gather_start(tile, s):
        def body(r, carry):
            tok = dst_ref[tile, r] & (n_tok - 1)
            pltpu.make_async_copy(h_hbm.at[pl.ds(tok, 1), :], xbuf.at[s, pl.ds(r, 1), :], sem_g.at[s]).start()
            return carry
        lax.fori_loop(0, tm, body, 0)

    def gather_wait(s):
        pltpu.make_async_copy(h_hbm.at[pl.ds(0, tm), :], xbuf.at[s], sem_g.at[s]).wait()

    def scatter_start(tile, s):
        def body(r, carry):
            d = dst_ref[tile, r]
            pltpu.make_async_copy(obuf.at[s, pl.ds(r, 1), :], y_hbm.at[pl.ds(d, 1), :], sem_s.at[s]).start()
            return carry
        lax.fori_loop(0, tm, body, 0)

    def scatter_wait(s):
        pltpu.make_async_copy(obuf.at[s], y_hbm.at[pl.ds(0, tm), :], sem_s.at[s]).wait()

    @pl.when(i >= n_used)
    def _():
        obuf[slot] = jnp.zeros((tm, obuf.shape[-1]), F32)
        tail = pltpu.make_async_copy(obuf.at[slot], y_hbm.at[pl.ds(i * tm, tm), :], sem_s.at[slot])
        tail.start()
        tail.wait()

    @pl.when(i < n_used)
    def _():
        @pl.when(i == 0)
        def _():
            gather_start(0, 0)

        @pl.when(i + 1 < n_used)
        def _():
            gather_start(i + 1, 1 - slot)

        gather_wait(slot)
        xt = xbuf[slot].astype(BF16)
        hg = jnp.dot(xt, w1g_ref[...], preferred_element_type=F32) + b1g_ref[...]
        hu = jnp.dot(xt, w1u_ref[...], preferred_element_type=F32) + b1u_ref[...]
        gate = jnp.minimum(hg, SWIGLU_LIMIT)
        up = jnp.clip(hu, -SWIGLU_LIMIT, SWIGLU_LIMIT)
        act = ((up + 1.0) * (gate * _sigmoid(gate * SWIGLU_ALPHA))).astype(BF16)
        out = jnp.dot(act, w2_ref[...], preferred_element_type=F32) + b2_ref[...]

        @pl.when(i >= 2)
        def _():
            scatter_wait(slot)

        obuf[slot] = out
        scatter_start(i, slot)

        @pl.when(i == n_used - 1)
        def _():
            scatter_wait(slot)

            @pl.when(i >= 1)
            def _():
                scatter_wait(1 - slot)


def _experts(te, nu, dst2d, h2, w1g, w1u, b1g, b1u, w2, b2, tm):
    T, D = h2.shape
    n_tiles = dst2d.shape[0]
    wspec = lambda: pl.BlockSpec((None, D, D), lambda i, te, nu, dst: (te[i], 0, 0))
    bspec = lambda: pl.BlockSpec((None, 1, D), lambda i, te, nu, dst: (te[i], 0, 0))
    return pl.pallas_call(
        functools.partial(_experts_kernel, tm=tm, n_tok=T),
        grid_spec=pltpu.PrefetchScalarGridSpec(
            num_scalar_prefetch=3,
            grid=(n_tiles,),
            in_specs=[pl.BlockSpec(memory_space=pl.ANY), wspec(), wspec(), bspec(), bspec(), wspec(), bspec()],
            out_specs=pl.BlockSpec(memory_space=pl.ANY),
            scratch_shapes=[pltpu.VMEM((2, tm, D), F32), pltpu.VMEM((2, tm, D), F32),
                            pltpu.SemaphoreType.DMA((2,)), pltpu.SemaphoreType.DMA((2,))],
        ),
        out_shape=jax.ShapeDtypeStruct((n_tiles * tm, D), F32),
        compiler_params=_cparams(("arbitrary",), 48),
        name="experts",
    )(te, nu, dst2d, h2, w1g, w1u, b1g, b1u, w2, b2)


def _route(topi, tm):
    T = topi.shape[0]
    E = N_EXPERTS
    n_asg = T * TOP_K
    n_tiles = n_asg // tm + E
    e_flat = topi.reshape(n_asg)
    order = jnp.argsort(e_flat, stable=True).astype(I32)
    counts = jnp.sum(e_flat[:, None] == jnp.arange(E, dtype=I32)[None, :], axis=0, dtype=I32)
    padded = (counts + tm - 1) // tm * tm
    pend = jnp.cumsum(padded)
    pstart = pend - padded
    cend = jnp.cumsum(counts)
    cstart = cend - counts
    n_used = pend[-1] // tm
    tile = jnp.arange(n_tiles, dtype=I32)
    te_raw = jnp.sum(tile[:, None] * tm >= pend[None, :], axis=1, dtype=I32)
    last_e = jnp.sum((n_used - 1) * tm >= pend, dtype=I32)
    te = jnp.where(te_raw < E, te_raw, last_e)
    slot = jnp.arange(n_tiles * tm, dtype=I32)
    es = jnp.repeat(te_raw, tm)
    esc = jnp.minimum(es, E - 1)
    j = slot - pstart[esc]
    valid = (es < E) & (j < counts[esc])
    a = order[jnp.clip(cstart[esc] + j, 0, n_asg - 1)]
    dst_valid = (a % TOP_K) * T + a // TOP_K
    cend_ext = jnp.concatenate([cend, jnp.full((1,), n_asg, I32)])
    dst_pad = n_asg + slot - cend_ext[es]
    dst = jnp.where(valid, dst_valid, dst_pad).astype(I32)
    return te, n_used.reshape(1).astype(I32), dst.reshape(n_tiles, tm)


def _final_kernel(x1_ref, y0_ref, y1_ref, y2_ref, y3_ref, tw_ref, p_ref, np_ref, wg_ref, wp_ref, nfin_ref, o_ref):
    tw = tw_ref[...]
    x2 = x1_ref[...]
    for kk, y_ref in enumerate((y0_ref, y1_ref, y2_ref, y3_ref)):
        x2 = x2 + tw[:, kk:kk + 1] * y_ref[...]
    h3 = (x2 * lax.rsqrt(jnp.mean(x2 * x2, axis=-1, keepdims=True) + EPS) * np_ref[...]).astype(BF16)
    gate = _sigmoid(jnp.dot(h3, wg_ref[...], preferred_element_type=F32))
    pe = jnp.dot(p_ref[...].astype(BF16), wp_ref[...], preferred_element_type=F32)
    x3 = x2 + gate * pe
    o_ref[...] = x3 * lax.rsqrt(jnp.mean(x3 * x3, axis=-1, keepdims=True) + EPS) * nfin_ref[...]


def _final(x1, ybuf, tw, p2, n_ple, wg, wp, n_fin, tm=256):
    T, D = x1.shape
    nb = T // tm
    row = lambda w: pl.BlockSpec((tm, w), lambda i: (i, 0))
    const = lambda a: pl.BlockSpec(a.shape, lambda i: (0, 0))
    yspec = lambda kk: pl.BlockSpec((tm, D), lambda i: (kk * nb + i, 0))
    return pl.pallas_call(
        _final_kernel,
        grid=(nb,),
        in_specs=[row(D)] + [yspec(kk) for kk in range(TOP_K)] + [row(LANES), row(p2.shape[1]),
                  const(n_ple), const(wg), const(wp), const(n_fin)],
        out_specs=row(D),
        out_shape=jax.ShapeDtypeStruct((T, D), F32),
        compiler_params=_cparams(("arbitrary",), 48),
        name="final",
    )(x1, ybuf, ybuf, ybuf, ybuf, tw, p2, n_ple, wg, wp, n_fin)


def kernel(x, p, positions, norm_mix, w_in, gla_fgate_up, gla_fgate_bias, gla_out_norm, w_branch_gla, w_branch_attn, w_out, norm_ffn, router_w, router_b, expert_w1, expert_b1, expert_w2, expert_b2, norm_ple, ple_gate_w, ple_proj, norm_final):
    B, S, D = x.shape
    T = B * S
    assert w_in.shape[0] == 1, "single-layer block: the final norm is fused into the layer's last kernel"
    x2 = x.reshape(T, D)
    pos2 = positions.reshape(T, 1)

    w = w_in[0]
    w_cat = jnp.concatenate([
        w[:, :3072],
        jnp.pad(w[:, 3072:3104], ((0, 0), (0, LANES - 2 * GLA_RANK))),
        w[:, 3104:3104 + 768] * (ATT_HD ** -0.5),
        w[:, 3104 + 768:],
    ], axis=1).astype(BF16)
    inv_freq = ROPE_THETA ** (-jnp.arange(0, ROT_DIM, 2, dtype=F32) / ROT_DIM)
    lane_f = (jnp.arange(LANES) % ATT_HD) % (ROT_DIM // 2)
    tab = jnp.zeros((8, LANES), F32).at[0].set(inv_freq[lane_f])
    u_pad = jnp.zeros((2, LANES, GLA_HEADS * GLA_DK), F32)
    u_pad = u_pad.at[0, :GLA_RANK].set(gla_fgate_up[0, 0]).at[1, GLA_RANK:2 * GLA_RANK].set(gla_fgate_up[0, 1])
    u_pad = u_pad.astype(BF16)
    bias = gla_fgate_bias[0].reshape(2, 1, GLA_HEADS * GLA_DK)

    qg, kg, vg, rg, al, qa, ka, va, ga, gb = _inproj(x2, pos2, norm_mix[0].reshape(1, D), tab, w_cat)
    r3 = lambda t: t.reshape(B, S, t.shape[-1])
    o_f, o_b = _gla(r3(qg), r3(kg), r3(vg), r3(al), u_pad, bias)
    atts, lses = [], []
    for gi, (window, dilation) in enumerate(ATT_GROUPS):
        o_g, lse_g = _attn_group(r3(qa), r3(ka), r3(va), gi, dilation, window)
        atts.append(o_g)
        lses.append(lse_g)

    rw = jnp.pad(router_w[0], ((0, 0), (0, LANES - N_EXPERTS)))
    rb = jnp.concatenate([router_b[0], jnp.full((LANES - N_EXPERTS,), -jnp.inf, F32)]).reshape(1, LANES)
    x1, h2, topi, topw = _merge(
        o_f.reshape(T, D), o_b.reshape(T, D), rg, gla_out_norm[0].reshape(1, D), atts, lses, ga, gb, x2,
        w_branch_gla[0].astype(BF16), w_branch_attn[0].astype(BF16), w_out[0].astype(BF16),
        norm_ffn[0].reshape(1, D), rw, rb)

    tm_e = 256
    assert T & (T - 1) == 0, "token count must be a power of two (row index is masked out of the slot code)"
    te, n_used, dst2d = _route(topi[:, :TOP_K], tm_e)
    w1 = expert_w1[0]
    ybuf = _experts(te, n_used, dst2d, h2,
                    w1[:, :, 0::2].astype(BF16), w1[:, :, 1::2].astype(BF16),
                    expert_b1[0][:, None, 0::2], expert_b1[0][:, None, 1::2],
                    expert_w2[0].astype(BF16), expert_b2[0][:, None, :], tm_e)

    out = _final(x1, ybuf, topw, p[0].reshape(T, -1), norm_ple[0].reshape(1, D), ple_gate_w[0].astype(BF16),
                 ple_proj[0].astype(BF16), norm_final.reshape(1, D))
    return out.reshape(B, S, D)
```

```python
import functools

import jax
import jax.numpy as jnp
from jax import lax
from jax.experimental import pallas as pl
from jax.experimental.pallas import tpu as pltpu

F32 = jnp.float32
BF16 = jnp.bfloat16
I32 = jnp.int32

EPS = 1e-6
GLA_HEADS = 4
GLA_DK = 128
GLA_DV = 256
GLA_RANK = 16
GLA_TAU = 16.0
GLA_CHUNK = 64
ATT_GROUPS = ((128, 1), (512, 4), (2048, 16))
ATT_HPG = 4
ATT_HD = 64
ATT_GW = ATT_HPG * ATT_HD
ROT_DIM = 16
ROPE_THETA = 500000.0
N_EXPERTS = 32
TOP_K = 4
SWIGLU_ALPHA = 1.702
SWIGLU_LIMIT = 7.0

LANES = 128
MIB = 1024 * 1024

SEG_WIDTHS = (512, 512, 1024, 1024, LANES, 768, 768, 768, 1024, 1024)
SEG_OFFS = tuple(sum(SEG_WIDTHS[:i]) for i in range(len(SEG_WIDTHS) + 1))


def _cparams(sem, vmem_mib):
    return pltpu.CompilerParams(dimension_semantics=sem, vmem_limit_bytes=vmem_mib * MIB)


def _inproj_kernel(x_ref, pos_ref, g_ref, tab_ref, w_ref, *refs):
    (qg_ref, kg_ref, vg_ref, rg_ref, al_ref, ga_ref, gb_ref), att_refs, ysc = refs[:7], refs[7:16], refs[16]
    x = x_ref[...]
    h = (x * lax.rsqrt(jnp.mean(x * x, axis=-1, keepdims=True) + EPS) * g_ref[...]).astype(BF16)

    def proj(seg):
        return jnp.dot(h, w_ref[:, SEG_OFFS[seg]:SEG_OFFS[seg + 1]], preferred_element_type=F32)

    qg_ref[...] = proj(0).astype(BF16)
    kg_ref[...] = proj(1).astype(BF16)
    vg_ref[...] = proj(2).astype(BF16)
    rg_ref[...] = proj(3).astype(BF16)
    al_ref[...] = proj(4).astype(BF16)
    ga_ref[...] = proj(8).astype(BF16)
    gb_ref[...] = proj(9).astype(BF16)

    ang = pos_ref[...].astype(F32) * tab_ref[0:1, :]
    cs = jnp.cos(ang)
    sn = jnp.sin(ang)
    lane = lax.broadcasted_iota(I32, (1, LANES), 1) % ATT_HD
    c_mul = jnp.where(lane < ROT_DIM, cs, 1.0)
    s_next = jnp.where(lane < ROT_DIM // 2, -sn, 0.0)
    s_prev = jnp.where((lane >= ROT_DIM // 2) & (lane < ROT_DIM), sn, 0.0)

    tm = ysc.shape[1]
    tiles_per_group = ATT_GW // LANES

    def emit(seg, which, rotary):
        y = proj(seg)
        for j in range(SEG_WIDTHS[seg] // LANES):
            t = y[:, j * LANES:(j + 1) * LANES]
            if rotary:
                t = (t * c_mul + pltpu.roll(t, LANES - ROT_DIM // 2, axis=1) * s_next
                     + pltpu.roll(t, ROT_DIM // 2, axis=1) * s_prev)
            ysc[j] = t
        for gi, (_, d) in enumerate(ATT_GROUPS):
            o_ref = att_refs[3 * gi + which]
            for r in range(d):
                for j in range(tiles_per_group):
                    o_ref[r, :, j * LANES:(j + 1) * LANES] = ysc[
                        gi * tiles_per_group + j, pl.ds(r, tm // d, stride=d), :].astype(BF16)

    emit(5, 0, True)
    emit(6, 1, True)
    emit(7, 2, False)


def _inproj(x2, pos2, gain, tab, w_cat, B, tm=512):
    T, D = x2.shape
    S = T // B
    nb = S // tm
    row_widths = [SEG_WIDTHS[s] for s in (0, 1, 2, 3, 4, 8, 9)]
    outs = [jax.ShapeDtypeStruct((T, w), BF16) for w in row_widths]
    out_specs = [pl.BlockSpec((tm, w), lambda i: (i, 0)) for w in row_widths]
    for _, d in ATT_GROUPS:
        for _ in range(3):
            outs.append(jax.ShapeDtypeStruct((B, d, S // d, ATT_GW), BF16))
            out_specs.append(pl.BlockSpec((None, d, tm // d, ATT_GW), lambda i: (i // nb, 0, i % nb, 0)))
    return pl.pallas_call(
        _inproj_kernel,
        grid=(T // tm,),
        in_specs=[
            pl.BlockSpec((tm, D), lambda i: (i, 0)),
            pl.BlockSpec((tm, 1), lambda i: (i, 0)),
            pl.BlockSpec((1, D), lambda i: (0, 0)),
            pl.BlockSpec((8, LANES), lambda i: (0, 0)),
            pl.BlockSpec((D, SEG_OFFS[-1]), lambda i: (0, 0), pipeline_mode=pl.Buffered(1)),
        ],
        out_specs=out_specs,
        out_shape=outs,
        scratch_shapes=[pltpu.VMEM((SEG_WIDTHS[5] // LANES, tm, LANES), F32)],
        compiler_params=_cparams(("arbitrary",), 56),
        name="inproj",
    )(x2, pos2, gain, tab, w_cat)


def _split3(a):
    a1 = a.astype(BF16)
    r1 = a - a1.astype(F32)
    a2 = r1.astype(BF16)
    a3 = (r1 - a2.astype(F32)).astype(BF16)
    return a1, a2, a3


def _gla_dir(q_ref, k_ref, v_ref, a_ref, u_ref, bias_ref, o_ref, st_ref, backward, n_chunks):
    C = GLA_CHUNK
    z = jnp.dot(a_ref[...], u_ref[...], preferred_element_type=F32) + bias_ref[...]
    g = (jnp.minimum(z, 0.0) - jnp.log1p(jnp.exp(-jnp.abs(z)))) * (1.0 / GLA_TAU)
    row = lax.broadcasted_iota(I32, (C, C), 0)
    col = lax.broadcasted_iota(I32, (C, C), 1)
    if backward:
        tri = (col >= row).astype(BF16)
        keep = col > row
        ref_row, last_row = C // 2, 0
    else:
        tri = (col <= row).astype(BF16)
        keep = col <= row
        ref_row, last_row = C // 2 - 1, C - 1
    order = range(n_chunks - 1, -1, -1) if backward else range(n_chunks)
    state = st_ref[...]
    for c in order:
        sl = slice(c * C, (c + 1) * C)
        g1, g2, g3 = _split3(g[sl])
        b = (jnp.dot(tri, g1, preferred_element_type=F32) + jnp.dot(tri, g2, preferred_element_type=F32)
             + jnp.dot(tri, g3, preferred_element_type=F32))
        b_ref = b[ref_row:ref_row + 1]
        b_last = b[last_row:last_row + 1]
        q = q_ref[sl, :].astype(F32) * (GLA_DK ** -0.5)
        k = k_ref[sl, :].astype(F32)
        v = v_ref[sl, :]
        qa = (q * jnp.exp(b - b_ref)).astype(BF16)
        ka = (k * jnp.exp(b_ref - b)).astype(BF16)
        s = lax.dot_general(qa, ka, (((1,), (1,)), ((), ())), preferred_element_type=F32)
        s = jnp.where(keep, s, 0.0).astype(BF16)
        o = jnp.dot(s, v, preferred_element_type=F32)
        qi = (q * jnp.exp(b)).astype(BF16)
        o = o + jnp.dot(qi, state.astype(BF16), preferred_element_type=F32)
        o_ref[sl, :] = o.astype(BF16)
        ks = (k * jnp.exp(b_last - b)).T.astype(BF16)
        kv = jnp.dot(ks, v, preferred_element_type=F32)
        dec = jnp.broadcast_to(jnp.exp(b_last), (8, GLA_DK)).T[:, 0:1]
        state = dec * state + kv
    st_ref[...] = state


def _gla_kernel(qf_ref, kf_ref, vf_ref, af_ref, qb_ref, kb_ref, vb_ref, ab_ref, u_ref, bias_ref,
                of_ref, ob_ref, st_ref, *, n_chunks):
    @pl.when(pl.program_id(2) == 0)
    def _():
        st_ref[...] = jnp.zeros_like(st_ref)

    _gla_dir(qf_ref, kf_ref, vf_ref, af_ref, u_ref.at[0], bias_ref.at[0], of_ref, st_ref.at[0], False, n_chunks)
    _gla_dir(qb_ref, kb_ref, vb_ref, ab_ref, u_ref.at[1], bias_ref.at[1], ob_ref, st_ref.at[1], True, n_chunks)


def _gla(qg, kg, vg, al, u_pad, bias, n_chunks=8):
    B, S, _ = qg.shape
    rows = n_chunks * GLA_CHUNK
    NB = S // rows
    fwd = lambda b, h, n: (b, n, h)
    bwd = lambda b, h, n: (b, NB - 1 - n, h)
    fwd0 = lambda b, h, n: (b, n, 0)
    bwd0 = lambda b, h, n: (b, NB - 1 - n, 0)

    def specs(im, im0):
        return [pl.BlockSpec((None, rows, GLA_DK), im), pl.BlockSpec((None, rows, GLA_DK), im),
                pl.BlockSpec((None, rows, GLA_DV), im), pl.BlockSpec((None, rows, LANES), im0)]

    out = jax.ShapeDtypeStruct((B, S, GLA_HEADS * GLA_DV), BF16)
    return pl.pallas_call(
        functools.partial(_gla_kernel, n_chunks=n_chunks),
        grid=(B, GLA_HEADS, NB),
        in_specs=specs(fwd, fwd0) + specs(bwd, bwd0) + [
            pl.BlockSpec((2, LANES, GLA_DK), lambda b, h, n: (0, 0, h)),
            pl.BlockSpec((2, 1, GLA_DK), lambda b, h, n: (0, 0, h)),
        ],
        out_specs=[pl.BlockSpec((None, rows, GLA_DV), fwd), pl.BlockSpec((None, rows, GLA_DV), bwd)],
        out_shape=[out, out],
        scratch_shapes=[pltpu.VMEM((2, GLA_DK, GLA_DV), F32)],
        compiler_params=_cparams(("arbitrary", "arbitrary", "arbitrary"), 32),
        name="gla",
    )(qg, kg, vg, al, qg, kg, vg, al, u_pad, bias)


def _attn_kernel(q_ref, k_ref, v_ref, o_ref, lse_ref, *, lq, win, half):
    L = k_ref.shape[0]
    q0 = pl.program_id(2) * lq
    k0 = pl.multiple_of(jnp.clip(q0 - half, 0, L - win), half)
    qpos = q0 + lax.broadcasted_iota(I32, (lq, 1), 0)
    kpos = k0 + lax.broadcasted_iota(I32, (1, win), 1)
    valid = jnp.abs(qpos - kpos) <= half
    q = q_ref[...]
    k = k_ref[pl.ds(k0, win), :]
    v = v_ref[pl.ds(k0, win), :]
    for h in range(ATT_HPG):
        hs = slice(h * ATT_HD, (h + 1) * ATT_HD)
        s = lax.dot_general(q[:, hs], k[:, hs], (((1,), (1,)), ((), ())), preferred_element_type=F32)
        s = jnp.where(valid, s, -jnp.inf)
        m = jnp.max(s, axis=-1, keepdims=True)
        pr = jnp.exp(s - m)
        l = jnp.sum(pr, axis=-1, keepdims=True)
        o = jnp.dot(pr.astype(BF16), v[:, hs], preferred_element_type=F32) / l
        o_ref[:, hs] = o.astype(BF16)
        lse_ref[:, hs] = jnp.broadcast_to(m + jnp.log(l), (lq, ATT_HD))


def _attn_group(q, k, v, window, lq=256):
    B, d, L, _ = q.shape
    half = window // (2 * d)
    lq = min(lq, L)
    win = min(lq + 2 * half, L)
    qmap = lambda b, r, i: (b, r, i, 0)
    kmap = lambda b, r, i: (b, r, 0, 0)
    return pl.pallas_call(
        functools.partial(_attn_kernel, lq=lq, win=win, half=half),
        grid=(B, d, L // lq),
        in_specs=[pl.BlockSpec((None, None, lq, ATT_GW), qmap),
                  pl.BlockSpec((None, None, L, ATT_GW), kmap),
                  pl.BlockSpec((None, None, L, ATT_GW), kmap)],
        out_specs=[pl.BlockSpec((None, None, lq, ATT_GW), qmap), pl.BlockSpec((None, None, lq, ATT_GW), qmap)],
        out_shape=[jax.ShapeDtypeStruct(q.shape, BF16), jax.ShapeDtypeStruct(q.shape, F32)],
        compiler_params=_cparams(("arbitrary", "arbitrary", "arbitrary"), 40),
        name=f"attn_d{d}",
    )(q, k, v)


def _sigmoid(t):
    return 1.0 / (1.0 + jnp.exp(-t))


def _merge_kernel(of_ref, ob_ref, rg_ref, gn_ref, a0_ref, a1_ref, a2_ref, l0_ref, l1_ref, l2_ref,
                  ga_ref, gb_ref, x_ref, wa_ref, wb_ref, wo_ref, nf_ref, rw_ref, rb_ref,
                  x1_ref, h2_ref, ti_ref, tw_ref, *scratch):
    def token_rows(blk_ref, scr):
        d, n, w = blk_ref.shape
        if d == 1:
            return blk_ref[0].astype(F32)
        for r in range(d):
            for j in range(w // LANES):
                scr[j, pl.ds(r, n, stride=d), :] = blk_ref[r, :, j * LANES:(j + 1) * LANES].astype(F32)
        return jnp.concatenate([scr[j] for j in range(w // LANES)], axis=-1)

    o = of_ref[...].astype(F32) + ob_ref[...].astype(F32)
    parts = []
    for h in range(GLA_HEADS):
        oh = o[:, h * GLA_DV:(h + 1) * GLA_DV]
        parts.append(oh * lax.rsqrt(jnp.mean(oh * oh, axis=-1, keepdims=True) + EPS))
    r = rg_ref[...].astype(F32)
    y_gla = jnp.concatenate(parts, axis=-1) * gn_ref[...] * (r * _sigmoid(r))

    l0, l1, l2 = (token_rows(l_ref, scr) for l_ref, scr in zip((l0_ref, l1_ref, l2_ref), scratch[0:3]))
    a0, a1, a2 = (token_rows(a_ref, scr) for a_ref, scr in zip((a0_ref, a1_ref, a2_ref), scratch[3:6]))
    m = jnp.maximum(jnp.maximum(l0, l1), l2)
    e0, e1, e2 = jnp.exp(l0 - m), jnp.exp(l1 - m), jnp.exp(l2 - m)
    y_att = (e0 * a0 + e1 * a1 + e2 * a2) / (e0 + e1 + e2)

    t_gla = jnp.dot(y_gla.astype(BF16), wa_ref[...], preferred_element_type=F32)
    t_att = jnp.dot(y_att.astype(BF16), wb_ref[...], preferred_element_type=F32)
    merged = _sigmoid(ga_ref[...].astype(F32)) * t_gla + _sigmoid(gb_ref[...].astype(F32)) * t_att
    x1 = x_ref[...] + jnp.dot(merged.astype(BF16), wo_ref[...], preferred_element_type=F32)
    x1_ref[...] = x1
    h2 = x1 * lax.rsqrt(jnp.mean(x1 * x1, axis=-1, keepdims=True) + EPS) * nf_ref[...]
    h2_ref[...] = h2

    logits = jnp.dot(h2, rw_ref[...], preferred_element_type=F32, precision=lax.Precision.HIGHEST) + rb_ref[...]
    lane = lax.broadcasted_iota(I32, logits.shape, 1).astype(F32)
    vals, idxs = [], []
    for _ in range(TOP_K):
        mx = jnp.max(logits, axis=-1, keepdims=True)
        ix = jnp.min(jnp.where(logits == mx, lane, float(LANES)), axis=-1, keepdims=True)
        vals.append(mx)
        idxs.append(ix)
        logits = jnp.where(lane == ix, -jnp.inf, logits)
    es = [jnp.exp(vk - vals[0]) for vk in vals]
    den = es[0] + es[1] + es[2] + es[3]
    ti = jnp.zeros(lane.shape, F32)
    tw = jnp.zeros(lane.shape, F32)
    for kk in range(TOP_K):
        ti = jnp.where(lane == float(kk), idxs[kk], ti)
        tw = jnp.where(lane == float(kk), es[kk] / den, tw)
    ti_ref[...] = ti.astype(I32)
    tw_ref[...] = tw


def _merge(o_f, o_b, rg, gn, atts, lses, ga, gb, x2, wa, wb, wo, nf, rw, rb, tm=256):
    T, D = x2.shape
    B = atts[0].shape[0]
    nb = T // B // tm
    row = lambda w: pl.BlockSpec((tm, w), lambda i: (i, 0))
    const = lambda a: pl.BlockSpec(a.shape, lambda i: (0, 0))
    res = lambda a: pl.BlockSpec((None, a.shape[1], tm // a.shape[1], ATT_GW), lambda i: (i // nb, 0, i % nb, 0))
    return pl.pallas_call(
        _merge_kernel,
        grid=(T // tm,),
        in_specs=[row(D), row(D), row(D), const(gn)] + [res(a) for a in atts] + [res(a) for a in lses]
                 + [row(D), row(D), row(D), const(wa), const(wb), const(wo), const(nf), const(rw), const(rb)],
        out_specs=[row(D), row(D), row(LANES), row(LANES)],
        out_shape=[jax.ShapeDtypeStruct((T, D), F32), jax.ShapeDtypeStruct((T, D), F32),
                   jax.ShapeDtypeStruct((T, LANES), I32), jax.ShapeDtypeStruct((T, LANES), F32)],
        scratch_shapes=[pltpu.VMEM((ATT_GW // LANES, tm, LANES), F32)] * 6,
        compiler_params=_cparams(("arbitrary",), 48),
        name="merge_router",
    )(o_f, o_b, rg, gn, *atts, *lses, ga, gb, x2, wa, wb, wo, nf, rw, rb)


DMA_UNROLL = 16


def _experts_kernel(te_ref, nu_ref, dst_ref, h_hbm, w1_ref, b1_ref, w2_ref, b2_ref,
                    y_hbm, xbuf, obuf, w1p, w2b, sem_g, sem_s, *, tm, n_tok):
    i = pl.program_id(0)
    n_used = nu_ref[0]
    slot = i % 2
    D = w2b.shape[0]

    def gather_start(tile, s):
        def body(c, carry):
            for u in range(DMA_UNROLL):
                r = c * DMA_UNROLL + u
                tok = dst_ref[tile, r] & (n_tok - 1)
                pltpu.make_async_copy(h_hbm.at[pl.ds(tok, 1), :], xbuf.at[s, pl.ds(r, 1), :], sem_g.at[s]).start()
            return carry
        lax.fori_loop(0, tm // DMA_UNROLL, body, 0)

    def gather_wait(s):
        pltpu.make_async_copy(h_hbm.at[pl.ds(0, tm), :], xbuf.at[s], sem_g.at[s]).wait()

    def scatter_start(tile, s):
        def body(c, carry):
            for u in range(DMA_UNROLL):
                r = c * DMA_UNROLL + u
                d = dst_ref[tile, r]
                pltpu.make_async_copy(obuf.at[s, pl.ds(r, 1), :], y_hbm.at[pl.ds(d, 1), :], sem_s.at[s]).start()
            return carry
        lax.fori_loop(0, tm // DMA_UNROLL, body, 0)

    def scatter_wait(s):
        pltpu.make_async_copy(obuf.at[s], y_hbm.at[pl.ds(0, tm), :], sem_s.at[s]).wait()

    @pl.when(i >= n_used)
    def _():
        obuf[slot] = jnp.zeros((tm, obuf.shape[-1]), F32)
        tail = pltpu.make_async_copy(obuf.at[slot], y_hbm.at[pl.ds(i * tm, tm), :], sem_s.at[slot])
        tail.start()
        tail.wait()

    @pl.when(i < n_used)
    def _():
        @pl.when(i == 0)
        def _():
            gather_start(0, 0)

        @pl.when(i + 1 < n_used)
        def _():
            gather_start(i + 1, 1 - slot)

        @pl.when((i == 0) | (te_ref[i] != te_ref[jnp.maximum(i - 1, 0)]))
        def _():
            kk = lax.broadcasted_iota(I32, (2 * LANES, 2 * LANES), 0)
            nn = lax.broadcasted_iota(I32, (2 * LANES, 2 * LANES), 1)
            perm = (kk == jnp.where(nn < LANES, 2 * nn, 2 * (nn - LANES) + 1)).astype(BF16)
            for c in range(w1p.shape[1] // (2 * LANES)):
                cs = slice(c * 2 * LANES, (c + 1) * 2 * LANES)
                w1p[:, cs] = jnp.dot(w1_ref[:, cs].astype(BF16), perm, preferred_element_type=F32).astype(BF16)
            w2b[...] = w2_ref[...].astype(BF16)

        gather_wait(slot)
        xt = xbuf[slot].astype(BF16)
        hid = jnp.dot(xt, w1p[...], preferred_element_type=F32) + b1_ref[...]
        acts = []
        for c in range(D // LANES):
            gate = jnp.minimum(hid[:, 2 * c * LANES:(2 * c + 1) * LANES], SWIGLU_LIMIT)
            up = jnp.clip(hid[:, (2 * c + 1) * LANES:(2 * c + 2) * LANES], -SWIGLU_LIMIT, SWIGLU_LIMIT)
            acts.append(((up + 1.0) * (gate * _sigmoid(gate * SWIGLU_ALPHA))).astype(BF16))
        act = jnp.concatenate(acts, axis=-1)
        out = jnp.dot(act, w2b[...], preferred_element_type=F32) + b2_ref[...]

        @pl.when(i >= 2)
        def _():
            scatter_wait(slot)

        obuf[slot] = out
        scatter_start(i, slot)

        @pl.when(i == n_used - 1)
        def _():
            scatter_wait(slot)

            @pl.when(i >= 1)
            def _():
                scatter_wait(1 - slot)


def _experts(te, nu, dst2d, h2, w1, b1p, w2, b2, tm):
    T, D = h2.shape
    n_tiles = dst2d.shape[0]
    F = w1.shape[-1]
    by_expert = lambda r, c: pl.BlockSpec((None, r, c), lambda i, te, nu, dst: (te[i], 0, 0))
    return pl.pallas_call(
        functools.partial(_experts_kernel, tm=tm, n_tok=T),
        grid_spec=pltpu.PrefetchScalarGridSpec(
            num_scalar_prefetch=3,
            grid=(n_tiles,),
            in_specs=[pl.BlockSpec(memory_space=pl.ANY), by_expert(D, F), by_expert(1, F),
                      by_expert(F // 2, D), by_expert(1, D)],
            out_specs=pl.BlockSpec(memory_space=pl.ANY),
            scratch_shapes=[pltpu.VMEM((2, tm, D), F32), pltpu.VMEM((2, tm, D), F32),
                            pltpu.VMEM((D, F), BF16), pltpu.VMEM((F // 2, D), BF16),
                            pltpu.SemaphoreType.DMA((2,)), pltpu.SemaphoreType.DMA((2,))],
        ),
        out_shape=jax.ShapeDtypeStruct((n_tiles * tm, D), F32),
        compiler_params=pltpu.CompilerParams(dimension_semantics=("arbitrary",), vmem_limit_bytes=56 * MIB,
                                             disable_bounds_checks=True),
        name="experts",
    )(te, nu, dst2d, h2, w1, b1p, w2, b2)


def _route(topi, tm):
    T = topi.shape[0]
    E = N_EXPERTS
    n_asg = T * TOP_K
    n_tiles = n_asg // tm + E
    e_flat = topi.reshape(n_asg)
    order = jnp.argsort(e_flat, stable=True).astype(I32)
    counts = jnp.sum(e_flat[:, None] == jnp.arange(E, dtype=I32)[None, :], axis=0, dtype=I32)
    padded = (counts + tm - 1) // tm * tm
    pend = jnp.cumsum(padded)
    pstart = pend - padded
    cend = jnp.cumsum(counts)
    cstart = cend - counts
    n_used = pend[-1] // tm
    tile = jnp.arange(n_tiles, dtype=I32)
    te_raw = jnp.sum(tile[:, None] * tm >= pend[None, :], axis=1, dtype=I32)
    last_e = jnp.sum((n_used - 1) * tm >= pend, dtype=I32)
    te = jnp.where(te_raw < E, te_raw, last_e)
    slot = jnp.arange(n_tiles * tm, dtype=I32)
    es = jnp.repeat(te_raw, tm)
    esc = jnp.minimum(es, E - 1)
    j = slot - pstart[esc]
    valid = (es < E) & (j < counts[esc])
    a = order[jnp.clip(cstart[esc] + j, 0, n_asg - 1)]
    dst_valid = (a % TOP_K) * T + a // TOP_K
    cend_ext = jnp.concatenate([cend, jnp.full((1,), n_asg, I32)])
    dst_pad = n_asg + slot - cend_ext[es]
    dst = jnp.where(valid, dst_valid, dst_pad).astype(I32)
    return te, n_used.reshape(1).astype(I32), dst.reshape(n_tiles, tm)


def _final_kernel(x1_ref, y0_ref, y1_ref, y2_ref, y3_ref, tw_ref, p_ref, np_ref, wg_ref, wp_ref, nfin_ref, o_ref):
    tw = tw_ref[...]
    x2 = x1_ref[...]
    for kk, y_ref in enumerate((y0_ref, y1_ref, y2_ref, y3_ref)):
        x2 = x2 + tw[:, kk:kk + 1] * y_ref[...]
    h3 = (x2 * lax.rsqrt(jnp.mean(x2 * x2, axis=-1, keepdims=True) + EPS) * np_ref[...]).astype(BF16)
    gate = _sigmoid(jnp.dot(h3, wg_ref[...], preferred_element_type=F32))
    pe = jnp.dot(p_ref[...].astype(BF16), wp_ref[...], preferred_element_type=F32)
    x3 = x2 + gate * pe
    o_ref[...] = x3 * lax.rsqrt(jnp.mean(x3 * x3, axis=-1, keepdims=True) + EPS) * nfin_ref[...]


def _final(x1, ybuf, tw, p2, n_ple, wg, wp, n_fin, tm=256):
    T, D = x1.shape
    nb = T // tm
    row = lambda w: pl.BlockSpec((tm, w), lambda i: (i, 0))
    const = lambda a: pl.BlockSpec(a.shape, lambda i: (0, 0))
    yspec = lambda kk: pl.BlockSpec((tm, D), lambda i: (kk * nb + i, 0))
    return pl.pallas_call(
        _final_kernel,
        grid=(nb,),
        in_specs=[row(D)] + [yspec(kk) for kk in range(TOP_K)] + [row(LANES), row(p2.shape[1]),
                  const(n_ple), const(wg), const(wp), const(n_fin)],
        out_specs=row(D),
        out_shape=jax.ShapeDtypeStruct((T, D), F32),
        compiler_params=_cparams(("arbitrary",), 48),
        name="final",
    )(x1, ybuf, ybuf, ybuf, ybuf, tw, p2, n_ple, wg, wp, n_fin)


def kernel(x, p, positions, norm_mix, w_in, gla_fgate_up, gla_fgate_bias, gla_out_norm, w_branch_gla, w_branch_attn, w_out, norm_ffn, router_w, router_b, expert_w1, expert_b1, expert_w2, expert_b2, norm_ple, ple_gate_w, ple_proj, norm_final):
    B, S, D = x.shape
    T = B * S
    assert w_in.shape[0] == 1, "single-layer block: the final norm is fused into the layer's last kernel"
    x2 = x.reshape(T, D)
    pos2 = positions.reshape(T, 1)

    w = w_in[0]
    w_cat = jnp.concatenate([
        w[:, :3072],
        jnp.pad(w[:, 3072:3104], ((0, 0), (0, LANES - 2 * GLA_RANK))),
        w[:, 3104:3104 + 768] * (ATT_HD ** -0.5),
        w[:, 3104 + 768:],
    ], axis=1).astype(BF16)
    inv_freq = ROPE_THETA ** (-jnp.arange(0, ROT_DIM, 2, dtype=F32) / ROT_DIM)
    lane_f = (jnp.arange(LANES) % ATT_HD) % (ROT_DIM // 2)
    tab = jnp.zeros((8, LANES), F32).at[0].set(inv_freq[lane_f])
    u_pad = jnp.zeros((2, LANES, GLA_HEADS * GLA_DK), F32)
    u_pad = u_pad.at[0, :GLA_RANK].set(gla_fgate_up[0, 0]).at[1, GLA_RANK:2 * GLA_RANK].set(gla_fgate_up[0, 1])
    u_pad = u_pad.astype(BF16)
    bias = gla_fgate_bias[0].reshape(2, 1, GLA_HEADS * GLA_DK)

    qg, kg, vg, rg, al, ga, gb, *att_in = _inproj(x2, pos2, norm_mix[0].reshape(1, D), tab, w_cat, B)
    r3 = lambda t: t.reshape(B, S, t.shape[-1])
    o_f, o_b = _gla(r3(qg), r3(kg), r3(vg), r3(al), u_pad, bias)
    atts, lses = [], []
    for gi, (window, _) in enumerate(ATT_GROUPS):
        o_g, lse_g = _attn_group(*att_in[3 * gi:3 * gi + 3], window)
        atts.append(o_g)
        lses.append(lse_g)

    rw = jnp.pad(router_w[0], ((0, 0), (0, LANES - N_EXPERTS)))
    rb = jnp.concatenate([router_b[0], jnp.full((LANES - N_EXPERTS,), -jnp.inf, F32)]).reshape(1, LANES)
    x1, h2, topi, topw = _merge(
        o_f.reshape(T, D), o_b.reshape(T, D), rg, gla_out_norm[0].reshape(1, D), atts, lses, ga, gb, x2,
        w_branch_gla[0].astype(BF16), w_branch_attn[0].astype(BF16), w_out[0].astype(BF16),
        norm_ffn[0].reshape(1, D), rw, rb)

    tm_e = 256
    assert T & (T - 1) == 0, "token count must be a power of two (row index is masked out of the slot code)"
    te, n_used, dst2d = _route(topi[:, :TOP_K], tm_e)
    E, _, F = expert_w1[0].shape
    b1p = expert_b1[0].reshape(E, F // (2 * LANES), LANES, 2).transpose(0, 1, 3, 2).reshape(E, 1, F)
    ybuf = _experts(te, n_used, dst2d, h2, expert_w1[0], b1p, expert_w2[0], expert_b2[0][:, None, :], tm_e)

    out = _final(x1, ybuf, topw, p[0].reshape(T, -1), norm_ple[0].reshape(1, D), ple_gate_w[0].astype(BF16),
                 ple_proj[0].astype(BF16), norm_final.reshape(1, D))
    return out.reshape(B, S, D)
```

```python
import functools

import jax
import jax.numpy as jnp
from jax import lax
from jax.experimental import pallas as pl
from jax.experimental.pallas import tpu as pltpu

F32 = jnp.float32
BF16 = jnp.bfloat16
I32 = jnp.int32

EPS = 1e-6
GLA_HEADS = 4
GLA_DK = 128
GLA_DV = 256
GLA_RANK = 16
GLA_TAU = 16.0
GLA_CHUNK = 64
ATT_GROUPS = ((128, 1), (512, 4), (2048, 16))
ATT_HPG = 4
ATT_HD = 64
ATT_GW = ATT_HPG * ATT_HD
ROT_DIM = 16
ROPE_THETA = 500000.0
N_EXPERTS = 32
TOP_K = 4
SWIGLU_ALPHA = 1.702
SWIGLU_LIMIT = 7.0

LANES = 128
MIB = 1024 * 1024

SEG_WIDTHS = (512, 512, 1024, 1024, LANES, 768, 768, 768, 1024, 1024)
SEG_OFFS = tuple(sum(SEG_WIDTHS[:i]) for i in range(len(SEG_WIDTHS) + 1))


def _cparams(sem, vmem_mib):
    return pltpu.CompilerParams(dimension_semantics=sem, vmem_limit_bytes=vmem_mib * MIB)


def _inproj_kernel(x_ref, pos_ref, g_ref, tab_ref, w_ref, *refs):
    (qg_ref, kg_ref, vg_ref, rg_ref, al_ref, ga_ref, gb_ref), att_refs, ysc = refs[:7], refs[7:16], refs[16]
    x = x_ref[...]
    h = (x * lax.rsqrt(jnp.mean(x * x, axis=-1, keepdims=True) + EPS) * g_ref[...]).astype(BF16)

    def proj(seg):
        return jnp.dot(h, w_ref[:, SEG_OFFS[seg]:SEG_OFFS[seg + 1]], preferred_element_type=F32)

    qg_ref[...] = proj(0).astype(BF16)
    kg_ref[...] = proj(1).astype(BF16)
    vg_ref[...] = proj(2).astype(BF16)
    rg_ref[...] = proj(3).astype(BF16)
    al_ref[...] = proj(4).astype(BF16)
    ga_ref[...] = proj(8).astype(BF16)
    gb_ref[...] = proj(9).astype(BF16)

    ang = pos_ref[...].astype(F32) * tab_ref[0:1, :]
    cs = jnp.cos(ang)
    sn = jnp.sin(ang)
    lane = lax.broadcasted_iota(I32, (1, LANES), 1) % ATT_HD
    c_mul = jnp.where(lane < ROT_DIM, cs, 1.0)
    s_next = jnp.where(lane < ROT_DIM // 2, -sn, 0.0)
    s_prev = jnp.where((lane >= ROT_DIM // 2) & (lane < ROT_DIM), sn, 0.0)

    tm = ysc.shape[1]
    tiles_per_group = ATT_GW // LANES

    def emit(seg, which, rotary):
        y = proj(seg)
        for j in range(SEG_WIDTHS[seg] // LANES):
            t = y[:, j * LANES:(j + 1) * LANES]
            if rotary:
                t = (t * c_mul + pltpu.roll(t, LANES - ROT_DIM // 2, axis=1) * s_next
                     + pltpu.roll(t, ROT_DIM // 2, axis=1) * s_prev)
            ysc[j] = t
        for gi, (_, d) in enumerate(ATT_GROUPS):
            o_ref = att_refs[3 * gi + which]
            for r in range(d):
                for j in range(tiles_per_group):
                    o_ref[r, :, j * LANES:(j + 1) * LANES] = ysc[
                        gi * tiles_per_group + j, pl.ds(r, tm // d, stride=d), :].astype(BF16)

    emit(5, 0, True)
    emit(6, 1, True)
    emit(7, 2, False)


def _inproj(x2, pos2, gain, tab, w_cat, B, tm=512):
    T, D = x2.shape
    S = T // B
    nb = S // tm
    row_widths = [SEG_WIDTHS[s] for s in (0, 1, 2, 3, 4, 8, 9)]
    outs = [jax.ShapeDtypeStruct((T, w), BF16) for w in row_widths]
    out_specs = [pl.BlockSpec((tm, w), lambda i: (i, 0)) for w in row_widths]
    for _, d in ATT_GROUPS:
        for _ in range(3):
            outs.append(jax.ShapeDtypeStruct((B, d, S // d, ATT_GW), BF16))
            out_specs.append(pl.BlockSpec((None, d, tm // d, ATT_GW), lambda i: (i // nb, 0, i % nb, 0)))
    return pl.pallas_call(
        _inproj_kernel,
        grid=(T // tm,),
        in_specs=[
            pl.BlockSpec((tm, D), lambda i: (i, 0)),
            pl.BlockSpec((tm, 1), lambda i: (i, 0)),
            pl.BlockSpec((1, D), lambda i: (0, 0)),
            pl.BlockSpec((8, LANES), lambda i: (0, 0)),
            pl.BlockSpec((D, SEG_OFFS[-1]), lambda i: (0, 0), pipeline_mode=pl.Buffered(1)),
        ],
        out_specs=out_specs,
        out_shape=outs,
        scratch_shapes=[pltpu.VMEM((SEG_WIDTHS[5] // LANES, tm, LANES), F32)],
        compiler_params=_cparams(("arbitrary",), 56),
        name="inproj",
    )(x2, pos2, gain, tab, w_cat)


def _split3(a):
    a1 = a.astype(BF16)
    r1 = a - a1.astype(F32)
    a2 = r1.astype(BF16)
    a3 = (r1 - a2.astype(F32)).astype(BF16)
    return a1, a2, a3


def _gla_chunk(q_ref, k_ref, v_ref, o_ref, g_ref, st_ref, r0, backward):
    C = GLA_CHUNK
    row = lax.broadcasted_iota(I32, (C, C), 0)
    col = lax.broadcasted_iota(I32, (C, C), 1)
    if backward:
        tri = (col >= row).astype(BF16)
        keep = col > row
        ref_row, last_row = C // 2, 0
    else:
        tri = (col <= row).astype(BF16)
        keep = col <= row
        ref_row, last_row = C // 2 - 1, C - 1
    rows = pl.ds(r0, C)
    g1, g2, g3 = _split3(g_ref[rows, :])
    b = (jnp.dot(tri, g1, preferred_element_type=F32) + jnp.dot(tri, g2, preferred_element_type=F32)
         + jnp.dot(tri, g3, preferred_element_type=F32))
    b_ref = b[ref_row:ref_row + 1]
    b_last = b[last_row:last_row + 1]
    q = q_ref[rows, :].astype(F32) * (GLA_DK ** -0.5)
    k = k_ref[rows, :].astype(F32)
    qa = (q * jnp.exp(b - b_ref)).astype(BF16)
    ka = (k * jnp.exp(b_ref - b)).astype(BF16)
    qi = (q * jnp.exp(b)).astype(BF16)
    ks = k * jnp.exp(b_last - b)
    dec = jnp.broadcast_to(jnp.exp(b_last), (8, b.shape[1])).T
    for h in range(GLA_HEADS):
        hk = slice(h * GLA_DK, (h + 1) * GLA_DK)
        hv = slice(h * GLA_DV, (h + 1) * GLA_DV)
        v = v_ref[rows, hv]
        state = st_ref[h]
        s = lax.dot_general(qa[:, hk], ka[:, hk], (((1,), (1,)), ((), ())), preferred_element_type=F32)
        s = jnp.where(keep, s, 0.0).astype(BF16)
        o = (jnp.dot(s, v, preferred_element_type=F32)
             + jnp.dot(qi[:, hk], state.astype(BF16), preferred_element_type=F32))
        o_ref[rows, hv] = o.astype(BF16)
        kv = jnp.dot(ks[:, hk].T.astype(BF16), v, preferred_element_type=F32)
        st_ref[h] = dec[hk, 0:1] * state + kv


def _gla_kernel(qf_ref, kf_ref, vf_ref, af_ref, qb_ref, kb_ref, vb_ref, ab_ref, u_ref, bias_ref,
                of_ref, ob_ref, st_ref, g_scr, *, n_chunks):
    @pl.when(pl.program_id(1) == 0)
    def _():
        st_ref[...] = jnp.zeros_like(st_ref)

    for d, a_ref in enumerate((af_ref, ab_ref)):
        z = jnp.dot(a_ref[...], u_ref[d], preferred_element_type=F32) + bias_ref[d]
        g_scr[d] = (jnp.minimum(z, 0.0) - jnp.log(1.0 + jnp.exp(-jnp.abs(z)))) * (1.0 / GLA_TAU)

    def body(c, carry):
        rf = pl.multiple_of(c * GLA_CHUNK, GLA_CHUNK)
        rb = pl.multiple_of((n_chunks - 1 - c) * GLA_CHUNK, GLA_CHUNK)
        _gla_chunk(qf_ref, kf_ref, vf_ref, of_ref, g_scr.at[0], st_ref.at[0], rf, False)
        _gla_chunk(qb_ref, kb_ref, vb_ref, ob_ref, g_scr.at[1], st_ref.at[1], rb, True)
        return carry

    lax.fori_loop(0, n_chunks, body, 0)


def _gla(qg, kg, vg, al, u_pad, bias, n_chunks=16):
    B, S, _ = qg.shape
    rows = n_chunks * GLA_CHUNK
    NB = S // rows
    fwd = lambda b, n: (b, n, 0)
    bwd = lambda b, n: (b, NB - 1 - n, 0)
    specs = lambda im: [pl.BlockSpec((None, rows, t.shape[-1]), im) for t in (qg, kg, vg, al)]
    out = jax.ShapeDtypeStruct((B, S, GLA_HEADS * GLA_DV), BF16)
    return pl.pallas_call(
        functools.partial(_gla_kernel, n_chunks=n_chunks),
        grid=(B, NB),
        in_specs=specs(fwd) + specs(bwd) + [
            pl.BlockSpec(u_pad.shape, lambda b, n: (0, 0, 0)),
            pl.BlockSpec(bias.shape, lambda b, n: (0, 0, 0)),
        ],
        out_specs=[pl.BlockSpec((None, rows, GLA_HEADS * GLA_DV), fwd),
                   pl.BlockSpec((None, rows, GLA_HEADS * GLA_DV), bwd)],
        out_shape=[out, out],
        scratch_shapes=[pltpu.VMEM((2, GLA_HEADS, GLA_DK, GLA_DV), F32),
                        pltpu.VMEM((2, rows, GLA_HEADS * GLA_DK), F32)],
        compiler_params=_cparams(("arbitrary", "arbitrary"), 48),
        name="gla",
    )(qg, kg, vg, al, qg, kg, vg, al, u_pad, bias)


def _attn_kernel(q_ref, k_ref, v_ref, o_ref, lse_ref, *, lq, win, half):
    L = k_ref.shape[0]
    q0 = pl.program_id(2) * lq
    k0 = pl.multiple_of(jnp.clip(q0 - half, 0, L - win), half)
    qpos = q0 + lax.broadcasted_iota(I32, (lq, 1), 0)
    kpos = k0 + lax.broadcasted_iota(I32, (1, win), 1)
    valid = jnp.abs(qpos - kpos) <= half
    q = q_ref[...]
    k = k_ref[pl.ds(k0, win), :]
    v = v_ref[pl.ds(k0, win), :]
    for h in range(ATT_HPG):
        hs = slice(h * ATT_HD, (h + 1) * ATT_HD)
        s = lax.dot_general(q[:, hs], k[:, hs], (((1,), (1,)), ((), ())), preferred_element_type=F32)
        s = jnp.where(valid, s, -jnp.inf)
        m = jnp.max(s, axis=-1, keepdims=True)
        pr = jnp.exp(s - m)
        l = jnp.sum(pr, axis=-1, keepdims=True)
        o = jnp.dot(pr.astype(BF16), v[:, hs], preferred_element_type=F32) / l
        o_ref[:, hs] = o.astype(BF16)
        lse_ref[:, hs] = jnp.broadcast_to(m + jnp.log(l), (lq, ATT_HD))


def _attn_group(q, k, v, window, lq=256):
    B, d, L, _ = q.shape
    half = window // (2 * d)
    lq = min(lq, L)
    win = min(lq + 2 * half, L)
    qmap = lambda b, r, i: (b, r, i, 0)
    kmap = lambda b, r, i: (b, r, 0, 0)
    return pl.pallas_call(
        functools.partial(_attn_kernel, lq=lq, win=win, half=half),
        grid=(B, d, L // lq),
        in_specs=[pl.BlockSpec((None, None, lq, ATT_GW), qmap),
                  pl.BlockSpec((None, None, L, ATT_GW), kmap),
                  pl.BlockSpec((None, None, L, ATT_GW), kmap)],
        out_specs=[pl.BlockSpec((None, None, lq, ATT_GW), qmap), pl.BlockSpec((None, None, lq, ATT_GW), qmap)],
        out_shape=[jax.ShapeDtypeStruct(q.shape, BF16), jax.ShapeDtypeStruct(q.shape, F32)],
        compiler_params=_cparams(("arbitrary", "arbitrary", "arbitrary"), 40),
        name=f"attn_d{d}",
    )(q, k, v)


def _sigmoid(t):
    return 1.0 / (1.0 + jnp.exp(-t))


def _merge_kernel(of_ref, ob_ref, rg_ref, gn_ref, a0_ref, a1_ref, a2_ref, l0_ref, l1_ref, l2_ref,
                  ga_ref, gb_ref, x_ref, wa_ref, wb_ref, wo_ref, nf_ref, rw_ref, rb_ref,
                  x1_ref, h2_ref, ti_ref, tw_ref, *scratch):
    def token_rows(blk_ref, scr):
        d, n, w = blk_ref.shape
        if d == 1:
            return blk_ref[0].astype(F32)
        for r in range(d):
            for j in range(w // LANES):
                scr[j, pl.ds(r, n, stride=d), :] = blk_ref[r, :, j * LANES:(j + 1) * LANES].astype(F32)
        return jnp.concatenate([scr[j] for j in range(w // LANES)], axis=-1)

    o = of_ref[...].astype(F32) + ob_ref[...].astype(F32)
    parts = []
    for h in range(GLA_HEADS):
        oh = o[:, h * GLA_DV:(h + 1) * GLA_DV]
        parts.append(oh * lax.rsqrt(jnp.mean(oh * oh, axis=-1, keepdims=True) + EPS))
    r = rg_ref[...].astype(F32)
    y_gla = jnp.concatenate(parts, axis=-1) * gn_ref[...] * (r * _sigmoid(r))

    l0, l1, l2 = (token_rows(l_ref, scr) for l_ref, scr in zip((l0_ref, l1_ref, l2_ref), scratch[0:3]))
    a0, a1, a2 = (token_rows(a_ref, scr) for a_ref, scr in zip((a0_ref, a1_ref, a2_ref), scratch[3:6]))
    m = jnp.maximum(jnp.maximum(l0, l1), l2)
    e0, e1, e2 = jnp.exp(l0 - m), jnp.exp(l1 - m), jnp.exp(l2 - m)
    y_att = (e0 * a0 + e1 * a1 + e2 * a2) / (e0 + e1 + e2)

    t_gla = jnp.dot(y_gla.astype(BF16), wa_ref[...], preferred_element_type=F32)
    t_att = jnp.dot(y_att.astype(BF16), wb_ref[...], preferred_element_type=F32)
    merged = _sigmoid(ga_ref[...].astype(F32)) * t_gla + _sigmoid(gb_ref[...].astype(F32)) * t_att
    x1 = x_ref[...] + jnp.dot(merged.astype(BF16), wo_ref[...], preferred_element_type=F32)
    x1_ref[...] = x1
    h2 = x1 * lax.rsqrt(jnp.mean(x1 * x1, axis=-1, keepdims=True) + EPS) * nf_ref[...]
    _to_slabs(h2_ref, h2)

    logits = jnp.dot(h2, rw_ref[...], preferred_element_type=F32, precision=lax.Precision.HIGHEST) + rb_ref[...]
    lane = lax.broadcasted_iota(I32, logits.shape, 1).astype(F32)
    vals, idxs = [], []
    for _ in range(TOP_K):
        mx = jnp.max(logits, axis=-1, keepdims=True)
        ix = jnp.min(jnp.where(logits == mx, lane, float(LANES)), axis=-1, keepdims=True)
        vals.append(mx)
        idxs.append(ix)
        logits = jnp.where(lane == ix, -jnp.inf, logits)
    es = [jnp.exp(vk - vals[0]) for vk in vals]
    den = es[0] + es[1] + es[2] + es[3]
    ti = jnp.zeros(lane.shape, F32)
    tw = jnp.zeros(lane.shape, F32)
    for kk in range(TOP_K):
        ti = jnp.where(lane == float(kk), idxs[kk], ti)
        tw = jnp.where(lane == float(kk), es[kk] / den, tw)
    ti_ref[...] = ti.astype(I32)
    tw_ref[...] = tw


def _merge(o_f, o_b, rg, gn, atts, lses, ga, gb, x2, wa, wb, wo, nf, rw, rb, tm=256):
    T, D = x2.shape
    B = atts[0].shape[0]
    nb = T // B // tm
    row = lambda w: pl.BlockSpec((tm, w), lambda i: (i, 0))
    const = lambda a: pl.BlockSpec(a.shape, lambda i: (0, 0))
    res = lambda a: pl.BlockSpec((None, a.shape[1], tm // a.shape[1], ATT_GW), lambda i: (i // nb, 0, i % nb, 0))
    return pl.pallas_call(
        _merge_kernel,
        grid=(T // tm,),
        in_specs=[row(D), row(D), row(D), const(gn)] + [res(a) for a in atts] + [res(a) for a in lses]
                 + [row(D), row(D), row(D), const(wa), const(wb), const(wo), const(nf), const(rw), const(rb)],
        out_specs=[row(D), pl.BlockSpec((tm * SLAB, LANES), lambda i: (i, 0)), row(LANES), row(LANES)],
        out_shape=[jax.ShapeDtypeStruct((T, D), F32), jax.ShapeDtypeStruct((T * SLAB, LANES), F32),
                   jax.ShapeDtypeStruct((T, LANES), I32), jax.ShapeDtypeStruct((T, LANES), F32)],
        scratch_shapes=[pltpu.VMEM((ATT_GW // LANES, tm, LANES), F32)] * 6,
        compiler_params=_cparams(("arbitrary",), 48),
        name="merge_router",
    )(o_f, o_b, rg, gn, *atts, *lses, ga, gb, x2, wa, wb, wo, nf, rw, rb)


SLAB = 8


def _to_slabs(ref, val):
    rows = val.shape[0]
    for s in range(SLAB):
        ref[pl.ds(s, rows, stride=SLAB), :] = val[:, s * LANES:(s + 1) * LANES]


def _from_slabs(ref, rows):
    return jnp.concatenate([ref[pl.ds(s, rows, stride=SLAB), :] for s in range(SLAB)], axis=-1)


def _experts_kernel(te_ref, dst_ref, h_hbm, w1_ref, b1_ref, w2_ref, b2_ref,
                    y_hbm, xbuf, obuf, xt_scr, w1p, w2b, sem_g, sem_s, *, tm, n_tok):
    i = pl.program_id(0)
    last = pl.num_programs(0) - 1
    slot = i % 2
    D = w2b.shape[0]

    def gather_start(row, s):
        for r in range(tm):
            tok = dst_ref[row, r] & (n_tok - 1)
            pltpu.make_async_copy(h_hbm.at[pl.ds(pl.multiple_of(tok * SLAB, SLAB), SLAB), :],
                                  xbuf.at[s, pl.ds(r * SLAB, SLAB), :], sem_g.at[s]).start()

    def gather_wait(s):
        pltpu.make_async_copy(h_hbm.at[pl.ds(0, tm * SLAB), :], xbuf.at[s], sem_g.at[s]).wait()

    def scatter_start(row, s):
        for r in range(tm):
            d = dst_ref[row, r]
            pltpu.make_async_copy(obuf.at[s, pl.ds(r * SLAB, SLAB), :],
                                  y_hbm.at[pl.ds(pl.multiple_of(d * SLAB, SLAB), SLAB), :], sem_s.at[s]).start()

    def scatter_wait(s):
        pltpu.make_async_copy(obuf.at[s], y_hbm.at[pl.ds(0, tm * SLAB), :], sem_s.at[s]).wait()

    @pl.when(i == 0)
    def _():
        obuf[1] = jnp.zeros(obuf.shape[1:], F32)
        gather_start(1, 0)

    @pl.when((i == 0) | (te_ref[i] != te_ref[jnp.maximum(i - 1, 0)]))
    def _():
        kk = lax.broadcasted_iota(I32, (2 * LANES, 2 * LANES), 0)
        nn = lax.broadcasted_iota(I32, (2 * LANES, 2 * LANES), 1)
        perm = (kk == jnp.where(nn < LANES, 2 * nn, 2 * (nn - LANES) + 1)).astype(BF16)
        for c in range(w1p.shape[1] // (2 * LANES)):
            cs = slice(c * 2 * LANES, (c + 1) * 2 * LANES)
            w1p[:, cs] = jnp.dot(w1_ref[:, cs].astype(BF16), perm, preferred_element_type=F32).astype(BF16)
        w2b[...] = w2_ref[...].astype(BF16)

    gather_wait(slot)
    xt_scr[...] = _from_slabs(xbuf.at[slot], tm).astype(BF16)
    gather_start(jnp.minimum(i + 2, last), 1 - slot)
    scatter_start(i, 1 - slot)
    hid = jnp.dot(xt_scr[...], w1p[...], preferred_element_type=F32) + b1_ref[...]
    acts = []
    for c in range(D // LANES):
        gate = jnp.minimum(hid[:, 2 * c * LANES:(2 * c + 1) * LANES], SWIGLU_LIMIT)
        up = jnp.clip(hid[:, (2 * c + 1) * LANES:(2 * c + 2) * LANES], -SWIGLU_LIMIT, SWIGLU_LIMIT)
        acts.append(((up + 1.0) * (gate * _sigmoid(gate * SWIGLU_ALPHA))).astype(BF16))
    act = jnp.concatenate(acts, axis=-1)
    out = jnp.dot(act, w2b[...], preferred_element_type=F32) + b2_ref[...]

    @pl.when(i > 0)
    def _():
        scatter_wait(slot)

    _to_slabs(obuf.at[slot], out)

    @pl.when(i == last)
    def _():
        scatter_wait(1 - slot)
        gather_wait(1 - slot)


def _experts(te_ext, dst_ext, h2s, w1, b1p, w2, b2, tm):
    T = h2s.shape[0] // SLAB
    n_steps = dst_ext.shape[0]
    _, D, F = w1.shape
    by_expert = lambda r, c: pl.BlockSpec((None, r, c), lambda i, te, dst: (te[i], 0, 0))
    return pl.pallas_call(
        functools.partial(_experts_kernel, tm=tm, n_tok=T),
        grid_spec=pltpu.PrefetchScalarGridSpec(
            num_scalar_prefetch=2,
            grid=(n_steps,),
            in_specs=[pl.BlockSpec(memory_space=pl.ANY), by_expert(D, F), by_expert(1, F),
                      by_expert(F // 2, D), by_expert(1, D)],
            out_specs=pl.BlockSpec(memory_space=pl.ANY),
            scratch_shapes=[pltpu.VMEM((2, tm * SLAB, LANES), F32), pltpu.VMEM((2, tm * SLAB, LANES), F32),
                            pltpu.VMEM((tm, D), BF16), pltpu.VMEM((D, F), BF16), pltpu.VMEM((F // 2, D), BF16),
                            pltpu.SemaphoreType.DMA((2,)), pltpu.SemaphoreType.DMA((2,))],
        ),
        out_shape=jax.ShapeDtypeStruct((n_steps * tm * SLAB, LANES), F32),
        compiler_params=pltpu.CompilerParams(dimension_semantics=("arbitrary",), vmem_limit_bytes=56 * MIB,
                                             disable_bounds_checks=True),
        name="experts",
    )(te_ext, dst_ext, h2s, w1, b1p, w2, b2)


def _route(topi, tm):
    T = topi.shape[0]
    E = N_EXPERTS
    n_asg = T * TOP_K
    n_tiles = n_asg // tm + E
    e_flat = topi.reshape(n_asg)
    order = jnp.argsort(e_flat, stable=True).astype(I32)
    counts = jnp.sum(e_flat[:, None] == jnp.arange(E, dtype=I32)[None, :], axis=0, dtype=I32)
    padded = (counts + tm - 1) // tm * tm
    pend = jnp.cumsum(padded)
    pstart = pend - padded
    cend = jnp.cumsum(counts)
    cstart = cend - counts
    n_used = pend[-1] // tm
    tile = jnp.arange(n_tiles, dtype=I32)
    te_raw = jnp.sum(tile[:, None] * tm >= pend[None, :], axis=1, dtype=I32)
    last_e = jnp.sum((n_used - 1) * tm >= pend, dtype=I32)
    te = jnp.where(te_raw < E, te_raw, last_e)
    slot = jnp.arange(n_tiles * tm, dtype=I32)
    es = jnp.repeat(te_raw, tm)
    esc = jnp.minimum(es, E - 1)
    j = slot - pstart[esc]
    valid = (es < E) & (j < counts[esc])
    a = order[jnp.clip(cstart[esc] + j, 0, n_asg - 1)]
    dst_valid = (a % TOP_K) * T + a // TOP_K
    cend_ext = jnp.concatenate([cend, jnp.full((1,), n_asg, I32)])
    dst_pad = n_asg + slot - cend_ext[es]
    dst = jnp.where(valid, dst_valid, dst_pad).astype(I32)
    dump = n_tiles * tm + jnp.arange(tm, dtype=I32)
    dst_ext = jnp.concatenate([dump[None, :], dst.reshape(n_tiles, tm)], axis=0)
    te_ext = jnp.concatenate([te, te[-1:]])
    return te_ext, dst_ext


def _final_kernel(x1_ref, y0_ref, y1_ref, y2_ref, y3_ref, tw_ref, p_ref, np_ref, wg_ref, wp_ref, nfin_ref, o_ref):
    tw = tw_ref[...]
    x2 = x1_ref[...]
    for kk, y_ref in enumerate((y0_ref, y1_ref, y2_ref, y3_ref)):
        x2 = x2 + tw[:, kk:kk + 1] * _from_slabs(y_ref, x2.shape[0])
    h3 = (x2 * lax.rsqrt(jnp.mean(x2 * x2, axis=-1, keepdims=True) + EPS) * np_ref[...]).astype(BF16)
    gate = _sigmoid(jnp.dot(h3, wg_ref[...], preferred_element_type=F32))
    pe = jnp.dot(p_ref[...].astype(BF16), wp_ref[...], preferred_element_type=F32)
    x3 = x2 + gate * pe
    o_ref[...] = x3 * lax.rsqrt(jnp.mean(x3 * x3, axis=-1, keepdims=True) + EPS) * nfin_ref[...]


def _final(x1, ybuf, tw, p2, n_ple, wg, wp, n_fin, tm=256):
    T, D = x1.shape
    nb = T // tm
    row = lambda w: pl.BlockSpec((tm, w), lambda i: (i, 0))
    const = lambda a: pl.BlockSpec(a.shape, lambda i: (0, 0))
    yspec = lambda kk: pl.BlockSpec((tm * SLAB, LANES), lambda i: (kk * nb + i, 0))
    return pl.pallas_call(
        _final_kernel,
        grid=(nb,),
        in_specs=[row(D)] + [yspec(kk) for kk in range(TOP_K)] + [row(LANES), row(p2.shape[1]),
                  const(n_ple), const(wg), const(wp), const(n_fin)],
        out_specs=row(D),
        out_shape=jax.ShapeDtypeStruct((T, D), F32),
        compiler_params=_cparams(("arbitrary",), 48),
        name="final",
    )(x1, ybuf, ybuf, ybuf, ybuf, tw, p2, n_ple, wg, wp, n_fin)


def kernel(x, p, positions, norm_mix, w_in, gla_fgate_up, gla_fgate_bias, gla_out_norm, w_branch_gla, w_branch_attn, w_out, norm_ffn, router_w, router_b, expert_w1, expert_b1, expert_w2, expert_b2, norm_ple, ple_gate_w, ple_proj, norm_final):
    B, S, D = x.shape
    T = B * S
    assert w_in.shape[0] == 1, "single-layer block: the final norm is fused into the layer's last kernel"
    x2 = x.reshape(T, D)
    pos2 = positions.reshape(T, 1)

    w = w_in[0]
    w_cat = jnp.concatenate([
        w[:, :3072],
        jnp.pad(w[:, 3072:3104], ((0, 0), (0, LANES - 2 * GLA_RANK))),
        w[:, 3104:3104 + 768] * (ATT_HD ** -0.5),
        w[:, 3104 + 768:],
    ], axis=1).astype(BF16)
    inv_freq = ROPE_THETA ** (-jnp.arange(0, ROT_DIM, 2, dtype=F32) / ROT_DIM)
    lane_f = (jnp.arange(LANES) % ATT_HD) % (ROT_DIM // 2)
    tab = jnp.zeros((8, LANES), F32).at[0].set(inv_freq[lane_f])
    u_pad = jnp.zeros((2, LANES, GLA_HEADS * GLA_DK), F32)
    u_pad = u_pad.at[0, :GLA_RANK].set(gla_fgate_up[0, 0]).at[1, GLA_RANK:2 * GLA_RANK].set(gla_fgate_up[0, 1])
    u_pad = u_pad.astype(BF16)
    bias = gla_fgate_bias[0].reshape(2, 1, GLA_HEADS * GLA_DK)

    qg, kg, vg, rg, al, ga, gb, *att_in = _inproj(x2, pos2, norm_mix[0].reshape(1, D), tab, w_cat, B)
    r3 = lambda t: t.reshape(B, S, t.shape[-1])
    o_f, o_b = _gla(r3(qg), r3(kg), r3(vg), r3(al), u_pad, bias)
    atts, lses = [], []
    for gi, (window, _) in enumerate(ATT_GROUPS):
        o_g, lse_g = _attn_group(*att_in[3 * gi:3 * gi + 3], window)
        atts.append(o_g)
        lses.append(lse_g)

    rw = jnp.pad(router_w[0], ((0, 0), (0, LANES - N_EXPERTS)))
    rb = jnp.concatenate([router_b[0], jnp.full((LANES - N_EXPERTS,), -jnp.inf, F32)]).reshape(1, LANES)
    x1, h2, topi, topw = _merge(
        o_f.reshape(T, D), o_b.reshape(T, D), rg, gla_out_norm[0].reshape(1, D), atts, lses, ga, gb, x2,
        w_branch_gla[0].astype(BF16), w_branch_attn[0].astype(BF16), w_out[0].astype(BF16),
        norm_ffn[0].reshape(1, D), rw, rb)

    tm_e = 256
    assert T & (T - 1) == 0, "token count must be a power of two (row index is masked out of the slot code)"
    te, dst2d = _route(topi[:, :TOP_K], tm_e)
    E, _, F = expert_w1[0].shape
    b1p = expert_b1[0].reshape(E, F // (2 * LANES), LANES, 2).transpose(0, 1, 3, 2).reshape(E, 1, F)
    ybuf = _experts(te, dst2d, h2, expert_w1[0], b1p, expert_w2[0], expert_b2[0][:, None, :], tm_e)

    out = _final(x1, ybuf, topw, p[0].reshape(T, -1), norm_ple[0].reshape(1, D), ple_gate_w[0].astype(BF16),
                 ple_proj[0].astype(BF16), norm_final.reshape(1, D))
    return out.reshape(B, S, D)
```

```python
import functools

import jax
import jax.numpy as jnp
from jax import lax
from jax.experimental import pallas as pl
from jax.experimental.pallas import tpu as pltpu

F32 = jnp.float32
BF16 = jnp.bfloat16
I32 = jnp.int32

EPS = 1e-6
GLA_HEADS = 4
GLA_DK = 128
GLA_DV = 256
GLA_RANK = 16
GLA_TAU = 16.0
GLA_CHUNK = 64
ATT_GROUPS = ((128, 1), (512, 4), (2048, 16))
ATT_HPG = 4
ATT_HD = 64
ATT_GW = ATT_HPG * ATT_HD
ROT_DIM = 16
ROPE_THETA = 500000.0
N_EXPERTS = 32
TOP_K = 4
SWIGLU_ALPHA = 1.702
SWIGLU_LIMIT = 7.0

LANES = 128
MIB = 1024 * 1024

SEG_WIDTHS = (512, 512, 1024, 1024, LANES, 768, 768, 768, 1024, 1024)
SEG_OFFS = tuple(sum(SEG_WIDTHS[:i]) for i in range(len(SEG_WIDTHS) + 1))


def _cparams(sem, vmem_mib):
    return pltpu.CompilerParams(dimension_semantics=sem, vmem_limit_bytes=vmem_mib * MIB)


def _inproj_kernel(x_ref, pos_ref, g_ref, tab_ref, w_ref, *refs):
    (qg_ref, kg_ref, vg_ref, rg_ref, al_ref, ga_ref, gb_ref), att_refs, ysc = refs[:7], refs[7:16], refs[16]
    x = x_ref[...]
    h = (x * lax.rsqrt(jnp.mean(x * x, axis=-1, keepdims=True) + EPS) * g_ref[...]).astype(BF16)

    def proj(seg):
        return jnp.dot(h, w_ref[:, SEG_OFFS[seg]:SEG_OFFS[seg + 1]], preferred_element_type=F32)

    qg_ref[...] = proj(0).astype(BF16)
    kg_ref[...] = proj(1).astype(BF16)
    vg_ref[...] = proj(2).astype(BF16)
    rg_ref[...] = proj(3).astype(BF16)
    al_ref[...] = proj(4).astype(BF16)
    ga_ref[...] = proj(8).astype(BF16)
    gb_ref[...] = proj(9).astype(BF16)

    ang = pos_ref[...].astype(F32) * tab_ref[0:1, :]
    cs = jnp.cos(ang)
    sn = jnp.sin(ang)
    lane = lax.broadcasted_iota(I32, (1, LANES), 1) % ATT_HD
    c_mul = jnp.where(lane < ROT_DIM, cs, 1.0)
    s_next = jnp.where(lane < ROT_DIM // 2, -sn, 0.0)
    s_prev = jnp.where((lane >= ROT_DIM // 2) & (lane < ROT_DIM), sn, 0.0)

    tm = ysc.shape[1]
    tiles_per_group = ATT_GW // LANES

    def emit(seg, which, rotary):
        y = proj(seg)
        for j in range(SEG_WIDTHS[seg] // LANES):
            t = y[:, j * LANES:(j + 1) * LANES]
            if rotary:
                t = (t * c_mul + pltpu.roll(t, LANES - ROT_DIM // 2, axis=1) * s_next
                     + pltpu.roll(t, ROT_DIM // 2, axis=1) * s_prev)
            ysc[j] = t
        for gi, (_, d) in enumerate(ATT_GROUPS):
            o_ref = att_refs[3 * gi + which]
            for r in range(d):
                for j in range(tiles_per_group):
                    o_ref[r, :, j * LANES:(j + 1) * LANES] = ysc[
                        gi * tiles_per_group + j, pl.ds(r, tm // d, stride=d), :].astype(BF16)

    emit(5, 0, True)
    emit(6, 1, True)
    emit(7, 2, False)


def _inproj(x2, pos2, gain, tab, w_cat, B, tm=512):
    T, D = x2.shape
    S = T // B
    nb = S // tm
    row_widths = [SEG_WIDTHS[s] for s in (0, 1, 2, 3, 4, 8, 9)]
    outs = [jax.ShapeDtypeStruct((T, w), BF16) for w in row_widths]
    out_specs = [pl.BlockSpec((tm, w), lambda i: (i, 0)) for w in row_widths]
    for _, d in ATT_GROUPS:
        for _ in range(3):
            outs.append(jax.ShapeDtypeStruct((B, d, S // d, ATT_GW), BF16))
            out_specs.append(pl.BlockSpec((None, d, tm // d, ATT_GW), lambda i: (i // nb, 0, i % nb, 0)))
    return pl.pallas_call(
        _inproj_kernel,
        grid=(T // tm,),
        in_specs=[
            pl.BlockSpec((tm, D), lambda i: (i, 0)),
            pl.BlockSpec((tm, 1), lambda i: (i, 0)),
            pl.BlockSpec((1, D), lambda i: (0, 0)),
            pl.BlockSpec((8, LANES), lambda i: (0, 0)),
            pl.BlockSpec((D, SEG_OFFS[-1]), lambda i: (0, 0), pipeline_mode=pl.Buffered(1)),
        ],
        out_specs=out_specs,
        out_shape=outs,
        scratch_shapes=[pltpu.VMEM((SEG_WIDTHS[5] // LANES, tm, LANES), F32)],
        compiler_params=_cparams(("arbitrary",), 56),
        name="inproj",
    )(x2, pos2, gain, tab, w_cat)


def _split3(a):
    a1 = a.astype(BF16)
    r1 = a - a1.astype(F32)
    a2 = r1.astype(BF16)
    a3 = (r1 - a2.astype(F32)).astype(BF16)
    return a1, a2, a3


def _gla_chunk(q_ref, k_ref, v_ref, o_ref, g_ref, st_ref, r0, backward):
    C = GLA_CHUNK
    row = lax.broadcasted_iota(I32, (C, C), 0)
    col = lax.broadcasted_iota(I32, (C, C), 1)
    if backward:
        tri = (col >= row).astype(BF16)
        keep = col > row
        ref_row, last_row = C // 2, 0
    else:
        tri = (col <= row).astype(BF16)
        keep = col <= row
        ref_row, last_row = C // 2 - 1, C - 1
    rows = pl.ds(r0, C)
    g1, g2, g3 = _split3(g_ref[rows, :])
    b = (jnp.dot(tri, g1, preferred_element_type=F32) + jnp.dot(tri, g2, preferred_element_type=F32)
         + jnp.dot(tri, g3, preferred_element_type=F32))
    b_ref = b[ref_row:ref_row + 1]
    b_last = b[last_row:last_row + 1]
    q = q_ref[rows, :].astype(F32) * (GLA_DK ** -0.5)
    k = k_ref[rows, :].astype(F32)
    qa = (q * jnp.exp(b - b_ref)).astype(BF16)
    ka = (k * jnp.exp(b_ref - b)).astype(BF16)
    qi = (q * jnp.exp(b)).astype(BF16)
    ks = k * jnp.exp(b_last - b)
    dec = jnp.broadcast_to(jnp.exp(b_last), (8, b.shape[1])).T
    for h in range(GLA_HEADS):
        hk = slice(h * GLA_DK, (h + 1) * GLA_DK)
        hv = slice(h * GLA_DV, (h + 1) * GLA_DV)
        v = v_ref[rows, hv]
        state = st_ref[h]
        s = lax.dot_general(qa[:, hk], ka[:, hk], (((1,), (1,)), ((), ())), preferred_element_type=F32)
        s = jnp.where(keep, s, 0.0).astype(BF16)
        o = (jnp.dot(s, v, preferred_element_type=F32)
             + jnp.dot(qi[:, hk], state.astype(BF16), preferred_element_type=F32))
        o_ref[rows, hv] = o.astype(BF16)
        kv = jnp.dot(ks[:, hk].T.astype(BF16), v, preferred_element_type=F32)
        st_ref[h] = dec[hk, 0:1] * state + kv


def _gla_kernel(qf_ref, kf_ref, vf_ref, af_ref, qb_ref, kb_ref, vb_ref, ab_ref, u_ref, bias_ref,
                of_ref, ob_ref, st_ref, g_scr, *, n_chunks):
    @pl.when(pl.program_id(1) == 0)
    def _():
        st_ref[...] = jnp.zeros_like(st_ref)

    for d, a_ref in enumerate((af_ref, ab_ref)):
        z = jnp.dot(a_ref[...], u_ref[d], preferred_element_type=F32) + bias_ref[d]
        g_scr[d] = (jnp.minimum(z, 0.0) - jnp.log(1.0 + jnp.exp(-jnp.abs(z)))) * (1.0 / GLA_TAU)

    def body(c, carry):
        rf = pl.multiple_of(c * GLA_CHUNK, GLA_CHUNK)
        rb = pl.multiple_of((n_chunks - 1 - c) * GLA_CHUNK, GLA_CHUNK)
        _gla_chunk(qf_ref, kf_ref, vf_ref, of_ref, g_scr.at[0], st_ref.at[0], rf, False)
        _gla_chunk(qb_ref, kb_ref, vb_ref, ob_ref, g_scr.at[1], st_ref.at[1], rb, True)
        return carry

    lax.fori_loop(0, n_chunks, body, 0)


def _gla(qg, kg, vg, al, u_pad, bias, n_chunks=16):
    B, S, _ = qg.shape
    rows = n_chunks * GLA_CHUNK
    NB = S // rows
    fwd = lambda b, n: (b, n, 0)
    bwd = lambda b, n: (b, NB - 1 - n, 0)
    specs = lambda im: [pl.BlockSpec((None, rows, t.shape[-1]), im) for t in (qg, kg, vg, al)]
    out = jax.ShapeDtypeStruct((B, S, GLA_HEADS * GLA_DV), BF16)
    return pl.pallas_call(
        functools.partial(_gla_kernel, n_chunks=n_chunks),
        grid=(B, NB),
        in_specs=specs(fwd) + specs(bwd) + [
            pl.BlockSpec(u_pad.shape, lambda b, n: (0, 0, 0)),
            pl.BlockSpec(bias.shape, lambda b, n: (0, 0, 0)),
        ],
        out_specs=[pl.BlockSpec((None, rows, GLA_HEADS * GLA_DV), fwd),
                   pl.BlockSpec((None, rows, GLA_HEADS * GLA_DV), bwd)],
        out_shape=[out, out],
        scratch_shapes=[pltpu.VMEM((2, GLA_HEADS, GLA_DK, GLA_DV), F32),
                        pltpu.VMEM((2, rows, GLA_HEADS * GLA_DK), F32)],
        compiler_params=_cparams(("arbitrary", "arbitrary"), 48),
        name="gla",
    )(qg, kg, vg, al, qg, kg, vg, al, u_pad, bias)


def _attn_kernel(q_ref, k_ref, v_ref, o_ref, lse_ref, *, lq, win, half):
    L = k_ref.shape[0]
    q0 = pl.program_id(2) * lq
    k0 = pl.multiple_of(jnp.clip(q0 - half, 0, L - win), half)
    qpos = q0 + lax.broadcasted_iota(I32, (lq, 1), 0)
    kpos = k0 + lax.broadcasted_iota(I32, (1, win), 1)
    valid = jnp.abs(qpos - kpos) <= half
    q = q_ref[...]
    k = k_ref[pl.ds(k0, win), :]
    v = v_ref[pl.ds(k0, win), :]
    for h in range(ATT_HPG):
        hs = slice(h * ATT_HD, (h + 1) * ATT_HD)
        s = lax.dot_general(q[:, hs], k[:, hs], (((1,), (1,)), ((), ())), preferred_element_type=F32)
        s = jnp.where(valid, s, -jnp.inf)
        m = jnp.max(s, axis=-1, keepdims=True)
        pr = jnp.exp(s - m)
        l = jnp.sum(pr, axis=-1, keepdims=True)
        o = jnp.dot(pr.astype(BF16), v[:, hs], preferred_element_type=F32) / l
        o_ref[:, hs] = o.astype(BF16)
        lse_ref[:, hs] = jnp.broadcast_to(m + jnp.log(l), (lq, ATT_HD))


def _attn_group(q, k, v, window, lq=256):
    B, d, L, _ = q.shape
    half = window // (2 * d)
    lq = min(lq, L)
    win = min(lq + 2 * half, L)
    qmap = lambda b, r, i: (b, r, i, 0)
    kmap = lambda b, r, i: (b, r, 0, 0)
    return pl.pallas_call(
        functools.partial(_attn_kernel, lq=lq, win=win, half=half),
        grid=(B, d, L // lq),
        in_specs=[pl.BlockSpec((None, None, lq, ATT_GW), qmap),
                  pl.BlockSpec((None, None, L, ATT_GW), kmap),
                  pl.BlockSpec((None, None, L, ATT_GW), kmap)],
        out_specs=[pl.BlockSpec((None, None, lq, ATT_GW), qmap), pl.BlockSpec((None, None, lq, ATT_GW), qmap)],
        out_shape=[jax.ShapeDtypeStruct(q.shape, BF16), jax.ShapeDtypeStruct(q.shape, F32)],
        compiler_params=_cparams(("arbitrary", "arbitrary", "arbitrary"), 40),
        name=f"attn_d{d}",
    )(q, k, v)


def _sigmoid(t):
    return 1.0 / (1.0 + jnp.exp(-t))


def _merge_kernel(of_ref, ob_ref, rg_ref, gn_ref, a0_ref, a1_ref, a2_ref, l0_ref, l1_ref, l2_ref,
                  ga_ref, gb_ref, x_ref, wa_ref, wb_ref, wo_ref, nf_ref, rw_ref, rb_ref,
                  x1_ref, h2_ref, ti_ref, tw_ref, *scratch):
    def token_rows(blk_ref, scr):
        d, n, w = blk_ref.shape
        if d == 1:
            return blk_ref[0].astype(F32)
        for r in range(d):
            for j in range(w // LANES):
                scr[j, pl.ds(r, n, stride=d), :] = blk_ref[r, :, j * LANES:(j + 1) * LANES].astype(F32)
        return jnp.concatenate([scr[j] for j in range(w // LANES)], axis=-1)

    o = of_ref[...].astype(F32) + ob_ref[...].astype(F32)
    parts = []
    for h in range(GLA_HEADS):
        oh = o[:, h * GLA_DV:(h + 1) * GLA_DV]
        parts.append(oh * lax.rsqrt(jnp.mean(oh * oh, axis=-1, keepdims=True) + EPS))
    r = rg_ref[...].astype(F32)
    y_gla = jnp.concatenate(parts, axis=-1) * gn_ref[...] * (r * _sigmoid(r))

    l0, l1, l2 = (token_rows(l_ref, scr) for l_ref, scr in zip((l0_ref, l1_ref, l2_ref), scratch[0:3]))
    a0, a1, a2 = (token_rows(a_ref, scr) for a_ref, scr in zip((a0_ref, a1_ref, a2_ref), scratch[3:6]))
    m = jnp.maximum(jnp.maximum(l0, l1), l2)
    e0, e1, e2 = jnp.exp(l0 - m), jnp.exp(l1 - m), jnp.exp(l2 - m)
    y_att = (e0 * a0 + e1 * a1 + e2 * a2) / (e0 + e1 + e2)

    t_gla = jnp.dot(y_gla.astype(BF16), wa_ref[...], preferred_element_type=F32)
    t_att = jnp.dot(y_att.astype(BF16), wb_ref[...], preferred_element_type=F32)
    merged = _sigmoid(ga_ref[...].astype(F32)) * t_gla + _sigmoid(gb_ref[...].astype(F32)) * t_att
    x1 = x_ref[...] + jnp.dot(merged.astype(BF16), wo_ref[...], preferred_element_type=F32)
    x1_ref[...] = x1
    h2 = x1 * lax.rsqrt(jnp.mean(x1 * x1, axis=-1, keepdims=True) + EPS) * nf_ref[...]
    _to_slabs(h2_ref, h2)

    logits = jnp.dot(h2, rw_ref[...], preferred_element_type=F32, precision=lax.Precision.HIGHEST) + rb_ref[...]
    lane = lax.broadcasted_iota(I32, logits.shape, 1).astype(F32)
    vals, idxs = [], []
    for _ in range(TOP_K):
        mx = jnp.max(logits, axis=-1, keepdims=True)
        ix = jnp.min(jnp.where(logits == mx, lane, float(LANES)), axis=-1, keepdims=True)
        vals.append(mx)
        idxs.append(ix)
        logits = jnp.where(lane == ix, -jnp.inf, logits)
    es = [jnp.exp(vk - vals[0]) for vk in vals]
    den = es[0] + es[1] + es[2] + es[3]
    ti = jnp.zeros(lane.shape, F32)
    tw = jnp.zeros(lane.shape, F32)
    for kk in range(TOP_K):
        ti = jnp.where(lane == float(kk), idxs[kk], ti)
        tw = jnp.where(lane == float(kk), es[kk] / den, tw)
    ti_ref[...] = ti.astype(I32)
    tw_ref[...] = tw


def _merge(o_f, o_b, rg, gn, atts, lses, ga, gb, x2, wa, wb, wo, nf, rw, rb, tm=256):
    T, D = x2.shape
    B = atts[0].shape[0]
    nb = T // B // tm
    row = lambda w: pl.BlockSpec((tm, w), lambda i: (i, 0))
    const = lambda a: pl.BlockSpec(a.shape, lambda i: (0, 0))
    res = lambda a: pl.BlockSpec((None, a.shape[1], tm // a.shape[1], ATT_GW), lambda i: (i // nb, 0, i % nb, 0))
    return pl.pallas_call(
        _merge_kernel,
        grid=(T // tm,),
        in_specs=[row(D), row(D), row(D), const(gn)] + [res(a) for a in atts] + [res(a) for a in lses]
                 + [row(D), row(D), row(D), const(wa), const(wb), const(wo), const(nf), const(rw), const(rb)],
        out_specs=[row(D), pl.BlockSpec((tm * SLAB, LANES), lambda i: (i, 0)), row(LANES), row(LANES)],
        out_shape=[jax.ShapeDtypeStruct((T, D), F32), jax.ShapeDtypeStruct((T * SLAB, LANES), F32),
                   jax.ShapeDtypeStruct((T, LANES), I32), jax.ShapeDtypeStruct((T, LANES), F32)],
        scratch_shapes=[pltpu.VMEM((ATT_GW // LANES, tm, LANES), F32)] * 6,
        compiler_params=_cparams(("arbitrary",), 48),
        name="merge_router",
    )(o_f, o_b, rg, gn, *atts, *lses, ga, gb, x2, wa, wb, wo, nf, rw, rb)


SLAB = 8


def _to_slabs(ref, val):
    rows = val.shape[0]
    for s in range(SLAB):
        ref[pl.ds(s, rows, stride=SLAB), :] = val[:, s * LANES:(s + 1) * LANES]


def _from_slabs(ref, rows):
    return jnp.concatenate([ref[pl.ds(s, rows, stride=SLAB), :] for s in range(SLAB)], axis=-1)


K_PHASES = 4


def _experts_kernel(te_ref, dst_ref, h_hbm, w1_ref, b1_ref, w2_ref, b2_ref,
                    y_hbm, buf, w1p, w2b, sem_g, sem_s, *, tm, n_tok):
    i = pl.program_id(0)
    last = pl.num_programs(0) - 1
    slot = i % 2
    D = w2b.shape[0]

    def gather_start(row, s, r0, r1):
        for r in range(r0, r1):
            tok = dst_ref[row, r] & (n_tok - 1)
            pltpu.make_async_copy(h_hbm.at[pl.ds(pl.multiple_of(tok * SLAB, SLAB), SLAB), :],
                                  buf.at[s, pl.ds(r * SLAB, SLAB), :], sem_g.at[s]).start()

    def gather_wait(s):
        pltpu.make_async_copy(h_hbm.at[pl.ds(0, tm * SLAB), :], buf.at[s], sem_g.at[s]).wait()

    def scatter_start(row, s, r0, r1):
        for r in range(r0, r1):
            d = dst_ref[row, r]
            pltpu.make_async_copy(buf.at[2 + s, pl.ds(r * SLAB, SLAB), :],
                                  y_hbm.at[pl.ds(pl.multiple_of(d * SLAB, SLAB), SLAB), :], sem_s.at[s]).start()

    def scatter_wait(s):
        pltpu.make_async_copy(buf.at[2 + s], y_hbm.at[pl.ds(0, tm * SLAB), :], sem_s.at[s]).wait()

    @pl.when(i == 0)
    def _():
        buf[3] = jnp.zeros(buf.shape[1:], F32)
        gather_start(1, 0, 0, tm)

    @pl.when((i == 0) | (te_ref[i] != te_ref[jnp.maximum(i - 1, 0)]))
    def _():
        kk = lax.broadcasted_iota(I32, (2 * LANES, 2 * LANES), 0)
        nn = lax.broadcasted_iota(I32, (2 * LANES, 2 * LANES), 1)
        perm = (kk == jnp.where(nn < LANES, 2 * nn, 2 * (nn - LANES) + 1)).astype(BF16)
        for c in range(w1p.shape[1] // (2 * LANES)):
            cs = slice(c * 2 * LANES, (c + 1) * 2 * LANES)
            w1p[:, cs] = jnp.dot(w1_ref[:, cs].astype(BF16), perm, preferred_element_type=F32).astype(BF16)
        w2b[...] = w2_ref[...].astype(BF16)

    gather_wait(slot)
    nxt = jnp.minimum(i + 2, last)
    rows_per = tm // K_PHASES
    slabs_per = SLAB // K_PHASES
    hid = b1_ref[...]
    for j in range(K_PHASES):
        gather_start(nxt, 1 - slot, j * rows_per, (j + 1) * rows_per)
        scatter_start(i, 1 - slot, j * rows_per, (j + 1) * rows_per)
        xj = jnp.concatenate([buf[slot, pl.ds(s, tm, stride=SLAB), :]
                              for s in range(j * slabs_per, (j + 1) * slabs_per)], axis=-1).astype(BF16)
        kw = slabs_per * LANES
        hid = hid + jnp.dot(xj, w1p[j * kw:(j + 1) * kw, :], preferred_element_type=F32)
    acts = []
    for c in range(D // LANES):
        gate = jnp.minimum(hid[:, 2 * c * LANES:(2 * c + 1) * LANES], SWIGLU_LIMIT)
        up = jnp.clip(hid[:, (2 * c + 1) * LANES:(2 * c + 2) * LANES], -SWIGLU_LIMIT, SWIGLU_LIMIT)
        acts.append(((up + 1.0) * (gate * _sigmoid(gate * SWIGLU_ALPHA))).astype(BF16))
    act = jnp.concatenate(acts, axis=-1)
    out = jnp.dot(act, w2b[...], preferred_element_type=F32) + b2_ref[...]

    @pl.when(i > 0)
    def _():
        scatter_wait(slot)

    _to_slabs(buf.at[2 + slot], out)

    @pl.when(i == last)
    def _():
        scatter_wait(1 - slot)
        gather_wait(1 - slot)


def _experts(te_ext, dst_ext, h2s, w1, b1p, w2, b2, tm):
    T = h2s.shape[0] // SLAB
    n_steps = dst_ext.shape[0]
    _, D, F = w1.shape
    by_expert = lambda r, c: pl.BlockSpec((None, r, c), lambda i, te, dst: (te[i], 0, 0))
    return pl.pallas_call(
        functools.partial(_experts_kernel, tm=tm, n_tok=T),
        grid_spec=pltpu.PrefetchScalarGridSpec(
            num_scalar_prefetch=2,
            grid=(n_steps,),
            in_specs=[pl.BlockSpec(memory_space=pl.ANY), by_expert(D, F), by_expert(1, F),
                      by_expert(F // 2, D), by_expert(1, D)],
            out_specs=pl.BlockSpec(memory_space=pl.ANY),
            scratch_shapes=[pltpu.VMEM((4, tm * SLAB, LANES), F32),
                            pltpu.VMEM((D, F), BF16), pltpu.VMEM((F // 2, D), BF16),
                            pltpu.SemaphoreType.DMA((2,)), pltpu.SemaphoreType.DMA((2,))],
        ),
        out_shape=jax.ShapeDtypeStruct((n_steps * tm * SLAB, LANES), F32),
        compiler_params=pltpu.CompilerParams(dimension_semantics=("arbitrary",), vmem_limit_bytes=56 * MIB,
                                             disable_bounds_checks=True),
        name="experts",
    )(te_ext, dst_ext, h2s, w1, b1p, w2, b2)


def _route(topi, tm):
    T = topi.shape[0]
    E = N_EXPERTS
    n_asg = T * TOP_K
    n_tiles = n_asg // tm + E
    e_flat = topi.reshape(n_asg)
    order = jnp.argsort(e_flat, stable=True).astype(I32)
    counts = jnp.sum(e_flat[:, None] == jnp.arange(E, dtype=I32)[None, :], axis=0, dtype=I32)
    padded = (counts + tm - 1) // tm * tm
    pend = jnp.cumsum(padded)
    pstart = pend - padded
    cend = jnp.cumsum(counts)
    cstart = cend - counts
    n_used = pend[-1] // tm
    tile = jnp.arange(n_tiles, dtype=I32)
    te_raw = jnp.sum(tile[:, None] * tm >= pend[None, :], axis=1, dtype=I32)
    last_e = jnp.sum((n_used - 1) * tm >= pend, dtype=I32)
    te = jnp.where(te_raw < E, te_raw, last_e)
    slot = jnp.arange(n_tiles * tm, dtype=I32)
    es = jnp.repeat(te_raw, tm)
    esc = jnp.minimum(es, E - 1)
    j = slot - pstart[esc]
    valid = (es < E) & (j < counts[esc])
    a = order[jnp.clip(cstart[esc] + j, 0, n_asg - 1)]
    dst_valid = (a % TOP_K) * T + a // TOP_K
    cend_ext = jnp.concatenate([cend, jnp.full((1,), n_asg, I32)])
    dst_pad = n_asg + slot - cend_ext[es]
    dst = jnp.where(valid, dst_valid, dst_pad).astype(I32)
    dump = n_tiles * tm + jnp.arange(tm, dtype=I32)
    dst_ext = jnp.concatenate([dump[None, :], dst.reshape(n_tiles, tm)], axis=0)
    te_ext = jnp.concatenate([te, te[-1:]])
    return te_ext, dst_ext


def _final_kernel(x1_ref, y0_ref, y1_ref, y2_ref, y3_ref, tw_ref, p_ref, np_ref, wg_ref, wp_ref, nfin_ref, o_ref):
    tw = tw_ref[...]
    x2 = x1_ref[...]
    for kk, y_ref in enumerate((y0_ref, y1_ref, y2_ref, y3_ref)):
        x2 = x2 + tw[:, kk:kk + 1] * _from_slabs(y_ref, x2.shape[0])
    h3 = (x2 * lax.rsqrt(jnp.mean(x2 * x2, axis=-1, keepdims=True) + EPS) * np_ref[...]).astype(BF16)
    gate = _sigmoid(jnp.dot(h3, wg_ref[...], preferred_element_type=F32))
    pe = jnp.dot(p_ref[...].astype(BF16), wp_ref[...], preferred_element_type=F32)
    x3 = x2 + gate * pe
    o_ref[...] = x3 * lax.rsqrt(jnp.mean(x3 * x3, axis=-1, keepdims=True) + EPS) * nfin_ref[...]


def _final(x1, ybuf, tw, p2, n_ple, wg, wp, n_fin, tm=256):
    T, D = x1.shape
    nb = T // tm
    row = lambda w: pl.BlockSpec((tm, w), lambda i: (i, 0))
    const = lambda a: pl.BlockSpec(a.shape, lambda i: (0, 0))
    yspec = lambda kk: pl.BlockSpec((tm * SLAB, LANES), lambda i: (kk * nb + i, 0))
    return pl.pallas_call(
        _final_kernel,
        grid=(nb,),
        in_specs=[row(D)] + [yspec(kk) for kk in range(TOP_K)] + [row(LANES), row(p2.shape[1]),
                  const(n_ple), const(wg), const(wp), const(n_fin)],
        out_specs=row(D),
        out_shape=jax.ShapeDtypeStruct((T, D), F32),
        compiler_params=_cparams(("arbitrary",), 48),
        name="final",
    )(x1, ybuf, ybuf, ybuf, ybuf, tw, p2, n_ple, wg, wp, n_fin)


def kernel(x, p, positions, norm_mix, w_in, gla_fgate_up, gla_fgate_bias, gla_out_norm, w_branch_gla, w_branch_attn, w_out, norm_ffn, router_w, router_b, expert_w1, expert_b1, expert_w2, expert_b2, norm_ple, ple_gate_w, ple_proj, norm_final):
    B, S, D = x.shape
    T = B * S
    assert w_in.shape[0] == 1, "single-layer block: the final norm is fused into the layer's last kernel"
    x2 = x.reshape(T, D)
    pos2 = positions.reshape(T, 1)

    w = w_in[0]
    w_cat = jnp.concatenate([
        w[:, :3072],
        jnp.pad(w[:, 3072:3104], ((0, 0), (0, LANES - 2 * GLA_RANK))),
        w[:, 3104:3104 + 768] * (ATT_HD ** -0.5),
        w[:, 3104 + 768:],
    ], axis=1).astype(BF16)
    inv_freq = ROPE_THETA ** (-jnp.arange(0, ROT_DIM, 2, dtype=F32) / ROT_DIM)
    lane_f = (jnp.arange(LANES) % ATT_HD) % (ROT_DIM // 2)
    tab = jnp.zeros((8, LANES), F32).at[0].set(inv_freq[lane_f])
    u_pad = jnp.zeros((2, LANES, GLA_HEADS * GLA_DK), F32)
    u_pad = u_pad.at[0, :GLA_RANK].set(gla_fgate_up[0, 0]).at[1, GLA_RANK:2 * GLA_RANK].set(gla_fgate_up[0, 1])
    u_pad = u_pad.astype(BF16)
    bias = gla_fgate_bias[0].reshape(2, 1, GLA_HEADS * GLA_DK)

    qg, kg, vg, rg, al, ga, gb, *att_in = _inproj(x2, pos2, norm_mix[0].reshape(1, D), tab, w_cat, B)
    r3 = lambda t: t.reshape(B, S, t.shape[-1])
    o_f, o_b = _gla(r3(qg), r3(kg), r3(vg), r3(al), u_pad, bias)
    atts, lses = [], []
    for gi, (window, _) in enumerate(ATT_GROUPS):
        o_g, lse_g = _attn_group(*att_in[3 * gi:3 * gi + 3], window)
        atts.append(o_g)
        lses.append(lse_g)

    rw = jnp.pad(router_w[0], ((0, 0), (0, LANES - N_EXPERTS)))
    rb = jnp.concatenate([router_b[0], jnp.full((LANES - N_EXPERTS,), -jnp.inf, F32)]).reshape(1, LANES)
    x1, h2, topi, topw = _merge(
        o_f.reshape(T, D), o_b.reshape(T, D), rg, gla_out_norm[0].reshape(1, D), atts, lses, ga, gb, x2,
        w_branch_gla[0].astype(BF16), w_branch_attn[0].astype(BF16), w_out[0].astype(BF16),
        norm_ffn[0].reshape(1, D), rw, rb)

    tm_e = 256
    assert T & (T - 1) == 0, "token count must be a power of two (row index is masked out of the slot code)"
    te, dst2d = _route(topi[:, :TOP_K], tm_e)
    E, _, F = expert_w1[0].shape
    b1p = expert_b1[0].reshape(E, F // (2 * LANES), LANES, 2).transpose(0, 1, 3, 2).reshape(E, 1, F)
    ybuf = _experts(te, dst2d, h2, expert_w1[0], b1p, expert_w2[0], expert_b2[0][:, None, :], tm_e)

    out = _final(x1, ybuf, topw, p[0].reshape(T, -1), norm_ple[0].reshape(1, D), ple_gate_w[0].astype(BF16),
                 ple_proj[0].astype(BF16), norm_final.reshape(1, D))
    return out.reshape(B, S, D)
```

```python
import functools

import jax
import jax.numpy as jnp
from jax import lax
from jax.experimental import pallas as pl
from jax.experimental.pallas import tpu as pltpu

F32 = jnp.float32
BF16 = jnp.bfloat16
I32 = jnp.int32

EPS = 1e-6
GLA_HEADS = 4
GLA_DK = 128
GLA_DV = 256
GLA_RANK = 16
GLA_TAU = 16.0
GLA_CHUNK = 64
ATT_GROUPS = ((128, 1), (512, 4), (2048, 16))
ATT_HPG = 4
ATT_HD = 64
ATT_GW = ATT_HPG * ATT_HD
ROT_DIM = 16
ROPE_THETA = 500000.0
N_EXPERTS = 32
TOP_K = 4
SWIGLU_ALPHA = 1.702
SWIGLU_LIMIT = 7.0

LANES = 128
MIB = 1024 * 1024

SEG_WIDTHS = (512, 512, 1024, 1024, LANES, 768, 768, 768, 1024, 1024)
SEG_OFFS = tuple(sum(SEG_WIDTHS[:i]) for i in range(len(SEG_WIDTHS) + 1))


def _cparams(sem, vmem_mib):
    return pltpu.CompilerParams(dimension_semantics=sem, vmem_limit_bytes=vmem_mib * MIB)


def _inproj_kernel(x_ref, pos_ref, g_ref, tab_ref, w_ref, *refs):
    (qg_ref, kg_ref, vg_ref, rg_ref, al_ref, ga_ref, gb_ref), att_refs, ysc = refs[:7], refs[7:16], refs[16]
    x = x_ref[...]
    h = (x * lax.rsqrt(jnp.mean(x * x, axis=-1, keepdims=True) + EPS) * g_ref[...]).astype(BF16)

    def proj(seg):
        return jnp.dot(h, w_ref[:, SEG_OFFS[seg]:SEG_OFFS[seg + 1]], preferred_element_type=F32)

    qg_ref[...] = proj(0).astype(BF16)
    kg_ref[...] = proj(1).astype(BF16)
    vg_ref[...] = proj(2).astype(BF16)
    rg_ref[...] = proj(3).astype(BF16)
    al_ref[...] = proj(4).astype(BF16)
    ga_ref[...] = proj(8).astype(BF16)
    gb_ref[...] = proj(9).astype(BF16)

    ang = pos_ref[...].astype(F32) * tab_ref[0:1, :]
    cs = jnp.cos(ang)
    sn = jnp.sin(ang)
    lane = lax.broadcasted_iota(I32, (1, LANES), 1) % ATT_HD
    c_mul = jnp.where(lane < ROT_DIM, cs, 1.0)
    s_next = jnp.where(lane < ROT_DIM // 2, -sn, 0.0)
    s_prev = jnp.where((lane >= ROT_DIM // 2) & (lane < ROT_DIM), sn, 0.0)

    tm = ysc.shape[1]
    tiles_per_group = ATT_GW // LANES

    def emit(seg, which, rotary):
        y = proj(seg)
        for j in range(SEG_WIDTHS[seg] // LANES):
            t = y[:, j * LANES:(j + 1) * LANES]
            if rotary:
                t = (t * c_mul + pltpu.roll(t, LANES - ROT_DIM // 2, axis=1) * s_next
                     + pltpu.roll(t, ROT_DIM // 2, axis=1) * s_prev)
            ysc[j] = t
        for gi, (_, d) in enumerate(ATT_GROUPS):
            o_ref = att_refs[3 * gi + which]
            for r in range(d):
                for j in range(tiles_per_group):
                    o_ref[r, :, j * LANES:(j + 1) * LANES] = ysc[
                        gi * tiles_per_group + j, pl.ds(r, tm // d, stride=d), :].astype(BF16)

    emit(5, 0, True)
    emit(6, 1, True)
    emit(7, 2, False)


def _inproj(x2, pos2, gain, tab, w_cat, B, tm=512):
    T, D = x2.shape
    S = T // B
    nb = S // tm
    row_widths = [SEG_WIDTHS[s] for s in (0, 1, 2, 3, 4, 8, 9)]
    outs = [jax.ShapeDtypeStruct((T, w), BF16) for w in row_widths]
    out_specs = [pl.BlockSpec((tm, w), lambda i: (i, 0)) for w in row_widths]
    for _, d in ATT_GROUPS:
        for _ in range(3):
            outs.append(jax.ShapeDtypeStruct((B, d, S // d, ATT_GW), BF16))
            out_specs.append(pl.BlockSpec((None, d, tm // d, ATT_GW), lambda i: (i // nb, 0, i % nb, 0)))
    return pl.pallas_call(
        _inproj_kernel,
        grid=(T // tm,),
        in_specs=[
            pl.BlockSpec((tm, D), lambda i: (i, 0)),
            pl.BlockSpec((tm, 1), lambda i: (i, 0)),
            pl.BlockSpec((1, D), lambda i: (0, 0)),
            pl.BlockSpec((8, LANES), lambda i: (0, 0)),
            pl.BlockSpec((D, SEG_OFFS[-1]), lambda i: (0, 0), pipeline_mode=pl.Buffered(1)),
        ],
        out_specs=out_specs,
        out_shape=outs,
        scratch_shapes=[pltpu.VMEM((SEG_WIDTHS[5] // LANES, tm, LANES), F32)],
        compiler_params=_cparams(("arbitrary",), 56),
        name="inproj",
    )(x2, pos2, gain, tab, w_cat)


def _split3(a):
    a1 = a.astype(BF16)
    r1 = a - a1.astype(F32)
    a2 = r1.astype(BF16)
    a3 = (r1 - a2.astype(F32)).astype(BF16)
    return a1, a2, a3


def _gla_chunk(q_ref, k_ref, v_ref, o_ref, g_ref, st_ref, r0, backward):
    C = GLA_CHUNK
    row = lax.broadcasted_iota(I32, (C, C), 0)
    col = lax.broadcasted_iota(I32, (C, C), 1)
    if backward:
        tri = (col >= row).astype(BF16)
        keep = col > row
        ref_row, last_row = C // 2, 0
    else:
        tri = (col <= row).astype(BF16)
        keep = col <= row
        ref_row, last_row = C // 2 - 1, C - 1
    rows = pl.ds(r0, C)
    g1, g2, g3 = _split3(g_ref[rows, :])
    b = (jnp.dot(tri, g1, preferred_element_type=F32) + jnp.dot(tri, g2, preferred_element_type=F32)
         + jnp.dot(tri, g3, preferred_element_type=F32))
    b_ref = b[ref_row:ref_row + 1]
    b_last = b[last_row:last_row + 1]
    q = q_ref[rows, :].astype(F32) * (GLA_DK ** -0.5)
    k = k_ref[rows, :].astype(F32)
    qa = (q * jnp.exp(b - b_ref)).astype(BF16)
    ka = (k * jnp.exp(b_ref - b)).astype(BF16)
    qi = (q * jnp.exp(b)).astype(BF16)
    ks = k * jnp.exp(b_last - b)
    dec = jnp.broadcast_to(jnp.exp(b_last), (8, b.shape[1])).T
    for h in range(GLA_HEADS):
        hk = slice(h * GLA_DK, (h + 1) * GLA_DK)
        hv = slice(h * GLA_DV, (h + 1) * GLA_DV)
        v = v_ref[rows, hv]
        state = st_ref[h]
        s = lax.dot_general(qa[:, hk], ka[:, hk], (((1,), (1,)), ((), ())), preferred_element_type=F32)
        s = jnp.where(keep, s, 0.0).astype(BF16)
        o = (jnp.dot(s, v, preferred_element_type=F32)
             + jnp.dot(qi[:, hk], state.astype(BF16), preferred_element_type=F32))
        o_ref[rows, hv] = o.astype(BF16)
        kv = jnp.dot(ks[:, hk].T.astype(BF16), v, preferred_element_type=F32)
        st_ref[h] = dec[hk, 0:1] * state + kv


def _gla_kernel(qf_ref, kf_ref, vf_ref, af_ref, qb_ref, kb_ref, vb_ref, ab_ref, u_ref, bias_ref,
                of_ref, ob_ref, st_ref, g_scr, *, n_chunks):
    @pl.when(pl.program_id(1) == 0)
    def _():
        st_ref[...] = jnp.zeros_like(st_ref)

    for d, a_ref in enumerate((af_ref, ab_ref)):
        z = jnp.dot(a_ref[...], u_ref[d], preferred_element_type=F32) + bias_ref[d]
        g_scr[d] = (jnp.minimum(z, 0.0) - jnp.log(1.0 + jnp.exp(-jnp.abs(z)))) * (1.0 / GLA_TAU)

    def body(c, carry):
        rf = pl.multiple_of(c * GLA_CHUNK, GLA_CHUNK)
        rb = pl.multiple_of((n_chunks - 1 - c) * GLA_CHUNK, GLA_CHUNK)
        _gla_chunk(qf_ref, kf_ref, vf_ref, of_ref, g_scr.at[0], st_ref.at[0], rf, False)
        _gla_chunk(qb_ref, kb_ref, vb_ref, ob_ref, g_scr.at[1], st_ref.at[1], rb, True)
        return carry

    lax.fori_loop(0, n_chunks, body, 0)


def _gla(qg, kg, vg, al, u_pad, bias, n_chunks=16):
    B, S, _ = qg.shape
    rows = n_chunks * GLA_CHUNK
    NB = S // rows
    fwd = lambda b, n: (b, n, 0)
    bwd = lambda b, n: (b, NB - 1 - n, 0)
    specs = lambda im: [pl.BlockSpec((None, rows, t.shape[-1]), im) for t in (qg, kg, vg, al)]
    out = jax.ShapeDtypeStruct((B, S, GLA_HEADS * GLA_DV), BF16)
    return pl.pallas_call(
        functools.partial(_gla_kernel, n_chunks=n_chunks),
        grid=(B, NB),
        in_specs=specs(fwd) + specs(bwd) + [
            pl.BlockSpec(u_pad.shape, lambda b, n: (0, 0, 0)),
            pl.BlockSpec(bias.shape, lambda b, n: (0, 0, 0)),
        ],
        out_specs=[pl.BlockSpec((None, rows, GLA_HEADS * GLA_DV), fwd),
                   pl.BlockSpec((None, rows, GLA_HEADS * GLA_DV), bwd)],
        out_shape=[out, out],
        scratch_shapes=[pltpu.VMEM((2, GLA_HEADS, GLA_DK, GLA_DV), F32),
                        pltpu.VMEM((2, rows, GLA_HEADS * GLA_DK), F32)],
        compiler_params=_cparams(("arbitrary", "arbitrary"), 48),
        name="gla",
    )(qg, kg, vg, al, qg, kg, vg, al, u_pad, bias)


def _attn_kernel(q_ref, k_ref, v_ref, o_ref, lse_ref, *, sub, win, half):
    L = k_ref.shape[0]
    lq = q_ref.shape[0]
    base = pl.program_id(2) * lq

    def body(t, carry):
        r0 = pl.multiple_of(t * sub, sub)
        rows = pl.ds(r0, sub)
        q0 = base + r0
        k0 = pl.multiple_of(jnp.clip(q0 - half, 0, L - win), half)
        qpos = q0 + lax.broadcasted_iota(I32, (sub, 1), 0)
        kpos = k0 + lax.broadcasted_iota(I32, (1, win), 1)
        valid = jnp.abs(qpos - kpos) <= half
        q = q_ref[rows, :]
        k = k_ref[pl.ds(k0, win), :]
        v = v_ref[pl.ds(k0, win), :]
        for h in range(ATT_HPG):
            hs = slice(h * ATT_HD, (h + 1) * ATT_HD)
            s = lax.dot_general(q[:, hs], k[:, hs], (((1,), (1,)), ((), ())), preferred_element_type=F32)
            s = jnp.where(valid, s, -jnp.inf)
            m = jnp.max(s, axis=-1, keepdims=True)
            pr = jnp.exp(s - m)
            l = jnp.sum(pr, axis=-1, keepdims=True)
            o = jnp.dot(pr.astype(BF16), v[:, hs], preferred_element_type=F32) / l
            o_ref[rows, hs] = o.astype(BF16)
            lse_ref[rows, hs] = jnp.broadcast_to(m + jnp.log(l), (sub, ATT_HD))
        return carry

    lax.fori_loop(0, lq // sub, body, 0)


def _attn_group(q, k, v, window, lq=1024, sub=128):
    B, d, L, _ = q.shape
    half = window // (2 * d)
    lq = min(lq, L)
    sub = min(sub, lq)
    win = min(sub + 2 * half, L)
    qmap = lambda b, r, i: (b, r, i, 0)
    kmap = lambda b, r, i: (b, r, 0, 0)
    return pl.pallas_call(
        functools.partial(_attn_kernel, sub=sub, win=win, half=half),
        grid=(B, d, L // lq),
        in_specs=[pl.BlockSpec((None, None, lq, ATT_GW), qmap),
                  pl.BlockSpec((None, None, L, ATT_GW), kmap),
                  pl.BlockSpec((None, None, L, ATT_GW), kmap)],
        out_specs=[pl.BlockSpec((None, None, lq, ATT_GW), qmap), pl.BlockSpec((None, None, lq, ATT_GW), qmap)],
        out_shape=[jax.ShapeDtypeStruct(q.shape, BF16), jax.ShapeDtypeStruct(q.shape, F32)],
        compiler_params=_cparams(("arbitrary", "arbitrary", "arbitrary"), 40),
        name=f"attn_d{d}",
    )(q, k, v)


def _sigmoid(t):
    return 1.0 / (1.0 + jnp.exp(-t))


def _merge_kernel(of_ref, ob_ref, rg_ref, gn_ref, a0_ref, a1_ref, a2_ref, l0_ref, l1_ref, l2_ref,
                  ga_ref, gb_ref, x_ref, wa_ref, wb_ref, wo_ref, nf_ref, rw_ref, rb_ref,
                  x1_ref, h2_ref, ti_ref, tw_ref, *scratch):
    def token_rows(blk_ref, scr):
        d, n, w = blk_ref.shape
        if d == 1:
            return blk_ref[0].astype(F32)
        for r in range(d):
            for j in range(w // LANES):
                scr[j, pl.ds(r, n, stride=d), :] = blk_ref[r, :, j * LANES:(j + 1) * LANES].astype(F32)
        return jnp.concatenate([scr[j] for j in range(w // LANES)], axis=-1)

    o = of_ref[...].astype(F32) + ob_ref[...].astype(F32)
    parts = []
    for h in range(GLA_HEADS):
        oh = o[:, h * GLA_DV:(h + 1) * GLA_DV]
        parts.append(oh * lax.rsqrt(jnp.mean(oh * oh, axis=-1, keepdims=True) + EPS))
    r = rg_ref[...].astype(F32)
    y_gla = jnp.concatenate(parts, axis=-1) * gn_ref[...] * (r * _sigmoid(r))

    l0, l1, l2 = (token_rows(l_ref, scr) for l_ref, scr in zip((l0_ref, l1_ref, l2_ref), scratch[0:3]))
    a0, a1, a2 = (token_rows(a_ref, scr) for a_ref, scr in zip((a0_ref, a1_ref, a2_ref), scratch[3:6]))
    m = jnp.maximum(jnp.maximum(l0, l1), l2)
    e0, e1, e2 = jnp.exp(l0 - m), jnp.exp(l1 - m), jnp.exp(l2 - m)
    y_att = (e0 * a0 + e1 * a1 + e2 * a2) / (e0 + e1 + e2)

    t_gla = jnp.dot(y_gla.astype(BF16), wa_ref[...], preferred_element_type=F32)
    t_att = jnp.dot(y_att.astype(BF16), wb_ref[...], preferred_element_type=F32)
    merged = _sigmoid(ga_ref[...].astype(F32)) * t_gla + _sigmoid(gb_ref[...].astype(F32)) * t_att
    x1 = x_ref[...] + jnp.dot(merged.astype(BF16), wo_ref[...], preferred_element_type=F32)
    x1_ref[...] = x1
    h2 = x1 * lax.rsqrt(jnp.mean(x1 * x1, axis=-1, keepdims=True) + EPS) * nf_ref[...]
    _to_slabs(h2_ref, h2)

    h_hi = h2.astype(BF16)
    h_lo = (h2 - h_hi.astype(F32)).astype(BF16)
    logits = jnp.dot(jnp.concatenate([h_hi, h_lo, h_hi], axis=-1), rw_ref[...],
                     preferred_element_type=F32) + rb_ref[...]
    lane = lax.broadcasted_iota(I32, logits.shape, 1).astype(F32)
    vals, idxs = [], []
    for _ in range(TOP_K):
        mx = jnp.max(logits, axis=-1, keepdims=True)
        ix = jnp.min(jnp.where(logits == mx, lane, float(LANES)), axis=-1, keepdims=True)
        vals.append(mx)
        idxs.append(ix)
        logits = jnp.where(lane == ix, -jnp.inf, logits)
    es = [jnp.exp(vk - vals[0]) for vk in vals]
    den = es[0] + es[1] + es[2] + es[3]
    ti = jnp.zeros(lane.shape, F32)
    tw = jnp.zeros(lane.shape, F32)
    for kk in range(TOP_K):
        ti = jnp.where(lane == float(kk), idxs[kk], ti)
        tw = jnp.where(lane == float(kk), es[kk] / den, tw)
    ti_ref[...] = ti.astype(I32)
    tw_ref[...] = tw


def _merge(o_f, o_b, rg, gn, atts, lses, ga, gb, x2, wa, wb, wo, nf, rw, rb, tm=256):
    T, D = x2.shape
    B = atts[0].shape[0]
    nb = T // B // tm
    row = lambda w: pl.BlockSpec((tm, w), lambda i: (i, 0))
    const = lambda a: pl.BlockSpec(a.shape, lambda i: (0, 0))
    res = lambda a: pl.BlockSpec((None, a.shape[1], tm // a.shape[1], ATT_GW), lambda i: (i // nb, 0, i % nb, 0))
    return pl.pallas_call(
        _merge_kernel,
        grid=(T // tm,),
        in_specs=[row(D), row(D), row(D), const(gn)] + [res(a) for a in atts] + [res(a) for a in lses]
                 + [row(D), row(D), row(D), const(wa), const(wb), const(wo), const(nf), const(rw), const(rb)],
        out_specs=[row(D), pl.BlockSpec((tm * SLAB, LANES), lambda i: (i, 0)), row(LANES), row(LANES)],
        out_shape=[jax.ShapeDtypeStruct((T, D), F32), jax.ShapeDtypeStruct((T * SLAB, LANES), F32),
                   jax.ShapeDtypeStruct((T, LANES), I32), jax.ShapeDtypeStruct((T, LANES), F32)],
        scratch_shapes=[pltpu.VMEM((ATT_GW // LANES, tm, LANES), F32)] * 6,
        compiler_params=_cparams(("arbitrary",), 48),
        name="merge_router",
    )(o_f, o_b, rg, gn, *atts, *lses, ga, gb, x2, wa, wb, wo, nf, rw, rb)


SLAB = 8


def _to_slabs(ref, val):
    rows = val.shape[0]
    for s in range(SLAB):
        ref[pl.ds(s, rows, stride=SLAB), :] = val[:, s * LANES:(s + 1) * LANES]


def _from_slabs(ref, rows):
    return jnp.concatenate([ref[pl.ds(s, rows, stride=SLAB), :] for s in range(SLAB)], axis=-1)


K_PHASES = 4


def _experts_kernel(te_ref, dst_ref, h_hbm, w1_ref, b1_ref, w2_ref, b2_ref,
                    y_hbm, buf, w1p, w2b, sem_g, sem_s, *, tm, n_tok):
    i = pl.program_id(0)
    last = pl.num_programs(0) - 1
    slot = i % 2
    D = w2b.shape[0]

    def gather_start(row, s, r0, r1):
        for r in range(r0, r1):
            tok = dst_ref[row, r] & (n_tok - 1)
            pltpu.make_async_copy(h_hbm.at[pl.ds(pl.multiple_of(tok * SLAB, SLAB), SLAB), :],
                                  buf.at[s, pl.ds(r * SLAB, SLAB), :], sem_g.at[s]).start()

    def gather_wait(s):
        pltpu.make_async_copy(h_hbm.at[pl.ds(0, tm * SLAB), :], buf.at[s], sem_g.at[s]).wait()

    def scatter_start(row, s, r0, r1):
        for r in range(r0, r1):
            d = dst_ref[row, r]
            pltpu.make_async_copy(buf.at[2 + s, pl.ds(r * SLAB, SLAB), :],
                                  y_hbm.at[pl.ds(pl.multiple_of(d * SLAB, SLAB), SLAB), :], sem_s.at[s]
                                  ).start(priority=1)

    def scatter_wait(s):
        pltpu.make_async_copy(buf.at[2 + s], y_hbm.at[pl.ds(0, tm * SLAB), :], sem_s.at[s]).wait()

    @pl.when(i == 0)
    def _():
        buf[3] = jnp.zeros(buf.shape[1:], F32)
        gather_start(1, 0, 0, tm)

    @pl.when((i == 0) | (te_ref[i] != te_ref[jnp.maximum(i - 1, 0)]))
    def _():
        kk = lax.broadcasted_iota(I32, (2 * LANES, 2 * LANES), 0)
        nn = lax.broadcasted_iota(I32, (2 * LANES, 2 * LANES), 1)
        perm = (kk == jnp.where(nn < LANES, 2 * nn, 2 * (nn - LANES) + 1)).astype(BF16)
        for c in range(w1p.shape[1] // (2 * LANES)):
            cs = slice(c * 2 * LANES, (c + 1) * 2 * LANES)
            w1p[:, cs] = jnp.dot(w1_ref[:, cs].astype(BF16), perm, preferred_element_type=F32).astype(BF16)
        w2b[...] = w2_ref[...].astype(BF16)

    gather_wait(slot)
    nxt = jnp.minimum(i + 2, last)
    rows_per = tm // K_PHASES
    slabs_per = SLAB // K_PHASES
    hid = b1_ref[...]
    for j in range(K_PHASES):
        gather_start(nxt, 1 - slot, j * rows_per, (j + 1) * rows_per)
        scatter_start(i, 1 - slot, j * rows_per, (j + 1) * rows_per)
        xj = jnp.concatenate([buf[slot, pl.ds(s, tm, stride=SLAB), :]
                              for s in range(j * slabs_per, (j + 1) * slabs_per)], axis=-1).astype(BF16)
        kw = slabs_per * LANES
        hid = hid + jnp.dot(xj, w1p[j * kw:(j + 1) * kw, :], preferred_element_type=F32)
    acts = []
    for c in range(D // LANES):
        gate = jnp.minimum(hid[:, 2 * c * LANES:(2 * c + 1) * LANES], SWIGLU_LIMIT)
        up = jnp.clip(hid[:, (2 * c + 1) * LANES:(2 * c + 2) * LANES], -SWIGLU_LIMIT, SWIGLU_LIMIT)
        acts.append(((up + 1.0) * (gate * _sigmoid(gate * SWIGLU_ALPHA))).astype(BF16))
    act = jnp.concatenate(acts, axis=-1)
    out = jnp.dot(act, w2b[...], preferred_element_type=F32) + b2_ref[...]

    @pl.when(i > 0)
    def _():
        scatter_wait(slot)

    _to_slabs(buf.at[2 + slot], out)

    @pl.when(i == last)
    def _():
        scatter_wait(1 - slot)
        gather_wait(1 - slot)


def _experts(te_ext, dst_ext, h2s, w1, b1p, w2, b2, tm):
    T = h2s.shape[0] // SLAB
    n_steps = dst_ext.shape[0]
    _, D, F = w1.shape
    by_expert = lambda r, c: pl.BlockSpec((None, r, c), lambda i, te, dst: (te[i], 0, 0))
    return pl.pallas_call(
        functools.partial(_experts_kernel, tm=tm, n_tok=T),
        grid_spec=pltpu.PrefetchScalarGridSpec(
            num_scalar_prefetch=2,
            grid=(n_steps,),
            in_specs=[pl.BlockSpec(memory_space=pl.ANY), by_expert(D, F), by_expert(1, F),
                      by_expert(F // 2, D), by_expert(1, D)],
            out_specs=pl.BlockSpec(memory_space=pl.ANY),
            scratch_shapes=[pltpu.VMEM((4, tm * SLAB, LANES), F32),
                            pltpu.VMEM((D, F), BF16), pltpu.VMEM((F // 2, D), BF16),
                            pltpu.SemaphoreType.DMA((2,)), pltpu.SemaphoreType.DMA((2,))],
        ),
        out_shape=jax.ShapeDtypeStruct((n_steps * tm * SLAB, LANES), F32),
        compiler_params=pltpu.CompilerParams(dimension_semantics=("arbitrary",), vmem_limit_bytes=56 * MIB,
                                             disable_bounds_checks=True),
        name="experts",
    )(te_ext, dst_ext, h2s, w1, b1p, w2, b2)


def _route(topi, tm):
    T = topi.shape[0]
    E = N_EXPERTS
    n_asg = T * TOP_K
    n_tiles = n_asg // tm + E
    e_flat = topi.reshape(n_asg)
    sorted_e, order = lax.sort((e_flat, jnp.arange(n_asg, dtype=I32)), num_keys=1, is_stable=True)
    cend = jnp.searchsorted(sorted_e, jnp.arange(1, E + 1, dtype=I32), side="left").astype(I32)
    cstart = jnp.concatenate([jnp.zeros((1,), I32), cend[:-1]])
    counts = cend - cstart
    padded = (counts + tm - 1) // tm * tm
    pend = jnp.cumsum(padded)
    pstart = pend - padded
    n_used = pend[-1] // tm
    tstart = jnp.arange(n_tiles, dtype=I32) * tm
    te_raw = jnp.sum(tstart[:, None] >= pend[None, :], axis=1, dtype=I32)
    last_e = jnp.sum((n_used - 1) * tm >= pend, dtype=I32)
    te = jnp.where(te_raw < E, te_raw, last_e)
    esc = jnp.minimum(te_raw, E - 1)
    j0 = tstart - pstart[esc]
    n_valid = jnp.where(te_raw < E, jnp.clip(counts[esc] - j0, 0, tm), 0)
    col = jnp.arange(tm, dtype=I32)[None, :]
    a = order[jnp.clip((cstart[esc] + j0)[:, None] + col, 0, n_asg - 1)]
    dst_valid = (a % TOP_K) * T + a // TOP_K
    cend_ext = jnp.concatenate([cend, jnp.full((1,), n_asg, I32)])
    dst_pad = (n_asg + tstart - cend_ext[te_raw])[:, None] + col
    dst = jnp.where(col < n_valid[:, None], dst_valid, dst_pad).astype(I32)
    dump = n_tiles * tm + jnp.arange(tm, dtype=I32)
    dst_ext = jnp.concatenate([dump[None, :], dst], axis=0)
    te_ext = jnp.concatenate([te, te[-1:]])
    return te_ext, dst_ext


def _final_kernel(x1_ref, y0_ref, y1_ref, y2_ref, y3_ref, tw_ref, p_ref, np_ref, wg_ref, wp_ref, nfin_ref, o_ref):
    tw = tw_ref[...]
    x2 = x1_ref[...]
    for kk, y_ref in enumerate((y0_ref, y1_ref, y2_ref, y3_ref)):
        x2 = x2 + tw[:, kk:kk + 1] * _from_slabs(y_ref, x2.shape[0])
    h3 = (x2 * lax.rsqrt(jnp.mean(x2 * x2, axis=-1, keepdims=True) + EPS) * np_ref[...]).astype(BF16)
    gate = _sigmoid(jnp.dot(h3, wg_ref[...], preferred_element_type=F32))
    pe = jnp.dot(p_ref[...].astype(BF16), wp_ref[...], preferred_element_type=F32)
    x3 = x2 + gate * pe
    o_ref[...] = x3 * lax.rsqrt(jnp.mean(x3 * x3, axis=-1, keepdims=True) + EPS) * nfin_ref[...]


def _final(x1, ybuf, tw, p2, n_ple, wg, wp, n_fin, tm=256):
    T, D = x1.shape
    nb = T // tm
    row = lambda w: pl.BlockSpec((tm, w), lambda i: (i, 0))
    const = lambda a: pl.BlockSpec(a.shape, lambda i: (0, 0))
    yspec = lambda kk: pl.BlockSpec((tm * SLAB, LANES), lambda i: (kk * nb + i, 0))
    return pl.pallas_call(
        _final_kernel,
        grid=(nb,),
        in_specs=[row(D)] + [yspec(kk) for kk in range(TOP_K)] + [row(LANES), row(p2.shape[1]),
                  const(n_ple), const(wg), const(wp), const(n_fin)],
        out_specs=row(D),
        out_shape=jax.ShapeDtypeStruct((T, D), F32),
        compiler_params=_cparams(("arbitrary",), 48),
        name="final",
    )(x1, ybuf, ybuf, ybuf, ybuf, tw, p2, n_ple, wg, wp, n_fin)


def kernel(x, p, positions, norm_mix, w_in, gla_fgate_up, gla_fgate_bias, gla_out_norm, w_branch_gla, w_branch_attn, w_out, norm_ffn, router_w, router_b, expert_w1, expert_b1, expert_w2, expert_b2, norm_ple, ple_gate_w, ple_proj, norm_final):
    B, S, D = x.shape
    T = B * S
    assert w_in.shape[0] == 1, "single-layer block: the final norm is fused into the layer's last kernel"
    x2 = x.reshape(T, D)
    pos2 = positions.reshape(T, 1)

    w = w_in[0]
    w_cat = jnp.concatenate([
        w[:, :3072],
        jnp.pad(w[:, 3072:3104], ((0, 0), (0, LANES - 2 * GLA_RANK))),
        w[:, 3104:3104 + 768] * (ATT_HD ** -0.5),
        w[:, 3104 + 768:],
    ], axis=1).astype(BF16)
    inv_freq = ROPE_THETA ** (-jnp.arange(0, ROT_DIM, 2, dtype=F32) / ROT_DIM)
    lane_f = (jnp.arange(LANES) % ATT_HD) % (ROT_DIM // 2)
    tab = jnp.zeros((8, LANES), F32).at[0].set(inv_freq[lane_f])
    u_pad = jnp.zeros((2, LANES, GLA_HEADS * GLA_DK), F32)
    u_pad = u_pad.at[0, :GLA_RANK].set(gla_fgate_up[0, 0]).at[1, GLA_RANK:2 * GLA_RANK].set(gla_fgate_up[0, 1])
    u_pad = u_pad.astype(BF16)
    bias = gla_fgate_bias[0].reshape(2, 1, GLA_HEADS * GLA_DK)

    qg, kg, vg, rg, al, ga, gb, *att_in = _inproj(x2, pos2, norm_mix[0].reshape(1, D), tab, w_cat, B)
    r3 = lambda t: t.reshape(B, S, t.shape[-1])
    o_f, o_b = _gla(r3(qg), r3(kg), r3(vg), r3(al), u_pad, bias)
    atts, lses = [], []
    for gi, (window, _) in enumerate(ATT_GROUPS):
        o_g, lse_g = _attn_group(*att_in[3 * gi:3 * gi + 3], window)
        atts.append(o_g)
        lses.append(lse_g)

    rw = jnp.pad(router_w[0], ((0, 0), (0, LANES - N_EXPERTS)))
    rw_hi = rw.astype(BF16)
    rw = jnp.concatenate([rw_hi, rw_hi, (rw - rw_hi.astype(F32)).astype(BF16)], axis=0)
    rb = jnp.concatenate([router_b[0], jnp.full((LANES - N_EXPERTS,), -jnp.inf, F32)]).reshape(1, LANES)
    x1, h2, topi, topw = _merge(
        o_f.reshape(T, D), o_b.reshape(T, D), rg, gla_out_norm[0].reshape(1, D), atts, lses, ga, gb, x2,
        w_branch_gla[0].astype(BF16), w_branch_attn[0].astype(BF16), w_out[0].astype(BF16),
        norm_ffn[0].reshape(1, D), rw, rb)

    tm_e = 256
    assert T & (T - 1) == 0, "token count must be a power of two (row index is masked out of the slot code)"
    te, dst2d = _route(topi[:, :TOP_K], tm_e)
    E, _, F = expert_w1[0].shape
    b1p = expert_b1[0].reshape(E, F // (2 * LANES), LANES, 2).transpose(0, 1, 3, 2).reshape(E, 1, F)
    ybuf = _experts(te, dst2d, h2, expert_w1[0], b1p, expert_w2[0], expert_b2[0][:, None, :], tm_e)

    out = _final(x1, ybuf, topw, p[0].reshape(T, -1), norm_ple[0].reshape(1, D), ple_gate_w[0].astype(BF16),
                 ple_proj[0].astype(BF16), norm_final.reshape(1, D))
    return out.reshape(B, S, D)
```

```python
import functools

import jax
import jax.numpy as jnp
from jax import lax
from jax.experimental import pallas as pl
from jax.experimental.pallas import tpu as pltpu

F32 = jnp.float32
BF16 = jnp.bfloat16
I32 = jnp.int32

EPS = 1e-6
GLA_HEADS = 4
GLA_DK = 128
GLA_DV = 256
GLA_RANK = 16
GLA_TAU = 16.0
GLA_CHUNK = 64
ATT_GROUPS = ((128, 1), (512, 4), (2048, 16))
ATT_HPG = 4
ATT_HD = 64
ATT_GW = ATT_HPG * ATT_HD
ROT_DIM = 16
ROPE_THETA = 500000.0
N_EXPERTS = 32
TOP_K = 4
SWIGLU_ALPHA = 1.702
SWIGLU_LIMIT = 7.0

LANES = 128
MIB = 1024 * 1024

SEG_WIDTHS = (512, 512, 1024, 1024, LANES, 768, 768, 768, 1024, 1024)
SEG_OFFS = tuple(sum(SEG_WIDTHS[:i]) for i in range(len(SEG_WIDTHS) + 1))


def _cparams(sem, vmem_mib):
    return pltpu.CompilerParams(dimension_semantics=sem, vmem_limit_bytes=vmem_mib * MIB)


def _inproj_kernel(x_ref, pos_ref, g_ref, tab_ref, w_ref, *refs):
    (qg_ref, kg_ref, vg_ref, rg_ref, al_ref, ga_ref, gb_ref), att_refs, ysc = refs[:7], refs[7:16], refs[16]
    x = x_ref[...]
    h = (x * lax.rsqrt(jnp.mean(x * x, axis=-1, keepdims=True) + EPS) * g_ref[...]).astype(BF16)

    def proj(seg):
        return jnp.dot(h, w_ref[:, SEG_OFFS[seg]:SEG_OFFS[seg + 1]], preferred_element_type=F32)

    qg_ref[...] = proj(0).astype(BF16)
    kg_ref[...] = proj(1).astype(BF16)
    vg_ref[...] = proj(2).astype(BF16)
    rg_ref[...] = proj(3).astype(BF16)
    al_ref[...] = proj(4).astype(BF16)
    ga_ref[...] = proj(8).astype(BF16)
    gb_ref[...] = proj(9).astype(BF16)

    ang = pos_ref[...].astype(F32) * tab_ref[0:1, :]
    cs = jnp.cos(ang)
    sn = jnp.sin(ang)
    lane = lax.broadcasted_iota(I32, (1, LANES), 1) % ATT_HD
    c_mul = jnp.where(lane < ROT_DIM, cs, 1.0)
    s_next = jnp.where(lane < ROT_DIM // 2, -sn, 0.0)
    s_prev = jnp.where((lane >= ROT_DIM // 2) & (lane < ROT_DIM), sn, 0.0)

    tm = ysc.shape[1]
    tiles_per_group = ATT_GW // LANES

    def emit(seg, which, rotary):
        y = proj(seg)
        for j in range(SEG_WIDTHS[seg] // LANES):
            t = y[:, j * LANES:(j + 1) * LANES]
            if rotary:
                t = (t * c_mul + pltpu.roll(t, LANES - ROT_DIM // 2, axis=1) * s_next
                     + pltpu.roll(t, ROT_DIM // 2, axis=1) * s_prev)
            ysc[j] = t
        for gi, (_, d) in enumerate(ATT_GROUPS):
            o_ref = att_refs[3 * gi + which]
            for r in range(d):
                for j in range(tiles_per_group):
                    o_ref[r, :, j * LANES:(j + 1) * LANES] = ysc[
                        gi * tiles_per_group + j, pl.ds(r, tm // d, stride=d), :].astype(BF16)

    emit(5, 0, True)
    emit(6, 1, True)
    emit(7, 2, False)


def _inproj(x2, pos2, gain, tab, w_cat, B, tm=512):
    T, D = x2.shape
    S = T // B
    nb = S // tm
    row_widths = [SEG_WIDTHS[s] for s in (0, 1, 2, 3, 4, 8, 9)]
    outs = [jax.ShapeDtypeStruct((T, w), BF16) for w in row_widths]
    out_specs = [pl.BlockSpec((tm, w), lambda i: (i, 0)) for w in row_widths]
    for _, d in ATT_GROUPS:
        for _ in range(3):
            outs.append(jax.ShapeDtypeStruct((B, d, S // d, ATT_GW), BF16))
            out_specs.append(pl.BlockSpec((None, d, tm // d, ATT_GW), lambda i: (i // nb, 0, i % nb, 0)))
    return pl.pallas_call(
        _inproj_kernel,
        grid=(T // tm,),
        in_specs=[
            pl.BlockSpec((tm, D), lambda i: (i, 0)),
            pl.BlockSpec((tm, 1), lambda i: (i, 0)),
            pl.BlockSpec((1, D), lambda i: (0, 0)),
            pl.BlockSpec((8, LANES), lambda i: (0, 0)),
            pl.BlockSpec((D, SEG_OFFS[-1]), lambda i: (0, 0), pipeline_mode=pl.Buffered(1)),
        ],
        out_specs=out_specs,
        out_shape=outs,
        scratch_shapes=[pltpu.VMEM((SEG_WIDTHS[5] // LANES, tm, LANES), F32)],
        compiler_params=_cparams(("arbitrary",), 56),
        name="inproj",
    )(x2, pos2, gain, tab, w_cat)


def _split3(a):
    a1 = a.astype(BF16)
    r1 = a - a1.astype(F32)
    a2 = r1.astype(BF16)
    a3 = (r1 - a2.astype(F32)).astype(BF16)
    return a1, a2, a3


GLA_UNROLL = 2


def _gla_chunk(q_ref, k_ref, v_ref, o_ref, g_ref, st_ref, r0, backward):
    C = GLA_CHUNK
    row = lax.broadcasted_iota(I32, (C, C), 0)
    col = lax.broadcasted_iota(I32, (C, C), 1)
    if backward:
        tri = (col >= row).astype(BF16)
        keep = col > row
        ref_row, last_row = C // 2, 0
    else:
        tri = (col <= row).astype(BF16)
        keep = col <= row
        ref_row, last_row = C // 2 - 1, C - 1
    rows = pl.ds(r0, C)
    g1, g2, g3 = _split3(g_ref[rows, :])
    b = (jnp.dot(tri, g1, preferred_element_type=F32) + jnp.dot(tri, g2, preferred_element_type=F32)
         + jnp.dot(tri, g3, preferred_element_type=F32))
    b_ref = b[ref_row:ref_row + 1]
    b_last = b[last_row:last_row + 1]
    q = q_ref[rows, :].astype(F32) * (GLA_DK ** -0.5)
    k = k_ref[rows, :].astype(F32)
    qa = (q * jnp.exp(b - b_ref)).astype(BF16)
    ka = (k * jnp.exp(b_ref - b)).astype(BF16)
    qi = (q * jnp.exp(b)).astype(BF16)
    ks = k * jnp.exp(b_last - b)
    dec = jnp.broadcast_to(jnp.exp(b_last), (8, b.shape[1])).T
    for h in range(GLA_HEADS):
        hk = slice(h * GLA_DK, (h + 1) * GLA_DK)
        hv = slice(h * GLA_DV, (h + 1) * GLA_DV)
        v = v_ref[rows, hv]
        state = st_ref[h]
        s = lax.dot_general(qa[:, hk], ka[:, hk], (((1,), (1,)), ((), ())), preferred_element_type=F32)
        s = jnp.where(keep, s, 0.0).astype(BF16)
        o = jnp.dot(jnp.concatenate([qi[:, hk], s], axis=-1),
                    jnp.concatenate([state.astype(BF16), v], axis=0), preferred_element_type=F32)
        o_ref[rows, hv] = o.astype(BF16)
        kv = jnp.dot(ks[:, hk].T.astype(BF16), v, preferred_element_type=F32)
        st_ref[h] = dec[hk, 0:1] * state + kv


def _gla_kernel(qf_ref, kf_ref, vf_ref, af_ref, qb_ref, kb_ref, vb_ref, ab_ref, u_ref, bias_ref,
                of_ref, ob_ref, st_ref, g_scr, *, n_chunks):
    @pl.when(pl.program_id(1) == 0)
    def _():
        st_ref[...] = jnp.zeros_like(st_ref)

    for d, a_ref in enumerate((af_ref, ab_ref)):
        z = jnp.dot(a_ref[...], u_ref[d], preferred_element_type=F32) + bias_ref[d]
        g_scr[d] = (jnp.minimum(z, 0.0) - jnp.log(1.0 + jnp.exp(-jnp.abs(z)))) * (1.0 / GLA_TAU)

    def body(it, carry):
        for u in range(GLA_UNROLL):
            c = it * GLA_UNROLL + u
            rf = pl.multiple_of(c * GLA_CHUNK, GLA_CHUNK)
            rb = pl.multiple_of((n_chunks - 1 - c) * GLA_CHUNK, GLA_CHUNK)
            _gla_chunk(qf_ref, kf_ref, vf_ref, of_ref, g_scr.at[0], st_ref.at[0], rf, False)
            _gla_chunk(qb_ref, kb_ref, vb_ref, ob_ref, g_scr.at[1], st_ref.at[1], rb, True)
        return carry

    lax.fori_loop(0, n_chunks // GLA_UNROLL, body, 0)


def _gla(qg, kg, vg, al, u_pad, bias, n_chunks=16):
    B, S, _ = qg.shape
    rows = n_chunks * GLA_CHUNK
    NB = S // rows
    fwd = lambda b, n: (b, n, 0)
    bwd = lambda b, n: (b, NB - 1 - n, 0)
    specs = lambda im: [pl.BlockSpec((None, rows, t.shape[-1]), im) for t in (qg, kg, vg, al)]
    out = jax.ShapeDtypeStruct((B, S, GLA_HEADS * GLA_DV), BF16)
    return pl.pallas_call(
        functools.partial(_gla_kernel, n_chunks=n_chunks),
        grid=(B, NB),
        in_specs=specs(fwd) + specs(bwd) + [
            pl.BlockSpec(u_pad.shape, lambda b, n: (0, 0, 0)),
            pl.BlockSpec(bias.shape, lambda b, n: (0, 0, 0)),
        ],
        out_specs=[pl.BlockSpec((None, rows, GLA_HEADS * GLA_DV), fwd),
                   pl.BlockSpec((None, rows, GLA_HEADS * GLA_DV), bwd)],
        out_shape=[out, out],
        scratch_shapes=[pltpu.VMEM((2, GLA_HEADS, GLA_DK, GLA_DV), F32),
                        pltpu.VMEM((2, rows, GLA_HEADS * GLA_DK), F32)],
        compiler_params=_cparams(("arbitrary", "arbitrary"), 48),
        name="gla",
    )(qg, kg, vg, al, qg, kg, vg, al, u_pad, bias)


def _attn_kernel(q_ref, k_ref, v_ref, o_ref, lse_ref, *, sub, win, half):
    L = k_ref.shape[0]
    lq = q_ref.shape[0]
    base = pl.program_id(2) * lq

    def body(t, carry):
        r0 = pl.multiple_of(t * sub, sub)
        rows = pl.ds(r0, sub)
        q0 = base + r0
        k0 = pl.multiple_of(jnp.clip(q0 - half, 0, L - win), half)
        qpos = q0 + lax.broadcasted_iota(I32, (sub, 1), 0)
        kpos = k0 + lax.broadcasted_iota(I32, (1, win), 1)
        valid = jnp.abs(qpos - kpos) <= half
        q = q_ref[rows, :]
        k = k_ref[pl.ds(k0, win), :]
        v = v_ref[pl.ds(k0, win), :]
        for h in range(ATT_HPG):
            hs = slice(h * ATT_HD, (h + 1) * ATT_HD)
            s = lax.dot_general(q[:, hs], k[:, hs], (((1,), (1,)), ((), ())), preferred_element_type=F32)
            s = jnp.where(valid, s, -jnp.inf)
            m = jnp.max(s, axis=-1, keepdims=True)
            pr = jnp.exp(s - m)
            l = jnp.sum(pr, axis=-1, keepdims=True)
            o = jnp.dot(pr.astype(BF16), v[:, hs], preferred_element_type=F32) / l
            o_ref[rows, hs] = o.astype(BF16)
            lse_ref[rows, hs] = jnp.broadcast_to(m + jnp.log(l), (sub, ATT_HD))
        return carry

    lax.fori_loop(0, lq // sub, body, 0)


def _attn_group(q, k, v, window, lq=1024, sub=128):
    B, d, L, _ = q.shape
    half = window // (2 * d)
    lq = min(lq, L)
    sub = min(sub, lq)
    win = min(sub + 2 * half, L)
    qmap = lambda b, r, i: (b, r, i, 0)
    kmap = lambda b, r, i: (b, r, 0, 0)
    return pl.pallas_call(
        functools.partial(_attn_kernel, sub=sub, win=win, half=half),
        grid=(B, d, L // lq),
        in_specs=[pl.BlockSpec((None, None, lq, ATT_GW), qmap),
                  pl.BlockSpec((None, None, L, ATT_GW), kmap),
                  pl.BlockSpec((None, None, L, ATT_GW), kmap)],
        out_specs=[pl.BlockSpec((None, None, lq, ATT_GW), qmap), pl.BlockSpec((None, None, lq, ATT_GW), qmap)],
        out_shape=[jax.ShapeDtypeStruct(q.shape, BF16), jax.ShapeDtypeStruct(q.shape, F32)],
        compiler_params=_cparams(("arbitrary", "arbitrary", "arbitrary"), 40),
        name=f"attn_d{d}",
    )(q, k, v)


def _sigmoid(t):
    return 1.0 / (1.0 + jnp.exp(-t))


def _merge_kernel(of_ref, ob_ref, rg_ref, gn_ref, a0_ref, a1_ref, a2_ref, l0_ref, l1_ref, l2_ref,
                  ga_ref, gb_ref, x_ref, wa_ref, wb_ref, wo_ref, nf_ref, rw_ref, rb_ref,
                  x1_ref, h2_ref, ti_ref, tw_ref, cnt_ref, *scratch):
    def token_rows(blk_ref, scr):
        d, n, w = blk_ref.shape
        if d == 1:
            return blk_ref[0].astype(F32)
        for r in range(d):
            for j in range(w // LANES):
                scr[j, pl.ds(r, n, stride=d), :] = blk_ref[r, :, j * LANES:(j + 1) * LANES].astype(F32)
        return jnp.concatenate([scr[j] for j in range(w // LANES)], axis=-1)

    o = of_ref[...].astype(F32) + ob_ref[...].astype(F32)
    parts = []
    for h in range(GLA_HEADS):
        oh = o[:, h * GLA_DV:(h + 1) * GLA_DV]
        parts.append(oh * lax.rsqrt(jnp.mean(oh * oh, axis=-1, keepdims=True) + EPS))
    r = rg_ref[...].astype(F32)
    y_gla = jnp.concatenate(parts, axis=-1) * gn_ref[...] * (r * _sigmoid(r))

    l0, l1, l2 = (token_rows(l_ref, scr) for l_ref, scr in zip((l0_ref, l1_ref, l2_ref), scratch[0:3]))
    a0, a1, a2 = (token_rows(a_ref, scr) for a_ref, scr in zip((a0_ref, a1_ref, a2_ref), scratch[3:6]))
    m = jnp.maximum(jnp.maximum(l0, l1), l2)
    e0, e1, e2 = jnp.exp(l0 - m), jnp.exp(l1 - m), jnp.exp(l2 - m)
    y_att = (e0 * a0 + e1 * a1 + e2 * a2) / (e0 + e1 + e2)

    t_gla = jnp.dot(y_gla.astype(BF16), wa_ref[...], preferred_element_type=F32)
    t_att = jnp.dot(y_att.astype(BF16), wb_ref[...], preferred_element_type=F32)
    merged = _sigmoid(ga_ref[...].astype(F32)) * t_gla + _sigmoid(gb_ref[...].astype(F32)) * t_att
    x1 = x_ref[...] + jnp.dot(merged.astype(BF16), wo_ref[...], preferred_element_type=F32)
    x1_ref[...] = x1
    h2 = x1 * lax.rsqrt(jnp.mean(x1 * x1, axis=-1, keepdims=True) + EPS) * nf_ref[...]
    _to_slabs(h2_ref, h2)

    h_hi = h2.astype(BF16)
    h_lo = (h2 - h_hi.astype(F32)).astype(BF16)
    logits = jnp.dot(jnp.concatenate([h_hi, h_lo, h_hi], axis=-1), rw_ref[...],
                     preferred_element_type=F32) + rb_ref[...]
    lane = lax.broadcasted_iota(I32, logits.shape, 1).astype(F32)
    vals, idxs = [], []
    for _ in range(TOP_K):
        mx = jnp.max(logits, axis=-1, keepdims=True)
        ix = jnp.min(jnp.where(logits == mx, lane, float(LANES)), axis=-1, keepdims=True)
        vals.append(mx)
        idxs.append(ix)
        logits = jnp.where(lane == ix, -jnp.inf, logits)
    es = [jnp.exp(vk - vals[0]) for vk in vals]
    den = es[0] + es[1] + es[2] + es[3]
    ti = jnp.zeros(lane.shape, F32)
    tw = jnp.zeros(lane.shape, F32)
    for kk in range(TOP_K):
        ti = jnp.where(lane == float(kk), idxs[kk], ti)
        tw = jnp.where(lane == float(kk), es[kk] / den, tw)
    ti_ref[...] = ti.astype(I32)
    tw_ref[...] = tw

    @pl.when(pl.program_id(0) == 0)
    def _():
        cnt_ref[...] = jnp.zeros_like(cnt_ref)

    hits = sum((lane == ix).astype(F32) for ix in idxs)
    cnt_ref[...] += jnp.sum(hits, axis=0, keepdims=True)


def _merge(o_f, o_b, rg, gn, atts, lses, ga, gb, x2, wa, wb, wo, nf, rw, rb, tm=256):
    T, D = x2.shape
    B = atts[0].shape[0]
    nb = T // B // tm
    row = lambda w: pl.BlockSpec((tm, w), lambda i: (i, 0))
    const = lambda a: pl.BlockSpec(a.shape, lambda i: (0, 0))
    res = lambda a: pl.BlockSpec((None, a.shape[1], tm // a.shape[1], ATT_GW), lambda i: (i // nb, 0, i % nb, 0))
    return pl.pallas_call(
        _merge_kernel,
        grid=(T // tm,),
        in_specs=[row(D), row(D), row(D), const(gn)] + [res(a) for a in atts] + [res(a) for a in lses]
                 + [row(D), row(D), row(D), const(wa), const(wb), const(wo), const(nf), const(rw), const(rb)],
        out_specs=[row(D), pl.BlockSpec((tm * SLAB, LANES), lambda i: (i, 0)), row(LANES), row(LANES),
                   pl.BlockSpec((1, LANES), lambda i: (0, 0))],
        out_shape=[jax.ShapeDtypeStruct((T, D), F32), jax.ShapeDtypeStruct((T * SLAB, LANES), F32),
                   jax.ShapeDtypeStruct((T, LANES), I32), jax.ShapeDtypeStruct((T, LANES), F32),
                   jax.ShapeDtypeStruct((1, LANES), F32)],
        scratch_shapes=[pltpu.VMEM((ATT_GW // LANES, tm, LANES), F32)] * 6,
        compiler_params=_cparams(("arbitrary",), 48),
        name="merge_router",
    )(o_f, o_b, rg, gn, *atts, *lses, ga, gb, x2, wa, wb, wo, nf, rw, rb)


SLAB = 8


def _to_slabs(ref, val):
    rows = val.shape[0]
    for s in range(SLAB):
        ref[pl.ds(s, rows, stride=SLAB), :] = val[:, s * LANES:(s + 1) * LANES]


def _from_slabs(ref, rows):
    return jnp.concatenate([ref[pl.ds(s, rows, stride=SLAB), :] for s in range(SLAB)], axis=-1)


K_PHASES = 4
N_XBUF = 3


def _experts_kernel(te_ref, dst_ref, h_hbm, w1_ref, b1_ref, w2_ref, b2_ref,
                    y_hbm, buf, w1p, w2b, sem_g, sem_s, *, tm, n_tok):
    i = pl.program_id(0)
    last = pl.num_programs(0) - 1
    slot = i % 2
    xslot = i % N_XBUF
    D = w2b.shape[0]

    def gather_start(row, s, r0, r1):
        for r in range(r0, r1):
            tok = dst_ref[row, r] & (n_tok - 1)
            pltpu.make_async_copy(h_hbm.at[pl.ds(pl.multiple_of(tok * SLAB, SLAB), SLAB), :],
                                  buf.at[s, pl.ds(r * SLAB, SLAB), :], sem_g.at[s]).start()

    def gather_wait(s):
        pltpu.make_async_copy(h_hbm.at[pl.ds(0, tm * SLAB), :], buf.at[s], sem_g.at[s]).wait()

    def scatter_start(row, s, r0, r1):
        for r in range(r0, r1):
            d = dst_ref[row, r]
            pltpu.make_async_copy(buf.at[N_XBUF + s, pl.ds(r * SLAB, SLAB), :],
                                  y_hbm.at[pl.ds(pl.multiple_of(d * SLAB, SLAB), SLAB), :], sem_s.at[s]).start()

    def scatter_wait(s):
        pltpu.make_async_copy(buf.at[N_XBUF + s], y_hbm.at[pl.ds(0, tm * SLAB), :], sem_s.at[s]).wait()

    @pl.when(i == 0)
    def _():
        buf[N_XBUF + 1] = jnp.zeros(buf.shape[1:], F32)
        gather_start(1, 0, 0, tm)
        gather_start(jnp.minimum(2, last), 1, 0, tm)

    @pl.when((i == 0) | (te_ref[i] != te_ref[jnp.maximum(i - 1, 0)]))
    def _():
        kk = lax.broadcasted_iota(I32, (2 * LANES, 2 * LANES), 0)
        nn = lax.broadcasted_iota(I32, (2 * LANES, 2 * LANES), 1)
        perm = (kk == jnp.where(nn < LANES, 2 * nn, 2 * (nn - LANES) + 1)).astype(BF16)
        for c in range(w1p.shape[1] // (2 * LANES)):
            cs = slice(c * 2 * LANES, (c + 1) * 2 * LANES)
            w1p[:, cs] = jnp.dot(w1_ref[:, cs].astype(BF16), perm, preferred_element_type=F32).astype(BF16)
        w2b[...] = w2_ref[...].astype(BF16)

    gather_wait(xslot)
    nxt = jnp.minimum(i + 3, last)
    nxt_slot = (i + 2) % N_XBUF
    rows_per = tm // K_PHASES
    slabs_per = SLAB // K_PHASES
    hid = b1_ref[...]
    for j in range(K_PHASES):
        gather_start(nxt, nxt_slot, j * rows_per, (j + 1) * rows_per)
        scatter_start(i, 1 - slot, j * rows_per, (j + 1) * rows_per)
        xj = jnp.concatenate([buf[xslot, pl.ds(s, tm, stride=SLAB), :]
                              for s in range(j * slabs_per, (j + 1) * slabs_per)], axis=-1).astype(BF16)
        kw = slabs_per * LANES
        hid = hid + jnp.dot(xj, w1p[j * kw:(j + 1) * kw, :], preferred_element_type=F32)
    acts = []
    for c in range(D // LANES):
        gate = jnp.minimum(hid[:, 2 * c * LANES:(2 * c + 1) * LANES], SWIGLU_LIMIT)
        up = jnp.clip(hid[:, (2 * c + 1) * LANES:(2 * c + 2) * LANES], -SWIGLU_LIMIT, SWIGLU_LIMIT)
        acts.append(((up + 1.0) * (gate * _sigmoid(gate * SWIGLU_ALPHA))).astype(BF16))
    act = jnp.concatenate(acts, axis=-1)
    out = jnp.dot(act, w2b[...], preferred_element_type=F32) + b2_ref[...]

    @pl.when(i > 0)
    def _():
        scatter_wait(slot)

    _to_slabs(buf.at[N_XBUF + slot], out)

    @pl.when(i == last)
    def _():
        scatter_wait(1 - slot)
        gather_wait((i + 1) % N_XBUF)
        gather_wait((i + 2) % N_XBUF)


def _experts(te_ext, dst_ext, h2s, w1, b1p, w2, b2, tm):
    T = h2s.shape[0] // SLAB
    n_steps = dst_ext.shape[0]
    _, D, F = w1.shape
    by_expert = lambda r, c: pl.BlockSpec((None, r, c), lambda i, te, dst: (te[i], 0, 0))
    return pl.pallas_call(
        functools.partial(_experts_kernel, tm=tm, n_tok=T),
        grid_spec=pltpu.PrefetchScalarGridSpec(
            num_scalar_prefetch=2,
            grid=(n_steps,),
            in_specs=[pl.BlockSpec(memory_space=pl.ANY), by_expert(D, F), by_expert(1, F),
                      by_expert(F // 2, D), by_expert(1, D)],
            out_specs=pl.BlockSpec(memory_space=pl.ANY),
            scratch_shapes=[pltpu.VMEM((N_XBUF + 2, tm * SLAB, LANES), F32),
                            pltpu.VMEM((D, F), BF16), pltpu.VMEM((F // 2, D), BF16),
                            pltpu.SemaphoreType.DMA((N_XBUF,)), pltpu.SemaphoreType.DMA((2,))],
        ),
        out_shape=jax.ShapeDtypeStruct((n_steps * tm * SLAB, LANES), F32),
        compiler_params=pltpu.CompilerParams(dimension_semantics=("arbitrary",), vmem_limit_bytes=56 * MIB,
                                             disable_bounds_checks=True),
        name="experts",
    )(te_ext, dst_ext, h2s, w1, b1p, w2, b2)


def _route(topi, counts, tm):
    T = topi.shape[0]
    E = N_EXPERTS
    n_asg = T * TOP_K
    n_tiles = n_asg // tm + E
    e_flat = topi.reshape(n_asg)
    order = jnp.argsort(e_flat, stable=True).astype(I32)
    cend = jnp.cumsum(counts)
    cstart = cend - counts
    padded = (counts + tm - 1) // tm * tm
    pend = jnp.cumsum(padded)
    pstart = pend - padded
    n_used = pend[-1] // tm
    tstart = jnp.arange(n_tiles, dtype=I32) * tm
    te_raw = jnp.sum(tstart[:, None] >= pend[None, :], axis=1, dtype=I32)
    last_e = jnp.sum((n_used - 1) * tm >= pend, dtype=I32)
    te = jnp.where(te_raw < E, te_raw, last_e)
    esc = jnp.minimum(te_raw, E - 1)
    j0 = tstart - pstart[esc]
    n_valid = jnp.where(te_raw < E, jnp.clip(counts[esc] - j0, 0, tm), 0)
    col = jnp.arange(tm, dtype=I32)[None, :]
    a = order[jnp.clip((cstart[esc] + j0)[:, None] + col, 0, n_asg - 1)]
    dst_valid = (a % TOP_K) * T + a // TOP_K
    cend_ext = jnp.concatenate([cend, jnp.full((1,), n_asg, I32)])
    dst_pad = (n_asg + tstart - cend_ext[te_raw])[:, None] + col
    dst = jnp.where(col < n_valid[:, None], dst_valid, dst_pad).astype(I32)
    dump = n_tiles * tm + jnp.arange(tm, dtype=I32)
    dst_ext = jnp.concatenate([dump[None, :], dst], axis=0)
    te_ext = jnp.concatenate([te, te[-1:]])
    return te_ext, dst_ext


def _final_kernel(x1_ref, y0_ref, y1_ref, y2_ref, y3_ref, tw_ref, p_ref, np_ref, wg_ref, wp_ref, nfin_ref, o_ref):
    tw = tw_ref[...]
    x2 = x1_ref[...]
    for kk, y_ref in enumerate((y0_ref, y1_ref, y2_ref, y3_ref)):
        x2 = x2 + tw[:, kk:kk + 1] * _from_slabs(y_ref, x2.shape[0])
    h3 = (x2 * lax.rsqrt(jnp.mean(x2 * x2, axis=-1, keepdims=True) + EPS) * np_ref[...]).astype(BF16)
    gate = _sigmoid(jnp.dot(h3, wg_ref[...], preferred_element_type=F32))
    pe = jnp.dot(p_ref[...].astype(BF16), wp_ref[...], preferred_element_type=F32)
    x3 = x2 + gate * pe
    o_ref[...] = x3 * lax.rsqrt(jnp.mean(x3 * x3, axis=-1, keepdims=True) + EPS) * nfin_ref[...]


def _final(x1, ybuf, tw, p2, n_ple, wg, wp, n_fin, tm=256):
    T, D = x1.shape
    nb = T // tm
    row = lambda w: pl.BlockSpec((tm, w), lambda i: (i, 0))
    const = lambda a: pl.BlockSpec(a.shape, lambda i: (0, 0))
    yspec = lambda kk: pl.BlockSpec((tm * SLAB, LANES), lambda i: (kk * nb + i, 0))
    return pl.pallas_call(
        _final_kernel,
        grid=(nb,),
        in_specs=[row(D)] + [yspec(kk) for kk in range(TOP_K)] + [row(LANES), row(p2.shape[1]),
                  const(n_ple), const(wg), const(wp), const(n_fin)],
        out_specs=row(D),
        out_shape=jax.ShapeDtypeStruct((T, D), F32),
        compiler_params=_cparams(("arbitrary",), 48),
        name="final",
    )(x1, ybuf, ybuf, ybuf, ybuf, tw, p2, n_ple, wg, wp, n_fin)


def kernel(x, p, positions, norm_mix, w_in, gla_fgate_up, gla_fgate_bias, gla_out_norm, w_branch_gla, w_branch_attn, w_out, norm_ffn, router_w, router_b, expert_w1, expert_b1, expert_w2, expert_b2, norm_ple, ple_gate_w, ple_proj, norm_final):
    B, S, D = x.shape
    T = B * S
    assert w_in.shape[0] == 1, "single-layer block: the final norm is fused into the layer's last kernel"
    x2 = x.reshape(T, D)
    pos2 = positions.reshape(T, 1)

    w = w_in[0]
    w_cat = jnp.concatenate([
        w[:, :3072],
        jnp.pad(w[:, 3072:3104], ((0, 0), (0, LANES - 2 * GLA_RANK))),
        w[:, 3104:3104 + 768] * (ATT_HD ** -0.5),
        w[:, 3104 + 768:],
    ], axis=1).astype(BF16)
    inv_freq = ROPE_THETA ** (-jnp.arange(0, ROT_DIM, 2, dtype=F32) / ROT_DIM)
    lane_f = (jnp.arange(LANES) % ATT_HD) % (ROT_DIM // 2)
    tab = jnp.zeros((8, LANES), F32).at[0].set(inv_freq[lane_f])
    u_pad = jnp.zeros((2, LANES, GLA_HEADS * GLA_DK), F32)
    u_pad = u_pad.at[0, :GLA_RANK].set(gla_fgate_up[0, 0]).at[1, GLA_RANK:2 * GLA_RANK].set(gla_fgate_up[0, 1])
    u_pad = u_pad.astype(BF16)
    bias = gla_fgate_bias[0].reshape(2, 1, GLA_HEADS * GLA_DK)

    qg, kg, vg, rg, al, ga, gb, *att_in = _inproj(x2, pos2, norm_mix[0].reshape(1, D), tab, w_cat, B)
    r3 = lambda t: t.reshape(B, S, t.shape[-1])
    o_f, o_b = _gla(r3(qg), r3(kg), r3(vg), r3(al), u_pad, bias)
    atts, lses = [], []
    for gi, (window, _) in enumerate(ATT_GROUPS):
        o_g, lse_g = _attn_group(*att_in[3 * gi:3 * gi + 3], window)
        atts.append(o_g)
        lses.append(lse_g)

    rw = jnp.pad(router_w[0], ((0, 0), (0, LANES - N_EXPERTS)))
    rw_hi = rw.astype(BF16)
    rw = jnp.concatenate([rw_hi, rw_hi, (rw - rw_hi.astype(F32)).astype(BF16)], axis=0)
    rb = jnp.concatenate([router_b[0], jnp.full((LANES - N_EXPERTS,), -jnp.inf, F32)]).reshape(1, LANES)
    x1, h2, topi, topw, cnt = _merge(
        o_f.reshape(T, D), o_b.reshape(T, D), rg, gla_out_norm[0].reshape(1, D), atts, lses, ga, gb, x2,
        w_branch_gla[0].astype(BF16), w_branch_attn[0].astype(BF16), w_out[0].astype(BF16),
        norm_ffn[0].reshape(1, D), rw, rb)

    tm_e = 256
    assert T & (T - 1) == 0, "token count must be a power of two (row index is masked out of the slot code)"
    te, dst2d = _route(topi[:, :TOP_K], cnt[0, :N_EXPERTS].astype(I32), tm_e)
    E, _, F = expert_w1[0].shape
    b1p = expert_b1[0].reshape(E, F // (2 * LANES), LANES, 2).transpose(0, 1, 3, 2).reshape(E, 1, F)
    ybuf = _experts(te, dst2d, h2, expert_w1[0], b1p, expert_w2[0], expert_b2[0][:, None, :], tm_e)

    out = _final(x1, ybuf, topw, p[0].reshape(T, -1), norm_ple[0].reshape(1, D), ple_gate_w[0].astype(BF16),
                 ple_proj[0].astype(BF16), norm_final.reshape(1, D))
    return out.reshape(B, S, D)
```

```python
import functools

import jax
import jax.numpy as jnp
from jax import lax
from jax.experimental import pallas as pl
from jax.experimental.pallas import tpu as pltpu

F32 = jnp.float32
BF16 = jnp.bfloat16
I32 = jnp.int32

EPS = 1e-6
GLA_HEADS = 4
GLA_DK = 128
GLA_DV = 256
GLA_RANK = 16
GLA_TAU = 16.0
GLA_CHUNK = 64
ATT_GROUPS = ((128, 1), (512, 4), (2048, 16))
ATT_HPG = 4
ATT_HD = 64
ATT_GW = ATT_HPG * ATT_HD
ROT_DIM = 16
ROPE_THETA = 500000.0
N_EXPERTS = 32
TOP_K = 4
SWIGLU_ALPHA = 1.702
SWIGLU_LIMIT = 7.0

LANES = 128
MIB = 1024 * 1024

SEG_WIDTHS = (512, 512, 1024, 1024, LANES, 768, 768, 768, 1024, 1024)
SEG_OFFS = tuple(sum(SEG_WIDTHS[:i]) for i in range(len(SEG_WIDTHS) + 1))


def _cparams(sem, vmem_mib):
    return pltpu.CompilerParams(dimension_semantics=sem, vmem_limit_bytes=vmem_mib * MIB)


def _inproj_kernel(x_ref, pos_ref, g_ref, tab_ref, w_ref, *refs):
    (qg_ref, kg_ref, vg_ref, rg_ref, al_ref, ga_ref, gb_ref), att_refs, ysc = refs[:7], refs[7:16], refs[16]
    x = x_ref[...]
    h = (x * lax.rsqrt(jnp.mean(x * x, axis=-1, keepdims=True) + EPS) * g_ref[...]).astype(BF16)

    def proj(seg):
        return jnp.dot(h, w_ref[:, SEG_OFFS[seg]:SEG_OFFS[seg + 1]], preferred_element_type=F32)

    qg_ref[...] = proj(0).astype(BF16)
    kg_ref[...] = proj(1).astype(BF16)
    vg_ref[...] = proj(2).astype(BF16)
    rg_ref[...] = proj(3).astype(BF16)
    al_ref[...] = proj(4).astype(BF16)
    ga_ref[...] = proj(8).astype(BF16)
    gb_ref[...] = proj(9).astype(BF16)

    ang = pos_ref[...].astype(F32) * tab_ref[0:1, :]
    cs = jnp.cos(ang)
    sn = jnp.sin(ang)
    lane = lax.broadcasted_iota(I32, (1, LANES), 1) % ATT_HD
    c_mul = jnp.where(lane < ROT_DIM, cs, 1.0)
    s_next = jnp.where(lane < ROT_DIM // 2, -sn, 0.0)
    s_prev = jnp.where((lane >= ROT_DIM // 2) & (lane < ROT_DIM), sn, 0.0)

    tm = ysc.shape[1]
    tiles_per_group = ATT_GW // LANES

    def emit(seg, which, rotary):
        y = proj(seg)
        for j in range(SEG_WIDTHS[seg] // LANES):
            t = y[:, j * LANES:(j + 1) * LANES]
            if rotary:
                t = (t * c_mul + pltpu.roll(t, LANES - ROT_DIM // 2, axis=1) * s_next
                     + pltpu.roll(t, ROT_DIM // 2, axis=1) * s_prev)
            ysc[j] = t
        for gi, (_, d) in enumerate(ATT_GROUPS):
            o_ref = att_refs[3 * gi + which]
            for r in range(d):
                for j in range(tiles_per_group):
                    o_ref[r, :, j * LANES:(j + 1) * LANES] = ysc[
                        gi * tiles_per_group + j, pl.ds(r, tm // d, stride=d), :].astype(BF16)

    emit(5, 0, True)
    emit(6, 1, True)
    emit(7, 2, False)


def _inproj(x2, pos2, gain, tab, w_cat, B, tm=512):
    T, D = x2.shape
    S = T // B
    nb = S // tm
    row_widths = [SEG_WIDTHS[s] for s in (0, 1, 2, 3, 4, 8, 9)]
    outs = [jax.ShapeDtypeStruct((T, w), BF16) for w in row_widths]
    out_specs = [pl.BlockSpec((tm, w), lambda i: (i, 0)) for w in row_widths]
    for _, d in ATT_GROUPS:
        for _ in range(3):
            outs.append(jax.ShapeDtypeStruct((B, d, S // d, ATT_GW), BF16))
            out_specs.append(pl.BlockSpec((None, d, tm // d, ATT_GW), lambda i: (i // nb, 0, i % nb, 0)))
    return pl.pallas_call(
        _inproj_kernel,
        grid=(T // tm,),
        in_specs=[
            pl.BlockSpec((tm, D), lambda i: (i, 0)),
            pl.BlockSpec((tm, 1), lambda i: (i, 0)),
            pl.BlockSpec((1, D), lambda i: (0, 0)),
            pl.BlockSpec((8, LANES), lambda i: (0, 0)),
            pl.BlockSpec((D, SEG_OFFS[-1]), lambda i: (0, 0), pipeline_mode=pl.Buffered(1)),
        ],
        out_specs=out_specs,
        out_shape=outs,
        scratch_shapes=[pltpu.VMEM((SEG_WIDTHS[5] // LANES, tm, LANES), F32)],
        compiler_params=_cparams(("arbitrary",), 56),
        name="inproj",
    )(x2, pos2, gain, tab, w_cat)


def _split3(a):
    a1 = a.astype(BF16)
    r1 = a - a1.astype(F32)
    a2 = r1.astype(BF16)
    a3 = (r1 - a2.astype(F32)).astype(BF16)
    return a1, a2, a3


GLA_GROUP = 4


def _gla_group(q_ref, k_ref, v_ref, o_ref, g_ref, st_ref, r0, backward):
    C, G = GLA_CHUNK, GLA_GROUP
    R = C * G
    HK = GLA_HEADS * GLA_DK
    ri = lax.broadcasted_iota(I32, (R, R), 0)
    ci = lax.broadcasted_iota(I32, (R, R), 1)
    same = (ri // C) == (ci // C)
    if backward:
        tri = (same & (ci >= ri)).astype(BF16)
        keep = same & (ci > ri)
        ref_off, last_off = C // 2, 0
        order = list(range(G - 1, -1, -1))
    else:
        tri = (same & (ci <= ri)).astype(BF16)
        keep = same & (ci <= ri)
        ref_off, last_off = C // 2 - 1, C - 1
        order = list(range(G))
    pos_of = {c: p for p, c in enumerate(order)}
    rows = pl.ds(r0, R)

    def per_chunk(vals):
        return jnp.concatenate([jnp.broadcast_to(v, (C, HK)) for v in vals], axis=0)

    g1, g2, g3 = _split3(g_ref[rows, :])
    b = (jnp.dot(tri, g1, preferred_element_type=F32) + jnp.dot(tri, g2, preferred_element_type=F32)
         + jnp.dot(tri, g3, preferred_element_type=F32))
    b_last = [b[c * C + last_off:c * C + last_off + 1] for c in range(G)]
    b_mid = [b[c * C + ref_off:c * C + ref_off + 1] for c in range(G)]
    E = [b_last[order[0]]]
    for p in range(1, G):
        E.append(E[-1] + b_last[order[p]])
    zero = jnp.zeros((1, HK), F32)
    one = jnp.ones((1, HK), F32)

    q = q_ref[rows, :].astype(F32) * (GLA_DK ** -0.5)
    k = k_ref[rows, :].astype(F32)
    bm = per_chunk(b_mid)
    qa = (q * jnp.exp(b - bm)).astype(BF16)
    ka = (k * jnp.exp(bm - b)).astype(BF16)
    qi = q * jnp.exp(b)
    q_in = (qi * per_chunk([jnp.exp(E[pos_of[c] - 1]) if pos_of[c] > 0 else one for c in range(G)])).astype(BF16)
    q_x = [(qi * per_chunk([jnp.exp(E[pos_of[c] - 1] - E[pp]) if pos_of[c] > pp else zero for c in range(G)])
            ).astype(BF16) for pp in range(G - 1)]
    ks = k * jnp.exp(per_chunk(b_last) - b)
    scale_rows = [jnp.exp(E[G - 1] - E[p]) for p in range(G)] + [jnp.exp(E[G - 1])] + [zero] * (8 - G - 1)
    scale_cols = jnp.concatenate(scale_rows, axis=0).T
    lane_chunk = lax.broadcasted_iota(I32, (1, R), 1) // C

    for h in range(GLA_HEADS):
        hk = slice(h * GLA_DK, (h + 1) * GLA_DK)
        hv = slice(h * GLA_DV, (h + 1) * GLA_DV)
        v = v_ref[rows, hv]
        state = st_ref[h]
        s = lax.dot_general(qa[:, hk], ka[:, hk], (((1,), (1,)), ((), ())), preferred_element_type=F32)
        s = jnp.where(keep, s, 0.0).astype(BF16)
        ks_t = ks[:, hk].T
        cols = scale_cols[hk]
        to_end = jnp.zeros((GLA_DK, R), F32)
        for p in range(G):
            to_end = jnp.where(lane_chunk == order[p], cols[:, p:p + 1], to_end)
        lhs_kv = jnp.concatenate([jnp.where(lane_chunk == order[p], ks_t, 0.0) for p in range(G - 1)]
                                 + [ks_t * to_end], axis=0).astype(BF16)
        kv = jnp.dot(lhs_kv, v, preferred_element_type=F32)
        n_x = (G - 1) * GLA_DK
        o = jnp.dot(jnp.concatenate([s, q_in[:, hk]] + [x[:, hk] for x in q_x], axis=-1),
                    jnp.concatenate([v, state.astype(BF16), kv[:n_x].astype(BF16)], axis=0),
                    preferred_element_type=F32)
        o_ref[rows, hv] = o.astype(BF16)
        st_ref[h] = cols[:, G:G + 1] * state + kv[n_x:]


def _gla_kernel(qf_ref, kf_ref, vf_ref, af_ref, qb_ref, kb_ref, vb_ref, ab_ref, u_ref, bias_ref,
                of_ref, ob_ref, st_ref, g_scr, *, n_chunks):
    @pl.when(pl.program_id(1) == 0)
    def _():
        st_ref[...] = jnp.zeros_like(st_ref)

    for d, a_ref in enumerate((af_ref, ab_ref)):
        z = jnp.dot(a_ref[...], u_ref[d], preferred_element_type=F32) + bias_ref[d]
        g_scr[d] = (jnp.minimum(z, 0.0) - jnp.log(1.0 + jnp.exp(-jnp.abs(z)))) * (1.0 / GLA_TAU)

    n_groups = n_chunks // GLA_GROUP
    group_rows = GLA_GROUP * GLA_CHUNK

    def body(it, carry):
        rf = pl.multiple_of(it * group_rows, group_rows)
        rb = pl.multiple_of((n_groups - 1 - it) * group_rows, group_rows)
        _gla_group(qf_ref, kf_ref, vf_ref, of_ref, g_scr.at[0], st_ref.at[0], rf, False)
        _gla_group(qb_ref, kb_ref, vb_ref, ob_ref, g_scr.at[1], st_ref.at[1], rb, True)
        return carry

    lax.fori_loop(0, n_groups, body, 0)


def _gla(qg, kg, vg, al, u_pad, bias, n_chunks=16):
    B, S, _ = qg.shape
    rows = n_chunks * GLA_CHUNK
    NB = S // rows
    fwd = lambda b, n: (b, n, 0)
    bwd = lambda b, n: (b, NB - 1 - n, 0)
    specs = lambda im: [pl.BlockSpec((None, rows, t.shape[-1]), im) for t in (qg, kg, vg, al)]
    out = jax.ShapeDtypeStruct((B, S, GLA_HEADS * GLA_DV), BF16)
    return pl.pallas_call(
        functools.partial(_gla_kernel, n_chunks=n_chunks),
        grid=(B, NB),
        in_specs=specs(fwd) + specs(bwd) + [
            pl.BlockSpec(u_pad.shape, lambda b, n: (0, 0, 0)),
            pl.BlockSpec(bias.shape, lambda b, n: (0, 0, 0)),
        ],
        out_specs=[pl.BlockSpec((None, rows, GLA_HEADS * GLA_DV), fwd),
                   pl.BlockSpec((None, rows, GLA_HEADS * GLA_DV), bwd)],
        out_shape=[out, out],
        scratch_shapes=[pltpu.VMEM((2, GLA_HEADS, GLA_DK, GLA_DV), F32),
                        pltpu.VMEM((2, rows, GLA_HEADS * GLA_DK), F32)],
        compiler_params=_cparams(("arbitrary", "arbitrary"), 48),
        name="gla",
    )(qg, kg, vg, al, qg, kg, vg, al, u_pad, bias)


def _attn_kernel(q_ref, k_ref, v_ref, o_ref, lse_ref, *, sub, win, half):
    L = k_ref.shape[0]
    lq = q_ref.shape[0]
    base = pl.program_id(2) * lq

    def body(t, carry):
        r0 = pl.multiple_of(t * sub, sub)
        rows = pl.ds(r0, sub)
        q0 = base + r0
        k0 = pl.multiple_of(jnp.clip(q0 - half, 0, L - win), half)
        qpos = q0 + lax.broadcasted_iota(I32, (sub, 1), 0)
        kpos = k0 + lax.broadcasted_iota(I32, (1, win), 1)
        valid = jnp.abs(qpos - kpos) <= half
        q = q_ref[rows, :]
        k = k_ref[pl.ds(k0, win), :]
        v = v_ref[pl.ds(k0, win), :]
        for h in range(ATT_HPG):
            hs = slice(h * ATT_HD, (h + 1) * ATT_HD)
            s = lax.dot_general(q[:, hs], k[:, hs], (((1,), (1,)), ((), ())), preferred_element_type=F32)
            s = jnp.where(valid, s, -jnp.inf)
            m = jnp.max(s, axis=-1, keepdims=True)
            pr = jnp.exp(s - m)
            l = jnp.sum(pr, axis=-1, keepdims=True)
            o = jnp.dot(pr.astype(BF16), v[:, hs], preferred_element_type=F32) / l
            o_ref[rows, hs] = o.astype(BF16)
            lse_ref[rows, hs] = jnp.broadcast_to(m + jnp.log(l), (sub, ATT_HD))
        return carry

    lax.fori_loop(0, lq // sub, body, 0)


def _attn_group(q, k, v, window, lq=1024, sub=128):
    B, d, L, _ = q.shape
    half = window // (2 * d)
    lq = min(lq, L)
    sub = min(sub, lq)
    win = min(sub + 2 * half, L)
    qmap = lambda b, r, i: (b, r, i, 0)
    kmap = lambda b, r, i: (b, r, 0, 0)
    return pl.pallas_call(
        functools.partial(_attn_kernel, sub=sub, win=win, half=half),
        grid=(B, d, L // lq),
        in_specs=[pl.BlockSpec((None, None, lq, ATT_GW), qmap),
                  pl.BlockSpec((None, None, L, ATT_GW), kmap),
                  pl.BlockSpec((None, None, L, ATT_GW), kmap)],
        out_specs=[pl.BlockSpec((None, None, lq, ATT_GW), qmap), pl.BlockSpec((None, None, lq, ATT_GW), qmap)],
        out_shape=[jax.ShapeDtypeStruct(q.shape, BF16), jax.ShapeDtypeStruct(q.shape, F32)],
        compiler_params=_cparams(("arbitrary", "arbitrary", "arbitrary"), 40),
        name=f"attn_d{d}",
    )(q, k, v)


def _sigmoid(t):
    return 1.0 / (1.0 + jnp.exp(-t))


def _merge_kernel(of_ref, ob_ref, rg_ref, gn_ref, a0_ref, a1_ref, a2_ref, l0_ref, l1_ref, l2_ref,
                  ga_ref, gb_ref, x_ref, wa_ref, wb_ref, wo_ref, nf_ref, rw_ref, rb_ref,
                  x1_ref, h2_ref, ti_ref, tw_ref, cnt_ref, *scratch):
    def token_rows(blk_ref, scr):
        d, n, w = blk_ref.shape
        if d == 1:
            return blk_ref[0].astype(F32)
        for r in range(d):
            for j in range(w // LANES):
                scr[j, pl.ds(r, n, stride=d), :] = blk_ref[r, :, j * LANES:(j + 1) * LANES].astype(F32)
        return jnp.concatenate([scr[j] for j in range(w // LANES)], axis=-1)

    o = of_ref[...].astype(F32) + ob_ref[...].astype(F32)
    parts = []
    for h in range(GLA_HEADS):
        oh = o[:, h * GLA_DV:(h + 1) * GLA_DV]
        parts.append(oh * lax.rsqrt(jnp.mean(oh * oh, axis=-1, keepdims=True) + EPS))
    r = rg_ref[...].astype(F32)
    y_gla = jnp.concatenate(parts, axis=-1) * gn_ref[...] * (r * _sigmoid(r))

    l0, l1, l2 = (token_rows(l_ref, scr) for l_ref, scr in zip((l0_ref, l1_ref, l2_ref), scratch[0:3]))
    a0, a1, a2 = (token_rows(a_ref, scr) for a_ref, scr in zip((a0_ref, a1_ref, a2_ref), scratch[3:6]))
    m = jnp.maximum(jnp.maximum(l0, l1), l2)
    e0, e1, e2 = jnp.exp(l0 - m), jnp.exp(l1 - m), jnp.exp(l2 - m)
    y_att = (e0 * a0 + e1 * a1 + e2 * a2) / (e0 + e1 + e2)

    t_gla = jnp.dot(y_gla.astype(BF16), wa_ref[...], preferred_element_type=F32)
    t_att = jnp.dot(y_att.astype(BF16), wb_ref[...], preferred_element_type=F32)
    merged = _sigmoid(ga_ref[...].astype(F32)) * t_gla + _sigmoid(gb_ref[...].astype(F32)) * t_att
    x1 = x_ref[...] + jnp.dot(merged.astype(BF16), wo_ref[...], preferred_element_type=F32)
    x1_ref[...] = x1
    h2 = x1 * lax.rsqrt(jnp.mean(x1 * x1, axis=-1, keepdims=True) + EPS) * nf_ref[...]
    _to_slabs(h2_ref, h2)

    h_hi = h2.astype(BF16)
    h_lo = (h2 - h_hi.astype(F32)).astype(BF16)
    logits = jnp.dot(jnp.concatenate([h_hi, h_lo, h_hi], axis=-1), rw_ref[...],
                     preferred_element_type=F32) + rb_ref[...]
    lane = lax.broadcasted_iota(I32, logits.shape, 1).astype(F32)
    vals, idxs = [], []
    for _ in range(TOP_K):
        mx = jnp.max(logits, axis=-1, keepdims=True)
        ix = jnp.min(jnp.where(logits == mx, lane, float(LANES)), axis=-1, keepdims=True)
        vals.append(mx)
        idxs.append(ix)
        logits = jnp.where(lane == ix, -jnp.inf, logits)
    es = [jnp.exp(vk - vals[0]) for vk in vals]
    den = es[0] + es[1] + es[2] + es[3]
    ti = jnp.zeros(lane.shape, F32)
    tw = jnp.zeros(lane.shape, F32)
    for kk in range(TOP_K):
        ti = jnp.where(lane == float(kk), idxs[kk], ti)
        tw = jnp.where(lane == float(kk), es[kk] / den, tw)
    ti_ref[...] = ti.astype(I32)
    tw_ref[...] = tw

    @pl.when(pl.program_id(0) == 0)
    def _():
        cnt_ref[...] = jnp.zeros_like(cnt_ref)

    hits = sum((lane == ix).astype(F32) for ix in idxs)
    cnt_ref[...] += jnp.sum(hits, axis=0, keepdims=True)


def _merge(o_f, o_b, rg, gn, atts, lses, ga, gb, x2, wa, wb, wo, nf, rw, rb, tm=256):
    T, D = x2.shape
    B = atts[0].shape[0]
    nb = T // B // tm
    row = lambda w: pl.BlockSpec((tm, w), lambda i: (i, 0))
    const = lambda a: pl.BlockSpec(a.shape, lambda i: (0, 0))
    res = lambda a: pl.BlockSpec((None, a.shape[1], tm // a.shape[1], ATT_GW), lambda i: (i // nb, 0, i % nb, 0))
    return pl.pallas_call(
        _merge_kernel,
        grid=(T // tm,),
        in_specs=[row(D), row(D), row(D), const(gn)] + [res(a) for a in atts] + [res(a) for a in lses]
                 + [row(D), row(D), row(D), const(wa), const(wb), const(wo), const(nf), const(rw), const(rb)],
        out_specs=[row(D), pl.BlockSpec((tm * SLAB, LANES), lambda i: (i, 0)), row(LANES), row(LANES),
                   pl.BlockSpec((1, LANES), lambda i: (0, 0))],
        out_shape=[jax.ShapeDtypeStruct((T, D), F32), jax.ShapeDtypeStruct((T * SLAB, LANES), F32),
                   jax.ShapeDtypeStruct((T, LANES), I32), jax.ShapeDtypeStruct((T, LANES), F32),
                   jax.ShapeDtypeStruct((1, LANES), F32)],
        scratch_shapes=[pltpu.VMEM((ATT_GW // LANES, tm, LANES), F32)] * 6,
        compiler_params=_cparams(("arbitrary",), 48),
        name="merge_router",
    )(o_f, o_b, rg, gn, *atts, *lses, ga, gb, x2, wa, wb, wo, nf, rw, rb)


SLAB = 8


def _to_slabs(ref, val):
    rows = val.shape[0]
    for s in range(SLAB):
        ref[pl.ds(s, rows, stride=SLAB), :] = val[:, s * LANES:(s + 1) * LANES]


def _from_slabs(ref, rows):
    return jnp.concatenate([ref[pl.ds(s, rows, stride=SLAB), :] for s in range(SLAB)], axis=-1)


K_PHASES = 4
N_XBUF = 3


def _experts_kernel(te_ref, dst_ref, h_hbm, w1_ref, b1_ref, w2_ref, b2_ref,
                    y_hbm, buf, w1p, w2b, sem_g, sem_s, *, tm, n_tok):
    i = pl.program_id(0)
    last = pl.num_programs(0) - 1
    slot = i % 2
    xslot = i % N_XBUF
    D = w2b.shape[0]

    def gather_start(row, s, r0, r1):
        for r in range(r0, r1):
            tok = dst_ref[row, r] & (n_tok - 1)
            pltpu.make_async_copy(h_hbm.at[pl.ds(pl.multiple_of(tok * SLAB, SLAB), SLAB), :],
                                  buf.at[s, pl.ds(r * SLAB, SLAB), :], sem_g.at[s]).start()

    def gather_wait(s):
        pltpu.make_async_copy(h_hbm.at[pl.ds(0, tm * SLAB), :], buf.at[s], sem_g.at[s]).wait()

    def scatter_start(row, s, r0, r1):
        for r in range(r0, r1):
            d = dst_ref[row, r]
            pltpu.make_async_copy(buf.at[N_XBUF + s, pl.ds(r * SLAB, SLAB), :],
                                  y_hbm.at[pl.ds(pl.multiple_of(d * SLAB, SLAB), SLAB), :], sem_s.at[s]).start()

    def scatter_wait(s):
        pltpu.make_async_copy(buf.at[N_XBUF + s], y_hbm.at[pl.ds(0, tm * SLAB), :], sem_s.at[s]).wait()

    @pl.when(i == 0)
    def _():
        buf[N_XBUF + 1] = jnp.zeros(buf.shape[1:], F32)
        gather_start(1, 0, 0, tm)
        gather_start(jnp.minimum(2, last), 1, 0, tm)

    @pl.when((i == 0) | (te_ref[i] != te_ref[jnp.maximum(i - 1, 0)]))
    def _():
        kk = lax.broadcasted_iota(I32, (2 * LANES, 2 * LANES), 0)
        nn = lax.broadcasted_iota(I32, (2 * LANES, 2 * LANES), 1)
        perm = (kk == jnp.where(nn < LANES, 2 * nn, 2 * (nn - LANES) + 1)).astype(BF16)
        for c in range(w1p.shape[1] // (2 * LANES)):
            cs = slice(c * 2 * LANES, (c + 1) * 2 * LANES)
            w1p[:, cs] = jnp.dot(w1_ref[:, cs].astype(BF16), perm, preferred_element_type=F32).astype(BF16)
        w2b[...] = w2_ref[...].astype(BF16)

    gather_wait(xslot)
    nxt = jnp.minimum(i + 3, last)
    nxt_slot = (i + 2) % N_XBUF
    rows_per = tm // K_PHASES
    slabs_per = SLAB // K_PHASES
    hid = b1_ref[...]
    for j in range(K_PHASES):
        gather_start(nxt, nxt_slot, j * rows_per, (j + 1) * rows_per)
        scatter_start(i, 1 - slot, j * rows_per, (j + 1) * rows_per)
        xj = jnp.concatenate([buf[xslot, pl.ds(s, tm, stride=SLAB), :]
                              for s in range(j * slabs_per, (j + 1) * slabs_per)], axis=-1).astype(BF16)
        kw = slabs_per * LANES
        hid = hid + jnp.dot(xj, w1p[j * kw:(j + 1) * kw, :], preferred_element_type=F32)
    acts = []
    for c in range(D // LANES):
        gate = jnp.minimum(hid[:, 2 * c * LANES:(2 * c + 1) * LANES], SWIGLU_LIMIT)
        up = jnp.clip(hid[:, (2 * c + 1) * LANES:(2 * c + 2) * LANES], -SWIGLU_LIMIT, SWIGLU_LIMIT)
        acts.append(((up + 1.0) * (gate * _sigmoid(gate * SWIGLU_ALPHA))).astype(BF16))
    act = jnp.concatenate(acts, axis=-1)
    out = jnp.dot(act, w2b[...], preferred_element_type=F32) + b2_ref[...]

    @pl.when(i > 0)
    def _():
        scatter_wait(slot)

    _to_slabs(buf.at[N_XBUF + slot], out)

    @pl.when(i == last)
    def _():
        scatter_wait(1 - slot)
        gather_wait((i + 1) % N_XBUF)
        gather_wait((i + 2) % N_XBUF)


def _experts(te_ext, dst_ext, h2s, w1, b1p, w2, b2, tm):
    T = h2s.shape[0] // SLAB
    n_steps = dst_ext.shape[0]
    _, D, F = w1.shape
    by_expert = lambda r, c: pl.BlockSpec((None, r, c), lambda i, te, dst: (te[i], 0, 0))
    return pl.pallas_call(
        functools.partial(_experts_kernel, tm=tm, n_tok=T),
        grid_spec=pltpu.PrefetchScalarGridSpec(
            num_scalar_prefetch=2,
            grid=(n_steps,),
            in_specs=[pl.BlockSpec(memory_space=pl.ANY), by_expert(D, F), by_expert(1, F),
                      by_expert(F // 2, D), by_expert(1, D)],
            out_specs=pl.BlockSpec(memory_space=pl.ANY),
            scratch_shapes=[pltpu.VMEM((N_XBUF + 2, tm * SLAB, LANES), F32),
                            pltpu.VMEM((D, F), BF16), pltpu.VMEM((F // 2, D), BF16),
                            pltpu.SemaphoreType.DMA((N_XBUF,)), pltpu.SemaphoreType.DMA((2,))],
        ),
        out_shape=jax.ShapeDtypeStruct((n_steps * tm * SLAB, LANES), F32),
        compiler_params=pltpu.CompilerParams(dimension_semantics=("arbitrary",), vmem_limit_bytes=56 * MIB,
                                             disable_bounds_checks=True),
        name="experts",
    )(te_ext, dst_ext, h2s, w1, b1p, w2, b2)


def _route(topi, counts, tm):
    T = topi.shape[0]
    E = N_EXPERTS
    n_asg = T * TOP_K
    n_tiles = n_asg // tm + E
    e_flat = topi.reshape(n_asg)
    _, order = lax.sort((e_flat, jnp.arange(n_asg, dtype=I32)), num_keys=1, is_stable=True)
    cend = jnp.cumsum(counts)
    cstart = cend - counts
    padded = (counts + tm - 1) // tm * tm
    pend = jnp.cumsum(padded)
    pstart = pend - padded
    n_used = pend[-1] // tm
    tstart = jnp.arange(n_tiles, dtype=I32) * tm
    te_raw = jnp.sum(tstart[:, None] >= pend[None, :], axis=1, dtype=I32)
    last_e = jnp.sum((n_used - 1) * tm >= pend, dtype=I32)
    te = jnp.where(te_raw < E, te_raw, last_e)
    esc = jnp.minimum(te_raw, E - 1)
    j0 = tstart - pstart[esc]
    n_valid = jnp.where(te_raw < E, jnp.clip(counts[esc] - j0, 0, tm), 0)
    col = jnp.arange(tm, dtype=I32)[None, :]
    a = order[jnp.clip((cstart[esc] + j0)[:, None] + col, 0, n_asg - 1)]
    dst_valid = (a % TOP_K) * T + a // TOP_K
    cend_ext = jnp.concatenate([cend, jnp.full((1,), n_asg, I32)])
    dst_pad = (n_asg + tstart - cend_ext[te_raw])[:, None] + col
    dst = jnp.where(col < n_valid[:, None], dst_valid, dst_pad).astype(I32)
    dump = n_tiles * tm + jnp.arange(tm, dtype=I32)
    dst_ext = jnp.concatenate([dump[None, :], dst], axis=0)
    te_ext = jnp.concatenate([te, te[-1:]])
    return te_ext, dst_ext


def _final_kernel(x1_ref, y0_ref, y1_ref, y2_ref, y3_ref, tw_ref, p_ref, np_ref, wg_ref, wp_ref, nfin_ref, o_ref):
    tw = tw_ref[...]
    x2 = x1_ref[...]
    for kk, y_ref in enumerate((y0_ref, y1_ref, y2_ref, y3_ref)):
        x2 = x2 + tw[:, kk:kk + 1] * _from_slabs(y_ref, x2.shape[0])
    h3 = (x2 * lax.rsqrt(jnp.mean(x2 * x2, axis=-1, keepdims=True) + EPS) * np_ref[...]).astype(BF16)
    gate = _sigmoid(jnp.dot(h3, wg_ref[...], preferred_element_type=F32))
    pe = jnp.dot(p_ref[...].astype(BF16), wp_ref[...], preferred_element_type=F32)
    x3 = x2 + gate * pe
    o_ref[...] = x3 * lax.rsqrt(jnp.mean(x3 * x3, axis=-1, keepdims=True) + EPS) * nfin_ref[...]


def _final(x1, ybuf, tw, p2, n_ple, wg, wp, n_fin, tm=256):
    T, D = x1.shape
    nb = T // tm
    row = lambda w: pl.BlockSpec((tm, w), lambda i: (i, 0))
    const = lambda a: pl.BlockSpec(a.shape, lambda i: (0, 0))
    yspec = lambda kk: pl.BlockSpec((tm * SLAB, LANES), lambda i: (kk * nb + i, 0))
    return pl.pallas_call(
        _final_kernel,
        grid=(nb,),
        in_specs=[row(D)] + [yspec(kk) for kk in range(TOP_K)] + [row(LANES), row(p2.shape[1]),
                  const(n_ple), const(wg), const(wp), const(n_fin)],
        out_specs=row(D),
        out_shape=jax.ShapeDtypeStruct((T, D), F32),
        compiler_params=_cparams(("arbitrary",), 48),
        name="final",
    )(x1, ybuf, ybuf, ybuf, ybuf, tw, p2, n_ple, wg, wp, n_fin)


def kernel(x, p, positions, norm_mix, w_in, gla_fgate_up, gla_fgate_bias, gla_out_norm, w_branch_gla, w_branch_attn, w_out, norm_ffn, router_w, router_b, expert_w1, expert_b1, expert_w2, expert_b2, norm_ple, ple_gate_w, ple_proj, norm_final):
    B, S, D = x.shape
    T = B * S
    assert w_in.shape[0] == 1, "single-layer block: the final norm is fused into the layer's last kernel"
    x2 = x.reshape(T, D)
    pos2 = positions.reshape(T, 1)

    w = w_in[0]
    w_cat = jnp.concatenate([
        w[:, :3072],
        jnp.pad(w[:, 3072:3104], ((0, 0), (0, LANES - 2 * GLA_RANK))),
        w[:, 3104:3104 + 768] * (ATT_HD ** -0.5),
        w[:, 3104 + 768:],
    ], axis=1).astype(BF16)
    inv_freq = ROPE_THETA ** (-jnp.arange(0, ROT_DIM, 2, dtype=F32) / ROT_DIM)
    lane_f = (jnp.arange(LANES) % ATT_HD) % (ROT_DIM // 2)
    tab = jnp.zeros((8, LANES), F32).at[0].set(inv_freq[lane_f])
    u_pad = jnp.zeros((2, LANES, GLA_HEADS * GLA_DK), F32)
    u_pad = u_pad.at[0, :GLA_RANK].set(gla_fgate_up[0, 0]).at[1, GLA_RANK:2 * GLA_RANK].set(gla_fgate_up[0, 1])
    u_pad = u_pad.astype(BF16)
    bias = gla_fgate_bias[0].reshape(2, 1, GLA_HEADS * GLA_DK)

    qg, kg, vg, rg, al, ga, gb, *att_in = _inproj(x2, pos2, norm_mix[0].reshape(1, D), tab, w_cat, B)
    r3 = lambda t: t.reshape(B, S, t.shape[-1])
    o_f, o_b = _gla(r3(qg), r3(kg), r3(vg), r3(al), u_pad, bias)
    atts, lses = [], []
    for gi, (window, _) in enumerate(ATT_GROUPS):
        o_g, lse_g = _attn_group(*att_in[3 * gi:3 * gi + 3], window)
        atts.append(o_g)
        lses.append(lse_g)

    rw = jnp.pad(router_w[0], ((0, 0), (0, LANES - N_EXPERTS)))
    rw_hi = rw.astype(BF16)
    rw = jnp.concatenate([rw_hi, rw_hi, (rw - rw_hi.astype(F32)).astype(BF16)], axis=0)
    rb = jnp.concatenate([router_b[0], jnp.full((LANES - N_EXPERTS,), -jnp.inf, F32)]).reshape(1, LANES)
    x1, h2, topi, topw, cnt = _merge(
        o_f.reshape(T, D), o_b.reshape(T, D), rg, gla_out_norm[0].reshape(1, D), atts, lses, ga, gb, x2,
        w_branch_gla[0].astype(BF16), w_branch_attn[0].astype(BF16), w_out[0].astype(BF16),
        norm_ffn[0].reshape(1, D), rw, rb)

    tm_e = 256
    assert T & (T - 1) == 0, "token count must be a power of two (row index is masked out of the slot code)"
    te, dst2d = _route(topi[:, :TOP_K], cnt[0, :N_EXPERTS].astype(I32), tm_e)
    E, _, F = expert_w1[0].shape
    b1p = expert_b1[0].reshape(E, F // (2 * LANES), LANES, 2).transpose(0, 1, 3, 2).reshape(E, 1, F)
    ybuf = _experts(te, dst2d, h2, expert_w1[0], b1p, expert_w2[0], expert_b2[0][:, None, :], tm_e)

    out = _final(x1, ybuf, topw, p[0].reshape(T, -1), norm_ple[0].reshape(1, D), ple_gate_w[0].astype(BF16),
                 ple_proj[0].astype(BF16), norm_final.reshape(1, D))
    return out.reshape(B, S, D)
```

```python
import functools

import jax
import jax.numpy as jnp
from jax import lax
from jax.experimental import pallas as pl
from jax.experimental.pallas import tpu as pltpu

F32 = jnp.float32
BF16 = jnp.bfloat16
I32 = jnp.int32

EPS = 1e-6
GLA_HEADS = 4
GLA_DK = 128
GLA_DV = 256
GLA_RANK = 16
GLA_TAU = 16.0
GLA_CHUNK = 64
ATT_GROUPS = ((128, 1), (512, 4), (2048, 16))
ATT_HPG = 4
ATT_HD = 64
ATT_GW = ATT_HPG * ATT_HD
ROT_DIM = 16
ROPE_THETA = 500000.0
N_EXPERTS = 32
TOP_K = 4
SWIGLU_ALPHA = 1.702
SWIGLU_LIMIT = 7.0

LANES = 128
MIB = 1024 * 1024

SEG_WIDTHS = (512, 512, 1024, 1024, LANES, 768, 768, 768, 1024, 1024)
SEG_OFFS = tuple(sum(SEG_WIDTHS[:i]) for i in range(len(SEG_WIDTHS) + 1))


def _cparams(sem, vmem_mib):
    return pltpu.CompilerParams(dimension_semantics=sem, vmem_limit_bytes=vmem_mib * MIB)


def _inproj_kernel(x_ref, pos_ref, g_ref, tab_ref, w_ref, *refs):
    (qg_ref, kg_ref, vg_ref, rg_ref, al_ref, ga_ref, gb_ref), att_refs, ysc = refs[:7], refs[7:16], refs[16]
    x = x_ref[...]
    h = (x * lax.rsqrt(jnp.mean(x * x, axis=-1, keepdims=True) + EPS) * g_ref[...]).astype(BF16)

    def proj(seg):
        return jnp.dot(h, w_ref[:, SEG_OFFS[seg]:SEG_OFFS[seg + 1]], preferred_element_type=F32)

    qg_ref[...] = proj(0).astype(BF16)
    kg_ref[...] = proj(1).astype(BF16)
    vg_ref[...] = proj(2).astype(BF16)
    rg_ref[...] = proj(3).astype(BF16)
    al_ref[...] = proj(4).astype(BF16)
    ga_ref[...] = proj(8).astype(BF16)
    gb_ref[...] = proj(9).astype(BF16)

    ang = pos_ref[...].astype(F32) * tab_ref[0:1, :]
    cs = jnp.cos(ang)
    sn = jnp.sin(ang)
    lane = lax.broadcasted_iota(I32, (1, LANES), 1) % ATT_HD
    c_mul = jnp.where(lane < ROT_DIM, cs, 1.0)
    s_next = jnp.where(lane < ROT_DIM // 2, -sn, 0.0)
    s_prev = jnp.where((lane >= ROT_DIM // 2) & (lane < ROT_DIM), sn, 0.0)

    tm = ysc.shape[1]
    tiles_per_group = ATT_GW // LANES

    def emit(seg, which, rotary):
        y = proj(seg)
        for j in range(SEG_WIDTHS[seg] // LANES):
            t = y[:, j * LANES:(j + 1) * LANES]
            if rotary:
                t = (t * c_mul + pltpu.roll(t, LANES - ROT_DIM // 2, axis=1) * s_next
                     + pltpu.roll(t, ROT_DIM // 2, axis=1) * s_prev)
            ysc[j] = t
        for gi, (_, d) in enumerate(ATT_GROUPS):
            o_ref = att_refs[3 * gi + which]
            for r in range(d):
                for j in range(tiles_per_group):
                    o_ref[r, :, j * LANES:(j + 1) * LANES] = ysc[
                        gi * tiles_per_group + j, pl.ds(r, tm // d, stride=d), :].astype(BF16)

    emit(5, 0, True)
    emit(6, 1, True)
    emit(7, 2, False)


def _inproj(x2, pos2, gain, tab, w_cat, B, tm=512):
    T, D = x2.shape
    S = T // B
    nb = S // tm
    row_widths = [SEG_WIDTHS[s] for s in (0, 1, 2, 3, 4, 8, 9)]
    outs = [jax.ShapeDtypeStruct((T, w), BF16) for w in row_widths]
    out_specs = [pl.BlockSpec((tm, w), lambda i: (i, 0)) for w in row_widths]
    for _, d in ATT_GROUPS:
        for _ in range(3):
            outs.append(jax.ShapeDtypeStruct((B, d, S // d, ATT_GW), BF16))
            out_specs.append(pl.BlockSpec((None, d, tm // d, ATT_GW), lambda i: (i // nb, 0, i % nb, 0)))
    return pl.pallas_call(
        _inproj_kernel,
        grid=(T // tm,),
        in_specs=[
            pl.BlockSpec((tm, D), lambda i: (i, 0)),
            pl.BlockSpec((tm, 1), lambda i: (i, 0)),
            pl.BlockSpec((1, D), lambda i: (0, 0)),
            pl.BlockSpec((8, LANES), lambda i: (0, 0)),
            pl.BlockSpec((D, SEG_OFFS[-1]), lambda i: (0, 0), pipeline_mode=pl.Buffered(1)),
        ],
        out_specs=out_specs,
        out_shape=outs,
        scratch_shapes=[pltpu.VMEM((SEG_WIDTHS[5] // LANES, tm, LANES), F32)],
        compiler_params=_cparams(("arbitrary",), 56),
        name="inproj",
    )(x2, pos2, gain, tab, w_cat)


def _split3(a):
    a1 = a.astype(BF16)
    r1 = a - a1.astype(F32)
    a2 = r1.astype(BF16)
    a3 = (r1 - a2.astype(F32)).astype(BF16)
    return a1, a2, a3


GLA_GROUP = 4


def _gla_group(q_ref, k_ref, v_ref, o_ref, g_ref, st_ref, r0, backward):
    C, G = GLA_CHUNK, GLA_GROUP
    R = C * G
    HK = GLA_HEADS * GLA_DK
    ri = lax.broadcasted_iota(I32, (R, R), 0)
    ci = lax.broadcasted_iota(I32, (R, R), 1)
    same = (ri // C) == (ci // C)
    if backward:
        tri = (same & (ci >= ri)).astype(BF16)
        keep = same & (ci > ri)
        ref_off, last_off = C // 2, 0
        order = list(range(G - 1, -1, -1))
    else:
        tri = (same & (ci <= ri)).astype(BF16)
        keep = same & (ci <= ri)
        ref_off, last_off = C // 2 - 1, C - 1
        order = list(range(G))
    pos_of = {c: p for p, c in enumerate(order)}
    rows = pl.ds(r0, R)

    def per_chunk(vals):
        return jnp.concatenate([jnp.broadcast_to(v, (C, HK)) for v in vals], axis=0)

    g1, g2, g3 = _split3(g_ref[rows, :])
    b = (jnp.dot(tri, g1, preferred_element_type=F32) + jnp.dot(tri, g2, preferred_element_type=F32)
         + jnp.dot(tri, g3, preferred_element_type=F32))
    b_last = [b[c * C + last_off:c * C + last_off + 1] for c in range(G)]
    b_mid = [b[c * C + ref_off:c * C + ref_off + 1] for c in range(G)]
    E = [b_last[order[0]]]
    for p in range(1, G):
        E.append(E[-1] + b_last[order[p]])
    zero = jnp.zeros((1, HK), F32)
    one = jnp.ones((1, HK), F32)

    q = q_ref[rows, :].astype(F32) * (GLA_DK ** -0.5)
    k = k_ref[rows, :].astype(F32)
    bm = per_chunk(b_mid)
    qa = (q * jnp.exp(b - bm)).astype(BF16)
    ka = (k * jnp.exp(bm - b)).astype(BF16)
    qi = q * jnp.exp(b)
    q_in = (qi * per_chunk([jnp.exp(E[pos_of[c] - 1]) if pos_of[c] > 0 else one for c in range(G)])).astype(BF16)
    q_x = [(qi * per_chunk([jnp.exp(E[pos_of[c] - 1] - E[pp]) if pos_of[c] > pp else zero for c in range(G)])
            ).astype(BF16) for pp in range(G - 1)]
    ks = k * jnp.exp(per_chunk(b_last) - b)
    scale_rows = [jnp.exp(E[G - 1] - E[p]) for p in range(G)] + [jnp.exp(E[G - 1])] + [zero] * (8 - G - 1)
    scale_cols = jnp.concatenate(scale_rows, axis=0).T
    lane_chunk = lax.broadcasted_iota(I32, (1, R), 1) // C

    for h in range(GLA_HEADS):
        hk = slice(h * GLA_DK, (h + 1) * GLA_DK)
        hv = slice(h * GLA_DV, (h + 1) * GLA_DV)
        v = v_ref[rows, hv]
        state = st_ref[h]
        s = lax.dot_general(qa[:, hk], ka[:, hk], (((1,), (1,)), ((), ())), preferred_element_type=F32)
        s = jnp.where(keep, s, 0.0).astype(BF16)
        ks_t = ks[:, hk].T
        cols = scale_cols[hk]
        to_end = jnp.zeros((GLA_DK, R), F32)
        for p in range(G):
            to_end = jnp.where(lane_chunk == order[p], cols[:, p:p + 1], to_end)
        lhs_kv = jnp.concatenate([jnp.where(lane_chunk == order[p], ks_t, 0.0) for p in range(G - 1)]
                                 + [ks_t * to_end], axis=0).astype(BF16)
        kv = jnp.dot(lhs_kv, v, preferred_element_type=F32)
        n_x = (G - 1) * GLA_DK
        o = jnp.dot(jnp.concatenate([s, q_in[:, hk]] + [x[:, hk] for x in q_x], axis=-1),
                    jnp.concatenate([v, state.astype(BF16), kv[:n_x].astype(BF16)], axis=0),
                    preferred_element_type=F32)
        o_ref[rows, hv] = o.astype(BF16)
        st_ref[h] = cols[:, G:G + 1] * state + kv[n_x:]


def _gla_kernel(qf_ref, kf_ref, vf_ref, af_ref, qb_ref, kb_ref, vb_ref, ab_ref, u_ref, bias_ref,
                of_ref, ob_ref, st_ref, g_scr, *, n_chunks):
    @pl.when(pl.program_id(1) == 0)
    def _():
        st_ref[...] = jnp.zeros_like(st_ref)

    for d, a_ref in enumerate((af_ref, ab_ref)):
        z = jnp.dot(a_ref[...], u_ref[d], preferred_element_type=F32) + bias_ref[d]
        g_scr[d] = (jnp.minimum(z, 0.0) - jnp.log(1.0 + jnp.exp(-jnp.abs(z)))) * (1.0 / GLA_TAU)

    n_groups = n_chunks // GLA_GROUP
    group_rows = GLA_GROUP * GLA_CHUNK

    def body(it, carry):
        rf = pl.multiple_of(it * group_rows, group_rows)
        rb = pl.multiple_of((n_groups - 1 - it) * group_rows, group_rows)
        _gla_group(qf_ref, kf_ref, vf_ref, of_ref, g_scr.at[0], st_ref.at[0], rf, False)
        _gla_group(qb_ref, kb_ref, vb_ref, ob_ref, g_scr.at[1], st_ref.at[1], rb, True)
        return carry

    lax.fori_loop(0, n_groups, body, 0)


def _gla(qg, kg, vg, al, u_pad, bias, n_chunks=16):
    B, S, _ = qg.shape
    rows = n_chunks * GLA_CHUNK
    NB = S // rows
    fwd = lambda b, n: (b, n, 0)
    bwd = lambda b, n: (b, NB - 1 - n, 0)
    specs = lambda im: [pl.BlockSpec((None, rows, t.shape[-1]), im) for t in (qg, kg, vg, al)]
    out = jax.ShapeDtypeStruct((B, S, GLA_HEADS * GLA_DV), BF16)
    return pl.pallas_call(
        functools.partial(_gla_kernel, n_chunks=n_chunks),
        grid=(B, NB),
        in_specs=specs(fwd) + specs(bwd) + [
            pl.BlockSpec(u_pad.shape, lambda b, n: (0, 0, 0)),
            pl.BlockSpec(bias.shape, lambda b, n: (0, 0, 0)),
        ],
        out_specs=[pl.BlockSpec((None, rows, GLA_HEADS * GLA_DV), fwd),
                   pl.BlockSpec((None, rows, GLA_HEADS * GLA_DV), bwd)],
        out_shape=[out, out],
        scratch_shapes=[pltpu.VMEM((2, GLA_HEADS, GLA_DK, GLA_DV), F32),
                        pltpu.VMEM((2, rows, GLA_HEADS * GLA_DK), F32)],
        compiler_params=_cparams(("arbitrary", "arbitrary"), 48),
        name="gla",
    )(qg, kg, vg, al, qg, kg, vg, al, u_pad, bias)


def _attn_kernel(q_ref, k_ref, v_ref, o_ref, lse_ref, *, sub, win, half):
    L = k_ref.shape[0]
    lq = q_ref.shape[0]
    base = pl.program_id(2) * lq

    def body(t, carry):
        r0 = pl.multiple_of(t * sub, sub)
        rows = pl.ds(r0, sub)
        q0 = base + r0
        k0 = pl.multiple_of(jnp.clip(q0 - half, 0, L - win), half)
        qpos = q0 + lax.broadcasted_iota(I32, (sub, 1), 0)
        kpos = k0 + lax.broadcasted_iota(I32, (1, win), 1)
        valid = jnp.abs(qpos - kpos) <= half
        q = q_ref[rows, :]
        k = k_ref[pl.ds(k0, win), :]
        v = v_ref[pl.ds(k0, win), :]
        own = ((lax.broadcasted_iota(I32, (ATT_HPG * win, ATT_GW), 0) // win)
               == (lax.broadcasted_iota(I32, (ATT_HPG * win, ATT_GW), 1) // ATT_HD))
        k_heads = jnp.where(own, jnp.concatenate([k] * ATT_HPG, axis=0), jnp.zeros((), BF16))
        v_heads = jnp.where(own, jnp.concatenate([v] * ATT_HPG, axis=0), jnp.zeros((), BF16))
        s_all = lax.dot_general(q, k_heads, (((1,), (1,)), ((), ())), preferred_element_type=F32)
        ps, ls, lses = [], [], []
        for h in range(ATT_HPG):
            s = jnp.where(valid, s_all[:, h * win:(h + 1) * win], -jnp.inf)
            m = jnp.max(s, axis=-1, keepdims=True)
            pr = jnp.exp(s - m)
            l = jnp.sum(pr, axis=-1, keepdims=True)
            ps.append(pr.astype(BF16))
            ls.append(jnp.broadcast_to(l, (sub, ATT_HD)))
            lses.append(jnp.broadcast_to(m + jnp.log(l), (sub, ATT_HD)))
        o = jnp.dot(jnp.concatenate(ps, axis=-1), v_heads, preferred_element_type=F32)
        o_ref[rows, :] = (o / jnp.concatenate(ls, axis=-1)).astype(BF16)
        lse_ref[rows, :] = jnp.concatenate(lses, axis=-1)
        return carry

    lax.fori_loop(0, lq // sub, body, 0)


def _attn_group(q, k, v, window, lq=1024, sub=128):
    B, d, L, _ = q.shape
    half = window // (2 * d)
    lq = min(lq, L)
    sub = min(sub, lq)
    win = min(sub + 2 * half, L)
    qmap = lambda b, r, i: (b, r, i, 0)
    kmap = lambda b, r, i: (b, r, 0, 0)
    return pl.pallas_call(
        functools.partial(_attn_kernel, sub=sub, win=win, half=half),
        grid=(B, d, L // lq),
        in_specs=[pl.BlockSpec((None, None, lq, ATT_GW), qmap),
                  pl.BlockSpec((None, None, L, ATT_GW), kmap),
                  pl.BlockSpec((None, None, L, ATT_GW), kmap)],
        out_specs=[pl.BlockSpec((None, None, lq, ATT_GW), qmap), pl.BlockSpec((None, None, lq, ATT_GW), qmap)],
        out_shape=[jax.ShapeDtypeStruct(q.shape, BF16), jax.ShapeDtypeStruct(q.shape, F32)],
        compiler_params=_cparams(("arbitrary", "arbitrary", "arbitrary"), 40),
        name=f"attn_d{d}",
    )(q, k, v)


def _sigmoid(t):
    return 1.0 / (1.0 + jnp.exp(-t))


def _merge_kernel(of_ref, ob_ref, rg_ref, gn_ref, a0_ref, a1_ref, a2_ref, l0_ref, l1_ref, l2_ref,
                  ga_ref, gb_ref, x_ref, wa_ref, wb_ref, wo_ref, nf_ref, rw_ref, rb_ref,
                  x1_ref, h2_ref, ti_ref, tw_ref, cnt_ref, *scratch):
    def token_rows(blk_ref, scr):
        d, n, w = blk_ref.shape
        if d == 1:
            return blk_ref[0].astype(F32)
        for r in range(d):
            for j in range(w // LANES):
                scr[j, pl.ds(r, n, stride=d), :] = blk_ref[r, :, j * LANES:(j + 1) * LANES].astype(F32)
        return jnp.concatenate([scr[j] for j in range(w // LANES)], axis=-1)

    o = of_ref[...].astype(F32) + ob_ref[...].astype(F32)
    parts = []
    for h in range(GLA_HEADS):
        oh = o[:, h * GLA_DV:(h + 1) * GLA_DV]
        parts.append(oh * lax.rsqrt(jnp.mean(oh * oh, axis=-1, keepdims=True) + EPS))
    r = rg_ref[...].astype(F32)
    y_gla = jnp.concatenate(parts, axis=-1) * gn_ref[...] * (r * _sigmoid(r))

    l0, l1, l2 = (token_rows(l_ref, scr) for l_ref, scr in zip((l0_ref, l1_ref, l2_ref), scratch[0:3]))
    a0, a1, a2 = (token_rows(a_ref, scr) for a_ref, scr in zip((a0_ref, a1_ref, a2_ref), scratch[3:6]))
    m = jnp.maximum(jnp.maximum(l0, l1), l2)
    e0, e1, e2 = jnp.exp(l0 - m), jnp.exp(l1 - m), jnp.exp(l2 - m)
    y_att = (e0 * a0 + e1 * a1 + e2 * a2) / (e0 + e1 + e2)

    t_gla = jnp.dot(y_gla.astype(BF16), wa_ref[...], preferred_element_type=F32)
    t_att = jnp.dot(y_att.astype(BF16), wb_ref[...], preferred_element_type=F32)
    merged = _sigmoid(ga_ref[...].astype(F32)) * t_gla + _sigmoid(gb_ref[...].astype(F32)) * t_att
    x1 = x_ref[...] + jnp.dot(merged.astype(BF16), wo_ref[...], preferred_element_type=F32)
    x1_ref[...] = x1
    h2 = x1 * lax.rsqrt(jnp.mean(x1 * x1, axis=-1, keepdims=True) + EPS) * nf_ref[...]
    _to_slabs(h2_ref, h2)

    h_hi = h2.astype(BF16)
    h_lo = (h2 - h_hi.astype(F32)).astype(BF16)
    logits = jnp.dot(jnp.concatenate([h_hi, h_lo, h_hi], axis=-1), rw_ref[...],
                     preferred_element_type=F32) + rb_ref[...]
    lane = lax.broadcasted_iota(I32, logits.shape, 1).astype(F32)
    vals, idxs = [], []
    for _ in range(TOP_K):
        mx = jnp.max(logits, axis=-1, keepdims=True)
        ix = jnp.min(jnp.where(logits == mx, lane, float(LANES)), axis=-1, keepdims=True)
        vals.append(mx)
        idxs.append(ix)
        logits = jnp.where(lane == ix, -jnp.inf, logits)
    es = [jnp.exp(vk - vals[0]) for vk in vals]
    den = es[0] + es[1] + es[2] + es[3]
    ti = jnp.zeros(lane.shape, F32)
    tw = jnp.zeros(lane.shape, F32)
    for kk in range(TOP_K):
        ti = jnp.where(lane == float(kk), idxs[kk], ti)
        tw = jnp.where(lane == float(kk), es[kk] / den, tw)
    ti_ref[...] = ti.astype(I32)
    tw_ref[...] = tw

    @pl.when(pl.program_id(0) == 0)
    def _():
        cnt_ref[...] = jnp.zeros_like(cnt_ref)

    hits = sum((lane == ix).astype(F32) for ix in idxs)
    cnt_ref[...] += jnp.sum(hits, axis=0, keepdims=True)


def _merge(o_f, o_b, rg, gn, atts, lses, ga, gb, x2, wa, wb, wo, nf, rw, rb, tm=256):
    T, D = x2.shape
    B = atts[0].shape[0]
    nb = T // B // tm
    row = lambda w: pl.BlockSpec((tm, w), lambda i: (i, 0))
    const = lambda a: pl.BlockSpec(a.shape, lambda i: (0, 0))
    res = lambda a: pl.BlockSpec((None, a.shape[1], tm // a.shape[1], ATT_GW), lambda i: (i // nb, 0, i % nb, 0))
    return pl.pallas_call(
        _merge_kernel,
        grid=(T // tm,),
        in_specs=[row(D), row(D), row(D), const(gn)] + [res(a) for a in atts] + [res(a) for a in lses]
                 + [row(D), row(D), row(D), const(wa), const(wb), const(wo), const(nf), const(rw), const(rb)],
        out_specs=[row(D), pl.BlockSpec((tm * SLAB, LANES), lambda i: (i, 0)), row(LANES), row(LANES),
                   pl.BlockSpec((1, LANES), lambda i: (0, 0))],
        out_shape=[jax.ShapeDtypeStruct((T, D), F32), jax.ShapeDtypeStruct((T * SLAB, LANES), F32),
                   jax.ShapeDtypeStruct((T, LANES), I32), jax.ShapeDtypeStruct((T, LANES), F32),
                   jax.ShapeDtypeStruct((1, LANES), F32)],
        scratch_shapes=[pltpu.VMEM((ATT_GW // LANES, tm, LANES), F32)] * 6,
        compiler_params=_cparams(("arbitrary",), 48),
        name="merge_router",
    )(o_f, o_b, rg, gn, *atts, *lses, ga, gb, x2, wa, wb, wo, nf, rw, rb)


SLAB = 8


def _to_slabs(ref, val):
    rows = val.shape[0]
    for s in range(SLAB):
        ref[pl.ds(s, rows, stride=SLAB), :] = val[:, s * LANES:(s + 1) * LANES]


def _from_slabs(ref, rows):
    return jnp.concatenate([ref[pl.ds(s, rows, stride=SLAB), :] for s in range(SLAB)], axis=-1)


K_PHASES = 4
N_XBUF = 3


def _experts_kernel(te_ref, dst_ref, h_hbm, w1_ref, b1_ref, w2_ref, b2_ref,
                    y_hbm, buf, w1p, w2b, sem_g, sem_s, *, tm, n_tok):
    i = pl.program_id(0)
    last = pl.num_programs(0) - 1
    slot = i % 2
    xslot = i % N_XBUF
    D = w2b.shape[0]

    def gather_start(row, s, r0, r1):
        for r in range(r0, r1):
            tok = dst_ref[row, r] & (n_tok - 1)
            pltpu.make_async_copy(h_hbm.at[pl.ds(pl.multiple_of(tok * SLAB, SLAB), SLAB), :],
                                  buf.at[s, pl.ds(r * SLAB, SLAB), :], sem_g.at[s]).start(priority=r % 2)

    def gather_wait(s):
        pltpu.make_async_copy(h_hbm.at[pl.ds(0, tm * SLAB), :], buf.at[s], sem_g.at[s]).wait()

    def scatter_start(row, s, r0, r1):
        for r in range(r0, r1):
            d = dst_ref[row, r]
            pltpu.make_async_copy(buf.at[N_XBUF + s, pl.ds(r * SLAB, SLAB), :],
                                  y_hbm.at[pl.ds(pl.multiple_of(d * SLAB, SLAB), SLAB), :], sem_s.at[s]).start()

    def scatter_wait(s):
        pltpu.make_async_copy(buf.at[N_XBUF + s], y_hbm.at[pl.ds(0, tm * SLAB), :], sem_s.at[s]).wait()

    @pl.when(i == 0)
    def _():
        buf[N_XBUF + 1] = jnp.zeros(buf.shape[1:], F32)
        gather_start(1, 0, 0, tm)
        gather_start(jnp.minimum(2, last), 1, 0, tm)

    @pl.when((i == 0) | (te_ref[i] != te_ref[jnp.maximum(i - 1, 0)]))
    def _():
        kk = lax.broadcasted_iota(I32, (2 * LANES, 2 * LANES), 0)
        nn = lax.broadcasted_iota(I32, (2 * LANES, 2 * LANES), 1)
        perm = (kk == jnp.where(nn < LANES, 2 * nn, 2 * (nn - LANES) + 1)).astype(BF16)
        for c in range(w1p.shape[1] // (2 * LANES)):
            cs = slice(c * 2 * LANES, (c + 1) * 2 * LANES)
            w1p[:, cs] = jnp.dot(w1_ref[:, cs].astype(BF16), perm, preferred_element_type=F32).astype(BF16)
        w2b[...] = w2_ref[...].astype(BF16)

    gather_wait(xslot)
    nxt = jnp.minimum(i + 3, last)
    nxt_slot = (i + 2) % N_XBUF
    rows_per = tm // K_PHASES
    slabs_per = SLAB // K_PHASES
    hid = b1_ref[...]
    for j in range(K_PHASES):
        gather_start(nxt, nxt_slot, j * rows_per, (j + 1) * rows_per)
        scatter_start(i, 1 - slot, j * rows_per, (j + 1) * rows_per)
        xj = jnp.concatenate([buf[xslot, pl.ds(s, tm, stride=SLAB), :]
                              for s in range(j * slabs_per, (j + 1) * slabs_per)], axis=-1).astype(BF16)
        kw = slabs_per * LANES
        hid = hid + jnp.dot(xj, w1p[j * kw:(j + 1) * kw, :], preferred_element_type=F32)
    acts = []
    for c in range(D // LANES):
        gate = jnp.minimum(hid[:, 2 * c * LANES:(2 * c + 1) * LANES], SWIGLU_LIMIT)
        up = jnp.clip(hid[:, (2 * c + 1) * LANES:(2 * c + 2) * LANES], -SWIGLU_LIMIT, SWIGLU_LIMIT)
        acts.append(((up + 1.0) * (gate * _sigmoid(gate * SWIGLU_ALPHA))).astype(BF16))
    act = jnp.concatenate(acts, axis=-1)
    out = jnp.dot(act, w2b[...], preferred_element_type=F32) + b2_ref[...]

    @pl.when(i > 0)
    def _():
        scatter_wait(slot)

    _to_slabs(buf.at[N_XBUF + slot], out)

    @pl.when(i == last)
    def _():
        scatter_wait(1 - slot)
        gather_wait((i + 1) % N_XBUF)
        gather_wait((i + 2) % N_XBUF)


def _experts(te_ext, dst_ext, h2s, w1, b1p, w2, b2, tm):
    T = h2s.shape[0] // SLAB
    n_steps = dst_ext.shape[0]
    _, D, F = w1.shape
    by_expert = lambda r, c: pl.BlockSpec((None, r, c), lambda i, te, dst: (te[i], 0, 0))
    return pl.pallas_call(
        functools.partial(_experts_kernel, tm=tm, n_tok=T),
        grid_spec=pltpu.PrefetchScalarGridSpec(
            num_scalar_prefetch=2,
            grid=(n_steps,),
            in_specs=[pl.BlockSpec(memory_space=pl.ANY), by_expert(D, F), by_expert(1, F),
                      by_expert(F // 2, D), by_expert(1, D)],
            out_specs=pl.BlockSpec(memory_space=pl.ANY),
            scratch_shapes=[pltpu.VMEM((N_XBUF + 2, tm * SLAB, LANES), F32),
                            pltpu.VMEM((D, F), BF16), pltpu.VMEM((F // 2, D), BF16),
                            pltpu.SemaphoreType.DMA((N_XBUF,)), pltpu.SemaphoreType.DMA((2,))],
        ),
        out_shape=jax.ShapeDtypeStruct((n_steps * tm * SLAB, LANES), F32),
        compiler_params=pltpu.CompilerParams(dimension_semantics=("arbitrary",), vmem_limit_bytes=56 * MIB,
                                             disable_bounds_checks=True),
        name="experts",
    )(te_ext, dst_ext, h2s, w1, b1p, w2, b2)


def _route(topi, counts, tm):
    T = topi.shape[0]
    E = N_EXPERTS
    n_asg = T * TOP_K
    n_tiles = n_asg // tm + E
    e_flat = topi.reshape(n_asg)
    _, order = lax.sort((e_flat, jnp.arange(n_asg, dtype=I32)), num_keys=1, is_stable=True)
    cend = jnp.cumsum(counts)
    cstart = cend - counts
    padded = (counts + tm - 1) // tm * tm
    pend = jnp.cumsum(padded)
    pstart = pend - padded
    n_used = pend[-1] // tm
    tstart = jnp.arange(n_tiles, dtype=I32) * tm
    te_raw = jnp.sum(tstart[:, None] >= pend[None, :], axis=1, dtype=I32)
    last_e = jnp.sum((n_used - 1) * tm >= pend, dtype=I32)
    te = jnp.where(te_raw < E, te_raw, last_e)
    esc = jnp.minimum(te_raw, E - 1)
    j0 = tstart - pstart[esc]
    n_valid = jnp.where(te_raw < E, jnp.clip(counts[esc] - j0, 0, tm), 0)
    col = jnp.arange(tm, dtype=I32)[None, :]
    a = order[jnp.clip((cstart[esc] + j0)[:, None] + col, 0, n_asg - 1)]
    dst_valid = (a % TOP_K) * T + a // TOP_K
    cend_ext = jnp.concatenate([cend, jnp.full((1,), n_asg, I32)])
    dst_pad = (n_asg + tstart - cend_ext[te_raw])[:, None] + col
    dst = jnp.where(col < n_valid[:, None], dst_valid, dst_pad).astype(I32)
    dump = n_tiles * tm + jnp.arange(tm, dtype=I32)
    dst_ext = jnp.concatenate([dump[None, :], dst], axis=0)
    te_ext = jnp.concatenate([te, te[-1:]])
    return te_ext, dst_ext


def _final_kernel(x1_ref, y0_ref, y1_ref, y2_ref, y3_ref, tw_ref, p_ref, np_ref, wg_ref, wp_ref, nfin_ref, o_ref):
    tw = tw_ref[...]
    x2 = x1_ref[...]
    for kk, y_ref in enumerate((y0_ref, y1_ref, y2_ref, y3_ref)):
        x2 = x2 + tw[:, kk:kk + 1] * _from_slabs(y_ref, x2.shape[0])
    h3 = (x2 * lax.rsqrt(jnp.mean(x2 * x2, axis=-1, keepdims=True) + EPS) * np_ref[...]).astype(BF16)
    gate = _sigmoid(jnp.dot(h3, wg_ref[...], preferred_element_type=F32))
    pe = jnp.dot(p_ref[...].astype(BF16), wp_ref[...], preferred_element_type=F32)
    x3 = x2 + gate * pe
    o_ref[...] = x3 * lax.rsqrt(jnp.mean(x3 * x3, axis=-1, keepdims=True) + EPS) * nfin_ref[...]


def _final(x1, ybuf, tw, p2, n_ple, wg, wp, n_fin, tm=256):
    T, D = x1.shape
    nb = T // tm
    row = lambda w: pl.BlockSpec((tm, w), lambda i: (i, 0))
    const = lambda a: pl.BlockSpec(a.shape, lambda i: (0, 0))
    yspec = lambda kk: pl.BlockSpec((tm * SLAB, LANES), lambda i: (kk * nb + i, 0))
    return pl.pallas_call(
        _final_kernel,
        grid=(nb,),
        in_specs=[row(D)] + [yspec(kk) for kk in range(TOP_K)] + [row(LANES), row(p2.shape[1]),
                  const(n_ple), const(wg), const(wp), const(n_fin)],
        out_specs=row(D),
        out_shape=jax.ShapeDtypeStruct((T, D), F32),
        compiler_params=_cparams(("arbitrary",), 48),
        name="final",
    )(x1, ybuf, ybuf, ybuf, ybuf, tw, p2, n_ple, wg, wp, n_fin)


def kernel(x, p, positions, norm_mix, w_in, gla_fgate_up, gla_fgate_bias, gla_out_norm, w_branch_gla, w_branch_attn, w_out, norm_ffn, router_w, router_b, expert_w1, expert_b1, expert_w2, expert_b2, norm_ple, ple_gate_w, ple_proj, norm_final):
    B, S, D = x.shape
    T = B * S
    assert w_in.shape[0] == 1, "single-layer block: the final norm is fused into the layer's last kernel"
    x2 = x.reshape(T, D)
    pos2 = positions.reshape(T, 1)

    w = w_in[0]
    w_cat = jnp.concatenate([
        w[:, :3072],
        jnp.pad(w[:, 3072:3104], ((0, 0), (0, LANES - 2 * GLA_RANK))),
        w[:, 3104:3104 + 768] * (ATT_HD ** -0.5),
        w[:, 3104 + 768:],
    ], axis=1).astype(BF16)
    inv_freq = ROPE_THETA ** (-jnp.arange(0, ROT_DIM, 2, dtype=F32) / ROT_DIM)
    lane_f = (jnp.arange(LANES) % ATT_HD) % (ROT_DIM // 2)
    tab = jnp.zeros((8, LANES), F32).at[0].set(inv_freq[lane_f])
    u_pad = jnp.zeros((2, LANES, GLA_HEADS * GLA_DK), F32)
    u_pad = u_pad.at[0, :GLA_RANK].set(gla_fgate_up[0, 0]).at[1, GLA_RANK:2 * GLA_RANK].set(gla_fgate_up[0, 1])
    u_pad = u_pad.astype(BF16)
    bias = gla_fgate_bias[0].reshape(2, 1, GLA_HEADS * GLA_DK)

    qg, kg, vg, rg, al, ga, gb, *att_in = _inproj(x2, pos2, norm_mix[0].reshape(1, D), tab, w_cat, B)
    r3 = lambda t: t.reshape(B, S, t.shape[-1])
    o_f, o_b = _gla(r3(qg), r3(kg), r3(vg), r3(al), u_pad, bias)
    atts, lses = [], []
    for gi, (window, _) in enumerate(ATT_GROUPS):
        o_g, lse_g = _attn_group(*att_in[3 * gi:3 * gi + 3], window)
        atts.append(o_g)
        lses.append(lse_g)

    rw = jnp.pad(router_w[0], ((0, 0), (0, LANES - N_EXPERTS)))
    rw_hi = rw.astype(BF16)
    rw = jnp.concatenate([rw_hi, rw_hi, (rw - rw_hi.astype(F32)).astype(BF16)], axis=0)
    rb = jnp.concatenate([router_b[0], jnp.full((LANES - N_EXPERTS,), -jnp.inf, F32)]).reshape(1, LANES)
    x1, h2, topi, topw, cnt = _merge(
        o_f.reshape(T, D), o_b.reshape(T, D), rg, gla_out_norm[0].reshape(1, D), atts, lses, ga, gb, x2,
        w_branch_gla[0].astype(BF16), w_branch_attn[0].astype(BF16), w_out[0].astype(BF16),
        norm_ffn[0].reshape(1, D), rw, rb)

    tm_e = 256
    assert T & (T - 1) == 0, "token count must be a power of two (row index is masked out of the slot code)"
    te, dst2d = _route(topi[:, :TOP_K], cnt[0, :N_EXPERTS].astype(I32), tm_e)
    E, _, F = expert_w1[0].shape
    b1p = expert_b1[0].reshape(E, F // (2 * LANES), LANES, 2).transpose(0, 1, 3, 2).reshape(E, 1, F)
    ybuf = _experts(te, dst2d, h2, expert_w1[0], b1p, expert_w2[0], expert_b2[0][:, None, :], tm_e)

    out = _final(x1, ybuf, topw, p[0].reshape(T, -1), norm_ple[0].reshape(1, D), ple_gate_w[0].astype(BF16),
                 ple_proj[0].astype(BF16), norm_final.reshape(1, D))
    return out.reshape(B, S, D)
```

```python
import functools

import jax
import jax.numpy as jnp
from jax import lax
from jax.experimental import pallas as pl
from jax.experimental.pallas import tpu as pltpu

F32 = jnp.float32
BF16 = jnp.bfloat16
I32 = jnp.int32

EPS = 1e-6
GLA_HEADS = 4
GLA_DK = 128
GLA_DV = 256
GLA_RANK = 16
GLA_TAU = 16.0
GLA_CHUNK = 64
ATT_GROUPS = ((128, 1), (512, 4), (2048, 16))
ATT_HPG = 4
ATT_HD = 64
ATT_GW = ATT_HPG * ATT_HD
ROT_DIM = 16
ROPE_THETA = 500000.0
N_EXPERTS = 32
TOP_K = 4
SWIGLU_ALPHA = 1.702
SWIGLU_LIMIT = 7.0

LANES = 128
MIB = 1024 * 1024

SEG_WIDTHS = (512, 512, 1024, 1024, LANES, 768, 768, 768, 1024, 1024)
SEG_OFFS = tuple(sum(SEG_WIDTHS[:i]) for i in range(len(SEG_WIDTHS) + 1))


def _cparams(sem, vmem_mib):
    return pltpu.CompilerParams(dimension_semantics=sem, vmem_limit_bytes=vmem_mib * MIB)


def _inproj_kernel(x_ref, pos_ref, g_ref, tab_ref, w_ref, *refs):
    (qg_ref, kg_ref, vg_ref, rg_ref, al_ref, ga_ref, gb_ref), att_refs, ysc = refs[:7], refs[7:16], refs[16]
    x = x_ref[...]
    h = (x * lax.rsqrt(jnp.mean(x * x, axis=-1, keepdims=True) + EPS) * g_ref[...]).astype(BF16)

    def proj(seg):
        return jnp.dot(h, w_ref[:, SEG_OFFS[seg]:SEG_OFFS[seg + 1]], preferred_element_type=F32)

    qg_ref[...] = proj(0).astype(BF16)
    kg_ref[...] = proj(1).astype(BF16)
    vg_ref[...] = proj(2).astype(BF16)
    rg_ref[...] = proj(3).astype(BF16)
    al_ref[...] = proj(4).astype(BF16)
    ga_ref[...] = proj(8).astype(BF16)
    gb_ref[...] = proj(9).astype(BF16)

    ang = pos_ref[...].astype(F32) * tab_ref[0:1, :]
    cs = jnp.cos(ang)
    sn = jnp.sin(ang)
    lane = lax.broadcasted_iota(I32, (1, LANES), 1) % ATT_HD
    c_mul = jnp.where(lane < ROT_DIM, cs, 1.0)
    s_next = jnp.where(lane < ROT_DIM // 2, -sn, 0.0)
    s_prev = jnp.where((lane >= ROT_DIM // 2) & (lane < ROT_DIM), sn, 0.0)

    tm = ysc.shape[1]
    tiles_per_group = ATT_GW // LANES

    def emit(seg, which, rotary):
        y = proj(seg)
        for j in range(SEG_WIDTHS[seg] // LANES):
            t = y[:, j * LANES:(j + 1) * LANES]
            if rotary:
                t = (t * c_mul + pltpu.roll(t, LANES - ROT_DIM // 2, axis=1) * s_next
                     + pltpu.roll(t, ROT_DIM // 2, axis=1) * s_prev)
            ysc[j] = t
        for gi, (_, d) in enumerate(ATT_GROUPS):
            o_ref = att_refs[3 * gi + which]
            for r in range(d):
                for j in range(tiles_per_group):
                    o_ref[r, :, j * LANES:(j + 1) * LANES] = ysc[
                        gi * tiles_per_group + j, pl.ds(r, tm // d, stride=d), :].astype(BF16)

    emit(5, 0, True)
    emit(6, 1, True)
    emit(7, 2, False)


def _inproj(x2, pos2, gain, tab, w_cat, B, tm=512):
    T, D = x2.shape
    S = T // B
    nb = S // tm
    row_widths = [SEG_WIDTHS[s] for s in (0, 1, 2, 3, 4, 8, 9)]
    outs = [jax.ShapeDtypeStruct((T, w), BF16) for w in row_widths]
    out_specs = [pl.BlockSpec((tm, w), lambda i: (i, 0)) for w in row_widths]
    for _, d in ATT_GROUPS:
        for _ in range(3):
            outs.append(jax.ShapeDtypeStruct((B, d, S // d, ATT_GW), BF16))
            out_specs.append(pl.BlockSpec((None, d, tm // d, ATT_GW), lambda i: (i // nb, 0, i % nb, 0)))
    return pl.pallas_call(
        _inproj_kernel,
        grid=(T // tm,),
        in_specs=[
            pl.BlockSpec((tm, D), lambda i: (i, 0)),
            pl.BlockSpec((tm, 1), lambda i: (i, 0)),
            pl.BlockSpec((1, D), lambda i: (0, 0)),
            pl.BlockSpec((8, LANES), lambda i: (0, 0)),
            pl.BlockSpec((D, SEG_OFFS[-1]), lambda i: (0, 0), pipeline_mode=pl.Buffered(1)),
        ],
        out_specs=out_specs,
        out_shape=outs,
        scratch_shapes=[pltpu.VMEM((SEG_WIDTHS[5] // LANES, tm, LANES), F32)],
        compiler_params=_cparams(("arbitrary",), 56),
        name="inproj",
    )(x2, pos2, gain, tab, w_cat)


def _split3(a):
    a1 = a.astype(BF16)
    r1 = a - a1.astype(F32)
    a2 = r1.astype(BF16)
    a3 = (r1 - a2.astype(F32)).astype(BF16)
    return a1, a2, a3


GLA_GROUP = 4


def _gla_group(q_ref, k_ref, v_ref, o_ref, g_ref, st_ref, r0, backward):
    C, G = GLA_CHUNK, GLA_GROUP
    R = C * G
    HK = GLA_HEADS * GLA_DK
    ri = lax.broadcasted_iota(I32, (R, R), 0)
    ci = lax.broadcasted_iota(I32, (R, R), 1)
    same = (ri // C) == (ci // C)
    if backward:
        tri = (same & (ci >= ri)).astype(BF16)
        keep = same & (ci > ri)
        ref_off, last_off = C // 2, 0
        order = list(range(G - 1, -1, -1))
    else:
        tri = (same & (ci <= ri)).astype(BF16)
        keep = same & (ci <= ri)
        ref_off, last_off = C // 2 - 1, C - 1
        order = list(range(G))
    pos_of = {c: p for p, c in enumerate(order)}
    rows = pl.ds(r0, R)

    def per_chunk(vals):
        return jnp.concatenate([jnp.broadcast_to(v, (C, HK)) for v in vals], axis=0)

    g1, g2, g3 = _split3(g_ref[rows, :])
    b = (jnp.dot(tri, g1, preferred_element_type=F32) + jnp.dot(tri, g2, preferred_element_type=F32)
         + jnp.dot(tri, g3, preferred_element_type=F32))
    b_last = [b[c * C + last_off:c * C + last_off + 1] for c in range(G)]
    b_mid = [b[c * C + ref_off:c * C + ref_off + 1] for c in range(G)]
    E = [b_last[order[0]]]
    for p in range(1, G):
        E.append(E[-1] + b_last[order[p]])
    zero = jnp.zeros((1, HK), F32)
    one = jnp.ones((1, HK), F32)

    q = q_ref[rows, :].astype(F32) * (GLA_DK ** -0.5)
    k = k_ref[rows, :].astype(F32)
    bm = per_chunk(b_mid)
    qa = (q * jnp.exp(b - bm)).astype(BF16)
    ka = (k * jnp.exp(bm - b)).astype(BF16)
    qi = q * jnp.exp(b)
    q_in = (qi * per_chunk([jnp.exp(E[pos_of[c] - 1]) if pos_of[c] > 0 else one for c in range(G)])).astype(BF16)
    q_x = [(qi * per_chunk([jnp.exp(E[pos_of[c] - 1] - E[pp]) if pos_of[c] > pp else zero for c in range(G)])
            ).astype(BF16) for pp in range(G - 1)]
    ks = k * jnp.exp(per_chunk(b_last) - b)
    scale_rows = [jnp.exp(E[G - 1] - E[p]) for p in range(G)] + [jnp.exp(E[G - 1])] + [zero] * (8 - G - 1)
    scale_cols = jnp.concatenate(scale_rows, axis=0).T
    lane_chunk = lax.broadcasted_iota(I32, (1, R), 1) // C

    for h in range(GLA_HEADS):
        hk = slice(h * GLA_DK, (h + 1) * GLA_DK)
        hv = slice(h * GLA_DV, (h + 1) * GLA_DV)
        v = v_ref[rows, hv]
        state = st_ref[h]
        s = lax.dot_general(qa[:, hk], ka[:, hk], (((1,), (1,)), ((), ())), preferred_element_type=F32)
        s = jnp.where(keep, s, 0.0).astype(BF16)
        ks_t = ks[:, hk].T
        cols = scale_cols[hk]
        to_end = jnp.zeros((GLA_DK, R), F32)
        for p in range(G):
            to_end = jnp.where(lane_chunk == order[p], cols[:, p:p + 1], to_end)
        lhs_kv = jnp.concatenate([jnp.where(lane_chunk == order[p], ks_t, 0.0) for p in range(G - 1)]
                                 + [ks_t * to_end], axis=0).astype(BF16)
        kv = jnp.dot(lhs_kv, v, preferred_element_type=F32)
        n_x = (G - 1) * GLA_DK
        o = jnp.dot(jnp.concatenate([s, q_in[:, hk]] + [x[:, hk] for x in q_x], axis=-1),
                    jnp.concatenate([v, state.astype(BF16), kv[:n_x].astype(BF16)], axis=0),
                    preferred_element_type=F32)
        o_ref[rows, hv] = o.astype(BF16)
        st_ref[h] = cols[:, G:G + 1] * state + kv[n_x:]


def _gla_kernel(qf_ref, kf_ref, vf_ref, af_ref, qb_ref, kb_ref, vb_ref, ab_ref, u_ref, bias_ref,
                of_ref, ob_ref, st_ref, g_scr, *, n_chunks):
    @pl.when(pl.program_id(1) == 0)
    def _():
        st_ref[...] = jnp.zeros_like(st_ref)

    for d, a_ref in enumerate((af_ref, ab_ref)):
        z = jnp.dot(a_ref[...], u_ref[d], preferred_element_type=F32) + bias_ref[d]
        g_scr[d] = (jnp.minimum(z, 0.0) - jnp.log(1.0 + jnp.exp(-jnp.abs(z)))) * (1.0 / GLA_TAU)

    n_groups = n_chunks // GLA_GROUP
    group_rows = GLA_GROUP * GLA_CHUNK

    def body(it, carry):
        rf = pl.multiple_of(it * group_rows, group_rows)
        rb = pl.multiple_of((n_groups - 1 - it) * group_rows, group_rows)
        _gla_group(qf_ref, kf_ref, vf_ref, of_ref, g_scr.at[0], st_ref.at[0], rf, False)
        _gla_group(qb_ref, kb_ref, vb_ref, ob_ref, g_scr.at[1], st_ref.at[1], rb, True)
        return carry

    lax.fori_loop(0, n_groups, body, 0)


def _gla(qg, kg, vg, al, u_pad, bias, n_chunks=16):
    B, S, _ = qg.shape
    rows = n_chunks * GLA_CHUNK
    NB = S // rows
    fwd = lambda b, n: (b, n, 0)
    bwd = lambda b, n: (b, NB - 1 - n, 0)
    specs = lambda im: [pl.BlockSpec((None, rows, t.shape[-1]), im) for t in (qg, kg, vg, al)]
    out = jax.ShapeDtypeStruct((B, S, GLA_HEADS * GLA_DV), BF16)
    return pl.pallas_call(
        functools.partial(_gla_kernel, n_chunks=n_chunks),
        grid=(B, NB),
        in_specs=specs(fwd) + specs(bwd) + [
            pl.BlockSpec(u_pad.shape, lambda b, n: (0, 0, 0)),
            pl.BlockSpec(bias.shape, lambda b, n: (0, 0, 0)),
        ],
        out_specs=[pl.BlockSpec((None, rows, GLA_HEADS * GLA_DV), fwd),
                   pl.BlockSpec((None, rows, GLA_HEADS * GLA_DV), bwd)],
        out_shape=[out, out],
        scratch_shapes=[pltpu.VMEM((2, GLA_HEADS, GLA_DK, GLA_DV), F32),
                        pltpu.VMEM((2, rows, GLA_HEADS * GLA_DK), F32)],
        compiler_params=_cparams(("arbitrary", "arbitrary"), 48),
        name="gla",
    )(qg, kg, vg, al, qg, kg, vg, al, u_pad, bias)


def _attn_kernel(q_ref, k_ref, v_ref, o_ref, lse_ref, *, sub, win, half):
    L = k_ref.shape[0]
    lq = q_ref.shape[0]
    base = pl.program_id(2) * lq
    own_f = ((lax.broadcasted_iota(I32, (ATT_HPG * win, ATT_GW), 0) // win)
             == (lax.broadcasted_iota(I32, (ATT_HPG * win, ATT_GW), 1) // ATT_HD)).astype(BF16)

    def body(t, carry):
        r0 = pl.multiple_of(t * sub, sub)
        rows = pl.ds(r0, sub)
        q0 = base + r0
        k0 = pl.multiple_of(jnp.clip(q0 - half, 0, L - win), half)
        qpos = q0 + lax.broadcasted_iota(I32, (sub, 1), 0)
        kpos = k0 + lax.broadcasted_iota(I32, (1, win), 1)
        valid = jnp.abs(qpos - kpos) <= half
        q = q_ref[rows, :]
        k = k_ref[pl.ds(k0, win), :]
        v = v_ref[pl.ds(k0, win), :]
        own = own_f > 0
        k_heads = jnp.where(own, jnp.concatenate([k] * ATT_HPG, axis=0), jnp.zeros((), BF16))
        v_heads = jnp.where(own, jnp.concatenate([v] * ATT_HPG, axis=0), jnp.zeros((), BF16))
        s_all = lax.dot_general(q, k_heads, (((1,), (1,)), ((), ())), preferred_element_type=F32)
        ps, ls, lses = [], [], []
        for h in range(ATT_HPG):
            s = jnp.where(valid, s_all[:, h * win:(h + 1) * win], -jnp.inf)
            m = jnp.max(s, axis=-1, keepdims=True)
            pr = jnp.exp(s - m)
            l = jnp.sum(pr, axis=-1, keepdims=True)
            ps.append(pr.astype(BF16))
            ls.append(jnp.broadcast_to(l, (sub, ATT_HD)))
            lses.append(jnp.broadcast_to(m + jnp.log(l), (sub, ATT_HD)))
        o = jnp.dot(jnp.concatenate(ps, axis=-1), v_heads, preferred_element_type=F32)
        o_ref[rows, :] = (o / jnp.concatenate(ls, axis=-1)).astype(BF16)
        lse_ref[rows, :] = jnp.concatenate(lses, axis=-1)
        return carry

    lax.fori_loop(0, lq // sub, body, 0)


def _attn_group(q, k, v, window, lq=1024, sub=128):
    B, d, L, _ = q.shape
    half = window // (2 * d)
    lq = min(lq, L)
    sub = min(sub, lq)
    win = min(sub + 2 * half, L)
    qmap = lambda b, r, i: (b, r, i, 0)
    kmap = lambda b, r, i: (b, r, 0, 0)
    return pl.pallas_call(
        functools.partial(_attn_kernel, sub=sub, win=win, half=half),
        grid=(B, d, L // lq),
        in_specs=[pl.BlockSpec((None, None, lq, ATT_GW), qmap),
                  pl.BlockSpec((None, None, L, ATT_GW), kmap),
                  pl.BlockSpec((None, None, L, ATT_GW), kmap)],
        out_specs=[pl.BlockSpec((None, None, lq, ATT_GW), qmap), pl.BlockSpec((None, None, lq, ATT_GW), qmap)],
        out_shape=[jax.ShapeDtypeStruct(q.shape, BF16), jax.ShapeDtypeStruct(q.shape, F32)],
        compiler_params=_cparams(("arbitrary", "arbitrary", "arbitrary"), 40),
        name=f"attn_d{d}",
    )(q, k, v)


def _sigmoid(t):
    return 1.0 / (1.0 + jnp.exp(-t))


def _merge_kernel(of_ref, ob_ref, rg_ref, gn_ref, a0_ref, a1_ref, a2_ref, l0_ref, l1_ref, l2_ref,
                  ga_ref, gb_ref, x_ref, wa_ref, wb_ref, wo_ref, nf_ref, rw_ref, rb_ref,
                  x1_ref, h2_ref, ti_ref, tw_ref, cnt_ref, *scratch):
    def token_rows(blk_ref, scr):
        d, n, w = blk_ref.shape
        if d == 1:
            return blk_ref[0].astype(F32)
        for r in range(d):
            for j in range(w // LANES):
                scr[j, pl.ds(r, n, stride=d), :] = blk_ref[r, :, j * LANES:(j + 1) * LANES].astype(F32)
        return jnp.concatenate([scr[j] for j in range(w // LANES)], axis=-1)

    o = of_ref[...].astype(F32) + ob_ref[...].astype(F32)
    parts = []
    for h in range(GLA_HEADS):
        oh = o[:, h * GLA_DV:(h + 1) * GLA_DV]
        parts.append(oh * lax.rsqrt(jnp.mean(oh * oh, axis=-1, keepdims=True) + EPS))
    r = rg_ref[...].astype(F32)
    y_gla = jnp.concatenate(parts, axis=-1) * gn_ref[...] * (r * _sigmoid(r))

    l0, l1, l2 = (token_rows(l_ref, scr) for l_ref, scr in zip((l0_ref, l1_ref, l2_ref), scratch[0:3]))
    a0, a1, a2 = (token_rows(a_ref, scr) for a_ref, scr in zip((a0_ref, a1_ref, a2_ref), scratch[3:6]))
    m = jnp.maximum(jnp.maximum(l0, l1), l2)
    e0, e1, e2 = jnp.exp(l0 - m), jnp.exp(l1 - m), jnp.exp(l2 - m)
    y_att = (e0 * a0 + e1 * a1 + e2 * a2) / (e0 + e1 + e2)

    t_gla = jnp.dot(y_gla.astype(BF16), wa_ref[...], preferred_element_type=F32)
    t_att = jnp.dot(y_att.astype(BF16), wb_ref[...], preferred_element_type=F32)
    merged = _sigmoid(ga_ref[...].astype(F32)) * t_gla + _sigmoid(gb_ref[...].astype(F32)) * t_att
    x1 = x_ref[...] + jnp.dot(merged.astype(BF16), wo_ref[...], preferred_element_type=F32)
    x1_ref[...] = x1
    h2 = x1 * lax.rsqrt(jnp.mean(x1 * x1, axis=-1, keepdims=True) + EPS) * nf_ref[...]
    _to_slabs(h2_ref, h2)

    h_hi = h2.astype(BF16)
    h_lo = (h2 - h_hi.astype(F32)).astype(BF16)
    logits = jnp.dot(jnp.concatenate([h_hi, h_lo, h_hi], axis=-1), rw_ref[...],
                     preferred_element_type=F32) + rb_ref[...]
    lane = lax.broadcasted_iota(I32, logits.shape, 1).astype(F32)
    vals, idxs = [], []
    for _ in range(TOP_K):
        mx = jnp.max(logits, axis=-1, keepdims=True)
        ix = jnp.min(jnp.where(logits == mx, lane, float(LANES)), axis=-1, keepdims=True)
        vals.append(mx)
        idxs.append(ix)
        logits = jnp.where(lane == ix, -jnp.inf, logits)
    es = [jnp.exp(vk - vals[0]) for vk in vals]
    den = es[0] + es[1] + es[2] + es[3]
    ti = jnp.zeros(lane.shape, F32)
    tw = jnp.zeros(lane.shape, F32)
    for kk in range(TOP_K):
        ti = jnp.where(lane == float(kk), idxs[kk], ti)
        tw = jnp.where(lane == float(kk), es[kk] / den, tw)
    ti_ref[...] = ti.astype(I32)
    tw_ref[...] = tw

    @pl.when(pl.program_id(0) == 0)
    def _():
        cnt_ref[...] = jnp.zeros_like(cnt_ref)

    hits = sum((lane == ix).astype(F32) for ix in idxs)
    cnt_ref[...] += jnp.sum(hits, axis=0, keepdims=True)


def _merge(o_f, o_b, rg, gn, atts, lses, ga, gb, x2, wa, wb, wo, nf, rw, rb, tm=256):
    T, D = x2.shape
    B = atts[0].shape[0]
    nb = T // B // tm
    row = lambda w: pl.BlockSpec((tm, w), lambda i: (i, 0))
    const = lambda a: pl.BlockSpec(a.shape, lambda i: (0, 0))
    res = lambda a: pl.BlockSpec((None, a.shape[1], tm // a.shape[1], ATT_GW), lambda i: (i // nb, 0, i % nb, 0))
    return pl.pallas_call(
        _merge_kernel,
        grid=(T // tm,),
        in_specs=[row(D), row(D), row(D), const(gn)] + [res(a) for a in atts] + [res(a) for a in lses]
                 + [row(D), row(D), row(D), const(wa), const(wb), const(wo), const(nf), const(rw), const(rb)],
        out_specs=[row(D), pl.BlockSpec((tm * SLAB, LANES), lambda i: (i, 0)), row(LANES), row(LANES),
                   pl.BlockSpec((1, LANES), lambda i: (0, 0))],
        out_shape=[jax.ShapeDtypeStruct((T, D), F32), jax.ShapeDtypeStruct((T * SLAB, LANES), F32),
                   jax.ShapeDtypeStruct((T, LANES), I32), jax.ShapeDtypeStruct((T, LANES), F32),
                   jax.ShapeDtypeStruct((1, LANES), F32)],
        scratch_shapes=[pltpu.VMEM((ATT_GW // LANES, tm, LANES), F32)] * 6,
        compiler_params=_cparams(("arbitrary",), 48),
        name="merge_router",
    )(o_f, o_b, rg, gn, *atts, *lses, ga, gb, x2, wa, wb, wo, nf, rw, rb)


SLAB = 8


def _to_slabs(ref, val):
    rows = val.shape[0]
    for s in range(SLAB):
        ref[pl.ds(s, rows, stride=SLAB), :] = val[:, s * LANES:(s + 1) * LANES]


def _from_slabs(ref, rows):
    return jnp.concatenate([ref[pl.ds(s, rows, stride=SLAB), :] for s in range(SLAB)], axis=-1)


K_PHASES = 4
DMA_SHARES = (1, 3, 3, 3, 3, 3)
N_XBUF = 3


def _experts_kernel(te_ref, dst_ref, h_hbm, w1_ref, b1_ref, w2_ref, b2_ref,
                    y_hbm, buf, w1p, w2b, sem_g, sem_s, *, tm, n_tok):
    i = pl.program_id(0)
    last = pl.num_programs(0) - 1
    slot = i % 2
    xslot = i % N_XBUF
    D = w2b.shape[0]

    def gather_start(row, s, r0, r1):
        for r in range(r0, r1):
            tok = dst_ref[row, r] & (n_tok - 1)
            pltpu.make_async_copy(h_hbm.at[pl.ds(pl.multiple_of(tok * SLAB, SLAB), SLAB), :],
                                  buf.at[s, pl.ds(r * SLAB, SLAB), :], sem_g.at[s]).start()

    def gather_wait(s):
        pltpu.make_async_copy(h_hbm.at[pl.ds(0, tm * SLAB), :], buf.at[s], sem_g.at[s]).wait()

    def scatter_start(row, s, r0, r1):
        for r in range(r0, r1):
            d = dst_ref[row, r]
            pltpu.make_async_copy(buf.at[N_XBUF + s, pl.ds(r * SLAB, SLAB), :],
                                  y_hbm.at[pl.ds(pl.multiple_of(d * SLAB, SLAB), SLAB), :], sem_s.at[s]).start()

    def scatter_wait(s):
        pltpu.make_async_copy(buf.at[N_XBUF + s], y_hbm.at[pl.ds(0, tm * SLAB), :], sem_s.at[s]).wait()

    @pl.when(i == 0)
    def _():
        buf[N_XBUF + 1] = jnp.zeros(buf.shape[1:], F32)
        gather_start(1, 0, 0, tm)
        gather_start(jnp.minimum(2, last), 1, 0, tm)

    @pl.when((i == 0) | (te_ref[i] != te_ref[jnp.maximum(i - 1, 0)]))
    def _():
        kk = lax.broadcasted_iota(I32, (2 * LANES, 2 * LANES), 0)
        nn = lax.broadcasted_iota(I32, (2 * LANES, 2 * LANES), 1)
        perm = (kk == jnp.where(nn < LANES, 2 * nn, 2 * (nn - LANES) + 1)).astype(BF16)
        for c in range(w1p.shape[1] // (2 * LANES)):
            cs = slice(c * 2 * LANES, (c + 1) * 2 * LANES)
            w1p[:, cs] = jnp.dot(w1_ref[:, cs].astype(BF16), perm, preferred_element_type=F32).astype(BF16)
        w2b[...] = w2_ref[...].astype(BF16)

    gather_wait(xslot)
    nxt = jnp.minimum(i + 3, last)
    nxt_slot = (i + 2) % N_XBUF
    bounds = [0]
    for share in DMA_SHARES:
        bounds.append(bounds[-1] + share * tm // sum(DMA_SHARES))

    def issue(phase):
        gather_start(nxt, nxt_slot, bounds[phase], bounds[phase + 1])
        scatter_start(i, 1 - slot, bounds[phase], bounds[phase + 1])

    slabs_per = SLAB // K_PHASES
    hid = b1_ref[...]
    for j in range(K_PHASES):
        issue(j)
        xj = jnp.concatenate([buf[xslot, pl.ds(s, tm, stride=SLAB), :]
                              for s in range(j * slabs_per, (j + 1) * slabs_per)], axis=-1).astype(BF16)
        kw = slabs_per * LANES
        hid = hid + jnp.dot(xj, w1p[j * kw:(j + 1) * kw, :], preferred_element_type=F32)
    acts = []
    for c in range(D // LANES):
        gate = jnp.minimum(hid[:, 2 * c * LANES:(2 * c + 1) * LANES], SWIGLU_LIMIT)
        up = jnp.clip(hid[:, (2 * c + 1) * LANES:(2 * c + 2) * LANES], -SWIGLU_LIMIT, SWIGLU_LIMIT)
        acts.append(((up + 1.0) * (gate * _sigmoid(gate * SWIGLU_ALPHA))).astype(BF16))
    act = jnp.concatenate(acts, axis=-1)

    @pl.when(i > 0)
    def _():
        scatter_wait(slot)

    half = D // 2
    for j in range(2):
        issue(K_PHASES + j)
        out = (jnp.dot(act, w2b[:, j * half:(j + 1) * half], preferred_element_type=F32)
               + b2_ref[:, j * half:(j + 1) * half])
        res = buf.at[N_XBUF + slot]
        for s in range(j * SLAB // 2, (j + 1) * SLAB // 2):
            res[pl.ds(s, tm, stride=SLAB), :] = out[:, (s - j * SLAB // 2) * LANES:(s - j * SLAB // 2 + 1) * LANES]

    @pl.when(i == last)
    def _():
        scatter_wait(1 - slot)
        gather_wait((i + 1) % N_XBUF)
        gather_wait((i + 2) % N_XBUF)


def _experts(te_ext, dst_ext, h2s, w1, b1p, w2, b2, tm):
    T = h2s.shape[0] // SLAB
    n_steps = dst_ext.shape[0]
    _, D, F = w1.shape
    by_expert = lambda r, c: pl.BlockSpec((None, r, c), lambda i, te, dst: (te[i], 0, 0))
    return pl.pallas_call(
        functools.partial(_experts_kernel, tm=tm, n_tok=T),
        grid_spec=pltpu.PrefetchScalarGridSpec(
            num_scalar_prefetch=2,
            grid=(n_steps,),
            in_specs=[pl.BlockSpec(memory_space=pl.ANY), by_expert(D, F), by_expert(1, F),
                      by_expert(F // 2, D), by_expert(1, D)],
            out_specs=pl.BlockSpec(memory_space=pl.ANY),
            scratch_shapes=[pltpu.VMEM((N_XBUF + 2, tm * SLAB, LANES), F32),
                            pltpu.VMEM((D, F), BF16), pltpu.VMEM((F // 2, D), BF16),
                            pltpu.SemaphoreType.DMA((N_XBUF,)), pltpu.SemaphoreType.DMA((2,))],
        ),
        out_shape=jax.ShapeDtypeStruct((n_steps * tm * SLAB, LANES), F32),
        compiler_params=pltpu.CompilerParams(dimension_semantics=("arbitrary",), vmem_limit_bytes=56 * MIB,
                                             disable_bounds_checks=True),
        name="experts",
    )(te_ext, dst_ext, h2s, w1, b1p, w2, b2)


def _route(topi, counts, tm):
    T = topi.shape[0]
    E = N_EXPERTS
    n_asg = T * TOP_K
    n_tiles = n_asg // tm + E
    e_flat = topi.reshape(n_asg)
    _, order = lax.sort((e_flat, jnp.arange(n_asg, dtype=I32)), num_keys=1, is_stable=True)
    cend = jnp.cumsum(counts)
    cstart = cend - counts
    padded = (counts + tm - 1) // tm * tm
    pend = jnp.cumsum(padded)
    pstart = pend - padded
    n_used = pend[-1] // tm
    tstart = jnp.arange(n_tiles, dtype=I32) * tm
    te_raw = jnp.sum(tstart[:, None] >= pend[None, :], axis=1, dtype=I32)
    last_e = jnp.sum((n_used - 1) * tm >= pend, dtype=I32)
    te = jnp.where(te_raw < E, te_raw, last_e)
    esc = jnp.minimum(te_raw, E - 1)
    j0 = tstart - pstart[esc]
    n_valid = jnp.where(te_raw < E, jnp.clip(counts[esc] - j0, 0, tm), 0)
    col = jnp.arange(tm, dtype=I32)[None, :]
    a = order[jnp.clip((cstart[esc] + j0)[:, None] + col, 0, n_asg - 1)]
    dst_valid = (a % TOP_K) * T + a // TOP_K
    cend_ext = jnp.concatenate([cend, jnp.full((1,), n_asg, I32)])
    dst_pad = (n_asg + tstart - cend_ext[te_raw])[:, None] + col
    dst = jnp.where(col < n_valid[:, None], dst_valid, dst_pad).astype(I32)
    dump = n_tiles * tm + jnp.arange(tm, dtype=I32)
    dst_ext = jnp.concatenate([dump[None, :], dst], axis=0)
    te_ext = jnp.concatenate([te, te[-1:]])
    return te_ext, dst_ext


def _final_kernel(x1_ref, y0_ref, y1_ref, y2_ref, y3_ref, tw_ref, p_ref, np_ref, wg_ref, wp_ref, nfin_ref, o_ref):
    tw = tw_ref[...]
    x2 = x1_ref[...]
    for kk, y_ref in enumerate((y0_ref, y1_ref, y2_ref, y3_ref)):
        x2 = x2 + tw[:, kk:kk + 1] * _from_slabs(y_ref, x2.shape[0])
    h3 = (x2 * lax.rsqrt(jnp.mean(x2 * x2, axis=-1, keepdims=True) + EPS) * np_ref[...]).astype(BF16)
    gate = _sigmoid(jnp.dot(h3, wg_ref[...], preferred_element_type=F32))
    pe = jnp.dot(p_ref[...].astype(BF16), wp_ref[...], preferred_element_type=F32)
    x3 = x2 + gate * pe
    o_ref[...] = x3 * lax.rsqrt(jnp.mean(x3 * x3, axis=-1, keepdims=True) + EPS) * nfin_ref[...]


def _final(x1, ybuf, tw, p2, n_ple, wg, wp, n_fin, tm=256):
    T, D = x1.shape
    nb = T // tm
    row = lambda w: pl.BlockSpec((tm, w), lambda i: (i, 0))
    const = lambda a: pl.BlockSpec(a.shape, lambda i: (0, 0))
    yspec = lambda kk: pl.BlockSpec((tm * SLAB, LANES), lambda i: (kk * nb + i, 0))
    return pl.pallas_call(
        _final_kernel,
        grid=(nb,),
        in_specs=[row(D)] + [yspec(kk) for kk in range(TOP_K)] + [row(LANES), row(p2.shape[1]),
                  const(n_ple), const(wg), const(wp), const(n_fin)],
        out_specs=row(D),
        out_shape=jax.ShapeDtypeStruct((T, D), F32),
        compiler_params=_cparams(("arbitrary",), 48),
        name="final",
    )(x1, ybuf, ybuf, ybuf, ybuf, tw, p2, n_ple, wg, wp, n_fin)


def kernel(x, p, positions, norm_mix, w_in, gla_fgate_up, gla_fgate_bias, gla_out_norm, w_branch_gla, w_branch_attn, w_out, norm_ffn, router_w, router_b, expert_w1, expert_b1, expert_w2, expert_b2, norm_ple, ple_gate_w, ple_proj, norm_final):
    B, S, D = x.shape
    T = B * S
    assert w_in.shape[0] == 1, "single-layer block: the final norm is fused into the layer's last kernel"
    x2 = x.reshape(T, D)
    pos2 = positions.reshape(T, 1)

    w = w_in[0]
    w_cat = jnp.concatenate([
        w[:, :3072],
        jnp.pad(w[:, 3072:3104], ((0, 0), (0, LANES - 2 * GLA_RANK))),
        w[:, 3104:3104 + 768] * (ATT_HD ** -0.5),
        w[:, 3104 + 768:],
    ], axis=1).astype(BF16)
    inv_freq = ROPE_THETA ** (-jnp.arange(0, ROT_DIM, 2, dtype=F32) / ROT_DIM)
    lane_f = (jnp.arange(LANES) % ATT_HD) % (ROT_DIM // 2)
    tab = jnp.zeros((8, LANES), F32).at[0].set(inv_freq[lane_f])
    u_pad = jnp.zeros((2, LANES, GLA_HEADS * GLA_DK), F32)
    u_pad = u_pad.at[0, :GLA_RANK].set(gla_fgate_up[0, 0]).at[1, GLA_RANK:2 * GLA_RANK].set(gla_fgate_up[0, 1])
    u_pad = u_pad.astype(BF16)
    bias = gla_fgate_bias[0].reshape(2, 1, GLA_HEADS * GLA_DK)

    qg, kg, vg, rg, al, ga, gb, *att_in = _inproj(x2, pos2, norm_mix[0].reshape(1, D), tab, w_cat, B)
    r3 = lambda t: t.reshape(B, S, t.shape[-1])
    o_f, o_b = _gla(r3(qg), r3(kg), r3(vg), r3(al), u_pad, bias)
    atts, lses = [], []
    for gi, (window, _) in enumerate(ATT_GROUPS):
        o_g, lse_g = _attn_group(*att_in[3 * gi:3 * gi + 3], window)
        atts.append(o_g)
        lses.append(lse_g)

    rw = jnp.pad(router_w[0], ((0, 0), (0, LANES - N_EXPERTS)))
    rw_hi = rw.astype(BF16)
    rw = jnp.concatenate([rw_hi, rw_hi, (rw - rw_hi.astype(F32)).astype(BF16)], axis=0)
    rb = jnp.concatenate([router_b[0], jnp.full((LANES - N_EXPERTS,), -jnp.inf, F32)]).reshape(1, LANES)
    x1, h2, topi, topw, cnt = _merge(
        o_f.reshape(T, D), o_b.reshape(T, D), rg, gla_out_norm[0].reshape(1, D), atts, lses, ga, gb, x2,
        w_branch_gla[0].astype(BF16), w_branch_attn[0].astype(BF16), w_out[0].astype(BF16),
        norm_ffn[0].reshape(1, D), rw, rb)

    tm_e = 256
    assert T & (T - 1) == 0, "token count must be a power of two (row index is masked out of the slot code)"
    te, dst2d = _route(topi[:, :TOP_K], cnt[0, :N_EXPERTS].astype(I32), tm_e)
    E, _, F = expert_w1[0].shape
    b1p = expert_b1[0].reshape(E, F // (2 * LANES), LANES, 2).transpose(0, 1, 3, 2).reshape(E, 1, F)
    ybuf = _experts(te, dst2d, h2, expert_w1[0], b1p, expert_w2[0], expert_b2[0][:, None, :], tm_e)

    out = _final(x1, ybuf, topw, p[0].reshape(T, -1), norm_ple[0].reshape(1, D), ple_gate_w[0].astype(BF16),
                 ple_proj[0].astype(BF16), norm_final.reshape(1, D))
    return out.reshape(B, S, D)
```

```python
import functools

import jax
import jax.numpy as jnp
from jax import lax
from jax.experimental import pallas as pl
from jax.experimental.pallas import tpu as pltpu

F32 = jnp.float32
BF16 = jnp.bfloat16
I32 = jnp.int32

EPS = 1e-6
GLA_HEADS = 4
GLA_DK = 128
GLA_DV = 256
GLA_RANK = 16
GLA_TAU = 16.0
GLA_CHUNK = 64
ATT_GROUPS = ((128, 1), (512, 4), (2048, 16))
ATT_HPG = 4
ATT_HD = 64
ATT_GW = ATT_HPG * ATT_HD
ROT_DIM = 16
ROPE_THETA = 500000.0
N_EXPERTS = 32
TOP_K = 4
SWIGLU_ALPHA = 1.702
SWIGLU_LIMIT = 7.0

LANES = 128
MIB = 1024 * 1024

TM_INPROJ = 512
TM_MERGE = 256
TM_FINAL = 256
TM_EXPERT = 256
GLA_BLOCK_CHUNKS = 16
ATT_BLOCK_ROWS = 1024
ATT_SUB_ROWS = 128
VMEM_MIB = {"inproj": 56, "gla": 48, "attn": 40, "merge": 48, "experts": 56, "final": 48}

SEG_WIDTHS = (512, 512, 1024, 1024, LANES, 768, 768, 768, 1024, 1024)
SEG_OFFS = tuple(sum(SEG_WIDTHS[:i]) for i in range(len(SEG_WIDTHS) + 1))


def _cparams(sem, vmem_mib):
    return pltpu.CompilerParams(dimension_semantics=sem, vmem_limit_bytes=vmem_mib * MIB)


def _inproj_kernel(x_ref, pos_ref, g_ref, tab_ref, w_ref, *refs):
    (qg_ref, kg_ref, vg_ref, rg_ref, al_ref, ga_ref, gb_ref), att_refs, ysc = refs[:7], refs[7:16], refs[16]
    x = x_ref[...]
    h = (x * lax.rsqrt(jnp.mean(x * x, axis=-1, keepdims=True) + EPS) * g_ref[...]).astype(BF16)

    def proj(seg):
        return jnp.dot(h, w_ref[:, SEG_OFFS[seg]:SEG_OFFS[seg + 1]], preferred_element_type=F32)

    qg_ref[...] = proj(0).astype(BF16)
    kg_ref[...] = proj(1).astype(BF16)
    vg_ref[...] = proj(2).astype(BF16)
    rg_ref[...] = proj(3).astype(BF16)
    al_ref[...] = proj(4).astype(BF16)
    ga_ref[...] = proj(8).astype(BF16)
    gb_ref[...] = proj(9).astype(BF16)

    ang = pos_ref[...].astype(F32) * tab_ref[0:1, :]
    cs = jnp.cos(ang)
    sn = jnp.sin(ang)
    lane = lax.broadcasted_iota(I32, (1, LANES), 1) % ATT_HD
    c_mul = jnp.where(lane < ROT_DIM, cs, 1.0)
    s_next = jnp.where(lane < ROT_DIM // 2, -sn, 0.0)
    s_prev = jnp.where((lane >= ROT_DIM // 2) & (lane < ROT_DIM), sn, 0.0)

    tm = ysc.shape[1]
    tiles_per_group = ATT_GW // LANES

    def emit(seg, which, rotary):
        y = proj(seg)
        for j in range(SEG_WIDTHS[seg] // LANES):
            t = y[:, j * LANES:(j + 1) * LANES]
            if rotary:
                t = (t * c_mul + pltpu.roll(t, LANES - ROT_DIM // 2, axis=1) * s_next
                     + pltpu.roll(t, ROT_DIM // 2, axis=1) * s_prev)
            ysc[j] = t
        for gi, (_, d) in enumerate(ATT_GROUPS):
            o_ref = att_refs[3 * gi + which]
            for r in range(d):
                for j in range(tiles_per_group):
                    o_ref[r, :, j * LANES:(j + 1) * LANES] = ysc[
                        gi * tiles_per_group + j, pl.ds(r, tm // d, stride=d), :].astype(BF16)

    emit(5, 0, True)
    emit(6, 1, True)
    emit(7, 2, False)


def _inproj(x2, pos2, gain, tab, w_cat, B, tm=TM_INPROJ):
    T, D = x2.shape
    S = T // B
    nb = S // tm
    row_widths = [SEG_WIDTHS[s] for s in (0, 1, 2, 3, 4, 8, 9)]
    outs = [jax.ShapeDtypeStruct((T, w), BF16) for w in row_widths]
    out_specs = [pl.BlockSpec((tm, w), lambda i: (i, 0)) for w in row_widths]
    for _, d in ATT_GROUPS:
        for _ in range(3):
            outs.append(jax.ShapeDtypeStruct((B, d, S // d, ATT_GW), BF16))
            out_specs.append(pl.BlockSpec((None, d, tm // d, ATT_GW), lambda i: (i // nb, 0, i % nb, 0)))
    return pl.pallas_call(
        _inproj_kernel,
        grid=(T // tm,),
        in_specs=[
            pl.BlockSpec((tm, D), lambda i: (i, 0)),
            pl.BlockSpec((tm, 1), lambda i: (i, 0)),
            pl.BlockSpec((1, D), lambda i: (0, 0)),
            pl.BlockSpec((8, LANES), lambda i: (0, 0)),
            pl.BlockSpec((D, SEG_OFFS[-1]), lambda i: (0, 0), pipeline_mode=pl.Buffered(1)),
        ],
        out_specs=out_specs,
        out_shape=outs,
        scratch_shapes=[pltpu.VMEM((SEG_WIDTHS[5] // LANES, tm, LANES), F32)],
        compiler_params=_cparams(("arbitrary",), VMEM_MIB["inproj"]),
        name="inproj",
    )(x2, pos2, gain, tab, w_cat)


def _split3(a):
    a1 = a.astype(BF16)
    r1 = a - a1.astype(F32)
    a2 = r1.astype(BF16)
    a3 = (r1 - a2.astype(F32)).astype(BF16)
    return a1, a2, a3


GLA_GROUP = 4


def _gla_group(q_ref, k_ref, v_ref, o_ref, g_ref, st_ref, r0, backward):
    C, G = GLA_CHUNK, GLA_GROUP
    R = C * G
    HK = GLA_HEADS * GLA_DK
    ri = lax.broadcasted_iota(I32, (R, R), 0)
    ci = lax.broadcasted_iota(I32, (R, R), 1)
    same = (ri // C) == (ci // C)
    if backward:
        tri = (same & (ci >= ri)).astype(BF16)
        keep = same & (ci > ri)
        ref_off, last_off = C // 2, 0
        order = list(range(G - 1, -1, -1))
    else:
        tri = (same & (ci <= ri)).astype(BF16)
        keep = same & (ci <= ri)
        ref_off, last_off = C // 2 - 1, C - 1
        order = list(range(G))
    pos_of = {c: p for p, c in enumerate(order)}
    rows = pl.ds(r0, R)

    def per_chunk(vals):
        return jnp.concatenate([jnp.broadcast_to(v, (C, HK)) for v in vals], axis=0)

    g1, g2, g3 = _split3(g_ref[rows, :])
    b = (jnp.dot(tri, g1, preferred_element_type=F32) + jnp.dot(tri, g2, preferred_element_type=F32)
         + jnp.dot(tri, g3, preferred_element_type=F32))
    b_last = [b[c * C + last_off:c * C + last_off + 1] for c in range(G)]
    b_mid = [b[c * C + ref_off:c * C + ref_off + 1] for c in range(G)]
    E = [b_last[order[0]]]
    for p in range(1, G):
        E.append(E[-1] + b_last[order[p]])
    zero = jnp.zeros((1, HK), F32)
    one = jnp.ones((1, HK), F32)

    q = q_ref[rows, :].astype(F32) * (GLA_DK ** -0.5)
    k = k_ref[rows, :].astype(F32)
    bm = per_chunk(b_mid)
    qa = (q * jnp.exp(b - bm)).astype(BF16)
    ka = (k * jnp.exp(bm - b)).astype(BF16)
    qi = q * jnp.exp(b)
    q_in = (qi * per_chunk([jnp.exp(E[pos_of[c] - 1]) if pos_of[c] > 0 else one for c in range(G)])).astype(BF16)
    q_x = [(qi * per_chunk([jnp.exp(E[pos_of[c] - 1] - E[pp]) if pos_of[c] > pp else zero for c in range(G)])
            ).astype(BF16) for pp in range(G - 1)]
    ks = k * jnp.exp(per_chunk(b_last) - b)
    scale_rows = [jnp.exp(E[G - 1] - E[p]) for p in range(G)] + [jnp.exp(E[G - 1])] + [zero] * (8 - G - 1)
    scale_cols = jnp.concatenate(scale_rows, axis=0).T
    lane_chunk = lax.broadcasted_iota(I32, (1, R), 1) // C

    for h in range(GLA_HEADS):
        hk = slice(h * GLA_DK, (h + 1) * GLA_DK)
        hv = slice(h * GLA_DV, (h + 1) * GLA_DV)
        v = v_ref[rows, hv]
        state = st_ref[h]
        s = lax.dot_general(qa[:, hk], ka[:, hk], (((1,), (1,)), ((), ())), preferred_element_type=F32)
        s = jnp.where(keep, s, 0.0).astype(BF16)
        ks_t = ks[:, hk].T
        cols = scale_cols[hk]
        to_end = jnp.zeros((GLA_DK, R), F32)
        for p in range(G):
            to_end = jnp.where(lane_chunk == order[p], cols[:, p:p + 1], to_end)
        lhs_kv = jnp.concatenate([jnp.where(lane_chunk == order[p], ks_t, 0.0) for p in range(G - 1)]
                                 + [ks_t * to_end], axis=0).astype(BF16)
        kv = jnp.dot(lhs_kv, v, preferred_element_type=F32)
        n_x = (G - 1) * GLA_DK
        o = jnp.dot(jnp.concatenate([s, q_in[:, hk]] + [x[:, hk] for x in q_x], axis=-1),
                    jnp.concatenate([v, state.astype(BF16), kv[:n_x].astype(BF16)], axis=0),
                    preferred_element_type=F32)
        o_ref[rows, hv] = o.astype(BF16)
        st_ref[h] = cols[:, G:G + 1] * state + kv[n_x:]


def _gla_kernel(qf_ref, kf_ref, vf_ref, af_ref, qb_ref, kb_ref, vb_ref, ab_ref, u_ref, bias_ref,
                of_ref, ob_ref, st_ref, g_scr, *, n_chunks):
    @pl.when(pl.program_id(1) == 0)
    def _():
        st_ref[...] = jnp.zeros_like(st_ref)

    for d, a_ref in enumerate((af_ref, ab_ref)):
        z = jnp.dot(a_ref[...], u_ref[d], preferred_element_type=F32) + bias_ref[d]
        g_scr[d] = (jnp.minimum(z, 0.0) - jnp.log(1.0 + jnp.exp(-jnp.abs(z)))) * (1.0 / GLA_TAU)

    n_groups = n_chunks // GLA_GROUP
    group_rows = GLA_GROUP * GLA_CHUNK

    def body(it, carry):
        rf = pl.multiple_of(it * group_rows, group_rows)
        rb = pl.multiple_of((n_groups - 1 - it) * group_rows, group_rows)
        _gla_group(qf_ref, kf_ref, vf_ref, of_ref, g_scr.at[0], st_ref.at[0], rf, False)
        _gla_group(qb_ref, kb_ref, vb_ref, ob_ref, g_scr.at[1], st_ref.at[1], rb, True)
        return carry

    lax.fori_loop(0, n_groups, body, 0)


def _gla(qg, kg, vg, al, u_pad, bias, n_chunks=GLA_BLOCK_CHUNKS):
    B, S, _ = qg.shape
    rows = n_chunks * GLA_CHUNK
    NB = S // rows
    fwd = lambda b, n: (b, n, 0)
    bwd = lambda b, n: (b, NB - 1 - n, 0)
    specs = lambda im: [pl.BlockSpec((None, rows, t.shape[-1]), im) for t in (qg, kg, vg, al)]
    out = jax.ShapeDtypeStruct((B, S, GLA_HEADS * GLA_DV), BF16)
    return pl.pallas_call(
        functools.partial(_gla_kernel, n_chunks=n_chunks),
        grid=(B, NB),
        in_specs=specs(fwd) + specs(bwd) + [
            pl.BlockSpec(u_pad.shape, lambda b, n: (0, 0, 0)),
            pl.BlockSpec(bias.shape, lambda b, n: (0, 0, 0)),
        ],
        out_specs=[pl.BlockSpec((None, rows, GLA_HEADS * GLA_DV), fwd),
                   pl.BlockSpec((None, rows, GLA_HEADS * GLA_DV), bwd)],
        out_shape=[out, out],
        scratch_shapes=[pltpu.VMEM((2, GLA_HEADS, GLA_DK, GLA_DV), F32),
                        pltpu.VMEM((2, rows, GLA_HEADS * GLA_DK), F32)],
        compiler_params=_cparams(("arbitrary", "arbitrary"), VMEM_MIB["gla"]),
        name="gla",
    )(qg, kg, vg, al, qg, kg, vg, al, u_pad, bias)


def _attn_kernel(q_ref, k_ref, v_ref, o_ref, lse_ref, *, sub, win, half):
    L = k_ref.shape[0]
    lq = q_ref.shape[0]
    base = pl.program_id(2) * lq

    def body(t, carry):
        r0 = pl.multiple_of(t * sub, sub)
        rows = pl.ds(r0, sub)
        q0 = base + r0
        k0 = pl.multiple_of(jnp.clip(q0 - half, 0, L - win), half)
        qpos = q0 + lax.broadcasted_iota(I32, (sub, 1), 0)
        kpos = k0 + lax.broadcasted_iota(I32, (1, win), 1)
        valid = jnp.abs(qpos - kpos) <= half
        q = q_ref[rows, :]
        k = k_ref[pl.ds(k0, win), :]
        v = v_ref[pl.ds(k0, win), :]
        own = ((lax.broadcasted_iota(I32, (ATT_HPG * win, ATT_GW), 0) // win)
               == (lax.broadcasted_iota(I32, (ATT_HPG * win, ATT_GW), 1) // ATT_HD))
        k_heads = jnp.where(own, jnp.concatenate([k] * ATT_HPG, axis=0), jnp.zeros((), BF16))
        v_heads = jnp.where(own, jnp.concatenate([v] * ATT_HPG, axis=0), jnp.zeros((), BF16))
        s_all = lax.dot_general(q, k_heads, (((1,), (1,)), ((), ())), preferred_element_type=F32)
        ps, ls, lses = [], [], []
        for h in range(ATT_HPG):
            s = jnp.where(valid, s_all[:, h * win:(h + 1) * win], -jnp.inf)
            m = jnp.max(s, axis=-1, keepdims=True)
            pr = jnp.exp(s - m)
            l = jnp.sum(pr, axis=-1, keepdims=True)
            ps.append(pr.astype(BF16))
            ls.append(jnp.broadcast_to(l, (sub, ATT_HD)))
            lses.append(jnp.broadcast_to(m + jnp.log(l), (sub, ATT_HD)))
        o = jnp.dot(jnp.concatenate(ps, axis=-1), v_heads, preferred_element_type=F32)
        o_ref[rows, :] = (o / jnp.concatenate(ls, axis=-1)).astype(BF16)
        lse_ref[rows, :] = jnp.concatenate(lses, axis=-1)
        return carry

    lax.fori_loop(0, lq // sub, body, 0)


def _attn_group(q, k, v, window, lq=ATT_BLOCK_ROWS, sub=ATT_SUB_ROWS):
    B, d, L, _ = q.shape
    half = window // (2 * d)
    lq = min(lq, L)
    sub = min(sub, lq)
    win = min(sub + 2 * half, L)
    qmap = lambda b, r, i: (b, r, i, 0)
    kmap = lambda b, r, i: (b, r, 0, 0)
    return pl.pallas_call(
        functools.partial(_attn_kernel, sub=sub, win=win, half=half),
        grid=(B, d, L // lq),
        in_specs=[pl.BlockSpec((None, None, lq, ATT_GW), qmap),
                  pl.BlockSpec((None, None, L, ATT_GW), kmap),
                  pl.BlockSpec((None, None, L, ATT_GW), kmap)],
        out_specs=[pl.BlockSpec((None, None, lq, ATT_GW), qmap), pl.BlockSpec((None, None, lq, ATT_GW), qmap)],
        out_shape=[jax.ShapeDtypeStruct(q.shape, BF16), jax.ShapeDtypeStruct(q.shape, F32)],
        compiler_params=_cparams(("arbitrary", "arbitrary", "arbitrary"), VMEM_MIB["attn"]),
        name=f"attn_d{d}",
    )(q, k, v)


def _sigmoid(t):
    return 0.5 * jnp.tanh(0.5 * t) + 0.5


def _merge_kernel(of_ref, ob_ref, rg_ref, gn_ref, a0_ref, a1_ref, a2_ref, l0_ref, l1_ref, l2_ref,
                  ga_ref, gb_ref, x_ref, wa_ref, wb_ref, wo_ref, nf_ref, rw_ref, rb_ref,
                  x1_ref, h2_ref, ti_ref, tw_ref, cnt_ref, *scratch):
    def token_rows(blk_ref, scr):
        d, n, w = blk_ref.shape
        if d == 1:
            return blk_ref[0].astype(F32)
        for r in range(d):
            for j in range(w // LANES):
                scr[j, pl.ds(r, n, stride=d), :] = blk_ref[r, :, j * LANES:(j + 1) * LANES].astype(F32)
        return jnp.concatenate([scr[j] for j in range(w // LANES)], axis=-1)

    o = of_ref[...].astype(F32) + ob_ref[...].astype(F32)
    parts = []
    for h in range(GLA_HEADS):
        oh = o[:, h * GLA_DV:(h + 1) * GLA_DV]
        parts.append(oh * lax.rsqrt(jnp.mean(oh * oh, axis=-1, keepdims=True) + EPS))
    r = rg_ref[...].astype(F32)
    y_gla = jnp.concatenate(parts, axis=-1) * gn_ref[...] * (r * _sigmoid(r))

    l0, l1, l2 = (token_rows(l_ref, scr) for l_ref, scr in zip((l0_ref, l1_ref, l2_ref), scratch[0:3]))
    a0, a1, a2 = (token_rows(a_ref, scr) for a_ref, scr in zip((a0_ref, a1_ref, a2_ref), scratch[3:6]))
    m = jnp.maximum(jnp.maximum(l0, l1), l2)
    e0, e1, e2 = jnp.exp(l0 - m), jnp.exp(l1 - m), jnp.exp(l2 - m)
    y_att = (e0 * a0 + e1 * a1 + e2 * a2) / (e0 + e1 + e2)

    t_gla = jnp.dot(y_gla.astype(BF16), wa_ref[...], preferred_element_type=F32)
    t_att = jnp.dot(y_att.astype(BF16), wb_ref[...], preferred_element_type=F32)
    merged = _sigmoid(ga_ref[...].astype(F32)) * t_gla + _sigmoid(gb_ref[...].astype(F32)) * t_att
    x1 = x_ref[...] + jnp.dot(merged.astype(BF16), wo_ref[...], preferred_element_type=F32)
    x1_ref[...] = x1
    h2 = x1 * lax.rsqrt(jnp.mean(x1 * x1, axis=-1, keepdims=True) + EPS) * nf_ref[...]
    _to_slabs(h2_ref, h2)

    h_hi = h2.astype(BF16)
    h_lo = (h2 - h_hi.astype(F32)).astype(BF16)
    logits = jnp.dot(jnp.concatenate([h_hi, h_lo, h_hi], axis=-1), rw_ref[...],
                     preferred_element_type=F32) + rb_ref[...]
    lane = lax.broadcasted_iota(I32, logits.shape, 1).astype(F32)
    vals, idxs = [], []
    for _ in range(TOP_K):
        mx = jnp.max(logits, axis=-1, keepdims=True)
        ix = jnp.min(jnp.where(logits == mx, lane, float(LANES)), axis=-1, keepdims=True)
        vals.append(mx)
        idxs.append(ix)
        logits = jnp.where(lane == ix, -jnp.inf, logits)
    es = [jnp.exp(vk - vals[0]) for vk in vals]
    den = es[0] + es[1] + es[2] + es[3]
    ti = jnp.zeros(lane.shape, F32)
    tw = jnp.zeros(lane.shape, F32)
    for kk in range(TOP_K):
        ti = jnp.where(lane == float(kk), idxs[kk], ti)
        tw = jnp.where(lane == float(kk), es[kk] / den, tw)
    ti_ref[...] = ti.astype(I32)
    tw_ref[...] = tw

    @pl.when(pl.program_id(0) == 0)
    def _():
        cnt_ref[...] = jnp.zeros_like(cnt_ref)

    hits = sum((lane == ix).astype(F32) for ix in idxs)
    cnt_ref[...] += jnp.sum(hits, axis=0, keepdims=True)


def _merge(o_f, o_b, rg, gn, atts, lses, ga, gb, x2, wa, wb, wo, nf, rw, rb, tm=TM_MERGE):
    T, D = x2.shape
    B = atts[0].shape[0]
    nb = T // B // tm
    row = lambda w: pl.BlockSpec((tm, w), lambda i: (i, 0))
    const = lambda a: pl.BlockSpec(a.shape, lambda i: (0, 0))
    res = lambda a: pl.BlockSpec((None, a.shape[1], tm // a.shape[1], ATT_GW), lambda i: (i // nb, 0, i % nb, 0))
    return pl.pallas_call(
        _merge_kernel,
        grid=(T // tm,),
        in_specs=[row(D), row(D), row(D), const(gn)] + [res(a) for a in atts] + [res(a) for a in lses]
                 + [row(D), row(D), row(D), const(wa), const(wb), const(wo), const(nf), const(rw), const(rb)],
        out_specs=[row(D), pl.BlockSpec((tm * SLAB, LANES), lambda i: (i, 0)), row(LANES), row(LANES),
                   pl.BlockSpec((1, LANES), lambda i: (0, 0))],
        out_shape=[jax.ShapeDtypeStruct((T, D), F32), jax.ShapeDtypeStruct((T * SLAB, LANES), F32),
                   jax.ShapeDtypeStruct((T, LANES), I32), jax.ShapeDtypeStruct((T, LANES), F32),
                   jax.ShapeDtypeStruct((1, LANES), F32)],
        scratch_shapes=[pltpu.VMEM((ATT_GW // LANES, tm, LANES), F32)] * 6,
        compiler_params=_cparams(("arbitrary",), VMEM_MIB["merge"]),
        name="merge_router",
    )(o_f, o_b, rg, gn, *atts, *lses, ga, gb, x2, wa, wb, wo, nf, rw, rb)


SLAB = 8


def _to_slabs(ref, val):
    rows = val.shape[0]
    for s in range(SLAB):
        ref[pl.ds(s, rows, stride=SLAB), :] = val[:, s * LANES:(s + 1) * LANES]


def _from_slabs(ref, rows):
    return jnp.concatenate([ref[pl.ds(s, rows, stride=SLAB), :] for s in range(SLAB)], axis=-1)


K_PHASES = 2
DMA_SHARES = (1, 3, 2, 2)
N_XBUF = 3


def _experts_kernel(te_ref, dst_ref, h_hbm, w1_ref, b1_ref, w2_ref, b2_ref,
                    y_hbm, buf, w1p, w2b, sem_g, sem_s, *, tm, n_tok):
    i = pl.program_id(0)
    last = pl.num_programs(0) - 1
    slot = i % 2
    xslot = i % N_XBUF
    D = w2b.shape[0]

    def gather_start(row, s, r0, r1):
        for r in range(r0, r1):
            tok = dst_ref[row, r] & (n_tok - 1)
            pltpu.make_async_copy(h_hbm.at[pl.ds(pl.multiple_of(tok * SLAB, SLAB), SLAB), :],
                                  buf.at[s, pl.ds(r * SLAB, SLAB), :], sem_g.at[s]).start()

    def gather_wait(s):
        pltpu.make_async_copy(h_hbm.at[pl.ds(0, tm * SLAB), :], buf.at[s], sem_g.at[s]).wait()

    def scatter_start(row, s, r0, r1):
        for r in range(r0, r1):
            d = dst_ref[row, r]
            pltpu.make_async_copy(buf.at[N_XBUF + s, pl.ds(r * SLAB, SLAB), :],
                                  y_hbm.at[pl.ds(pl.multiple_of(d * SLAB, SLAB), SLAB), :], sem_s.at[s]).start()

    def scatter_wait(s):
        pltpu.make_async_copy(buf.at[N_XBUF + s], y_hbm.at[pl.ds(0, tm * SLAB), :], sem_s.at[s]).wait()

    @pl.when(i == 0)
    def _():
        buf[N_XBUF + 1] = jnp.zeros(buf.shape[1:], F32)
        gather_start(1, 0, 0, tm)
        gather_start(jnp.minimum(2, last), 1, 0, tm)

    @pl.when((i == 0) | (te_ref[i] != te_ref[jnp.maximum(i - 1, 0)]))
    def _():
        kk = lax.broadcasted_iota(I32, (2 * LANES, 2 * LANES), 0)
        nn = lax.broadcasted_iota(I32, (2 * LANES, 2 * LANES), 1)
        perm = (kk == jnp.where(nn < LANES, 2 * nn, 2 * (nn - LANES) + 1)).astype(BF16)
        for c in range(w1p.shape[1] // (2 * LANES)):
            cs = slice(c * 2 * LANES, (c + 1) * 2 * LANES)
            w1p[:, cs] = jnp.dot(w1_ref[:, cs].astype(BF16), perm, preferred_element_type=F32).astype(BF16)
        w2b[...] = w2_ref[...].astype(BF16)

    gather_wait(xslot)
    nxt = jnp.minimum(i + 3, last)
    nxt_slot = (i + 2) % N_XBUF
    bounds = [0]
    for share in DMA_SHARES:
        bounds.append(bounds[-1] + share * tm // sum(DMA_SHARES))

    def issue(phase):
        gather_start(nxt, nxt_slot, bounds[phase], bounds[phase + 1])
        scatter_start(i, 1 - slot, bounds[phase], bounds[phase + 1])

    slabs_per = SLAB // K_PHASES
    hid = b1_ref[...]
    for j in range(K_PHASES):
        issue(j)
        xj = jnp.concatenate([buf[xslot, pl.ds(s, tm, stride=SLAB), :]
                              for s in range(j * slabs_per, (j + 1) * slabs_per)], axis=-1).astype(BF16)
        kw = slabs_per * LANES
        hid = hid + jnp.dot(xj, w1p[j * kw:(j + 1) * kw, :], preferred_element_type=F32)
    acts = []
    for c in range(D // LANES):
        gate = jnp.minimum(hid[:, 2 * c * LANES:(2 * c + 1) * LANES], SWIGLU_LIMIT)
        up = jnp.clip(hid[:, (2 * c + 1) * LANES:(2 * c + 2) * LANES], -SWIGLU_LIMIT, SWIGLU_LIMIT)
        acts.append(((up + 1.0) * (gate * _sigmoid(gate * SWIGLU_ALPHA))).astype(BF16))
    act = jnp.concatenate(acts, axis=-1)

    @pl.when(i > 0)
    def _():
        scatter_wait(slot)

    n_out = len(DMA_SHARES) - K_PHASES
    part = D // n_out
    res = buf.at[N_XBUF + slot]
    for j in range(n_out):
        issue(K_PHASES + j)
        out = (jnp.dot(act, w2b[:, j * part:(j + 1) * part], preferred_element_type=F32)
               + b2_ref[:, j * part:(j + 1) * part])
        for t in range(part // LANES):
            res[pl.ds(j * part // LANES + t, tm, stride=SLAB), :] = out[:, t * LANES:(t + 1) * LANES]

    @pl.when(i == last)
    def _():
        scatter_wait(1 - slot)
        gather_wait((i + 1) % N_XBUF)
        gather_wait((i + 2) % N_XBUF)


def _experts(te_ext, dst_ext, h2s, w1, b1p, w2, b2, tm):
    T = h2s.shape[0] // SLAB
    n_steps = dst_ext.shape[0]
    _, D, F = w1.shape
    by_expert = lambda r, c: pl.BlockSpec((None, r, c), lambda i, te, dst: (te[i], 0, 0))
    return pl.pallas_call(
        functools.partial(_experts_kernel, tm=tm, n_tok=T),
        grid_spec=pltpu.PrefetchScalarGridSpec(
            num_scalar_prefetch=2,
            grid=(n_steps,),
            in_specs=[pl.BlockSpec(memory_space=pl.ANY), by_expert(D, F), by_expert(1, F),
                      by_expert(F // 2, D), by_expert(1, D)],
            out_specs=pl.BlockSpec(memory_space=pl.ANY),
            scratch_shapes=[pltpu.VMEM((N_XBUF + 2, tm * SLAB, LANES), F32),
                            pltpu.VMEM((D, F), BF16), pltpu.VMEM((F // 2, D), BF16),
                            pltpu.SemaphoreType.DMA((N_XBUF,)), pltpu.SemaphoreType.DMA((2,))],
        ),
        out_shape=jax.ShapeDtypeStruct((n_steps * tm * SLAB, LANES), F32),
        compiler_params=pltpu.CompilerParams(dimension_semantics=("arbitrary",),
                                             vmem_limit_bytes=VMEM_MIB["experts"] * MIB,
                                             disable_bounds_checks=True),
        name="experts",
    )(te_ext, dst_ext, h2s, w1, b1p, w2, b2)


def _route(topi, counts, tm):
    T = topi.shape[0]
    E = N_EXPERTS
    n_asg = T * TOP_K
    n_tiles = n_asg // tm + E
    e_flat = topi.reshape(n_asg)
    _, order = lax.sort((e_flat, jnp.arange(n_asg, dtype=I32)), num_keys=1, is_stable=True)
    cend = jnp.cumsum(counts)
    cstart = cend - counts
    padded = (counts + tm - 1) // tm * tm
    pend = jnp.cumsum(padded)
    pstart = pend - padded
    n_used = pend[-1] // tm
    tstart = jnp.arange(n_tiles, dtype=I32) * tm
    te_raw = jnp.sum(tstart[:, None] >= pend[None, :], axis=1, dtype=I32)
    last_e = jnp.sum((n_used - 1) * tm >= pend, dtype=I32)
    te = jnp.where(te_raw < E, te_raw, last_e)
    esc = jnp.minimum(te_raw, E - 1)
    j0 = tstart - pstart[esc]
    n_valid = jnp.where(te_raw < E, jnp.clip(counts[esc] - j0, 0, tm), 0)
    col = jnp.arange(tm, dtype=I32)[None, :]
    a = order[jnp.clip((cstart[esc] + j0)[:, None] + col, 0, n_asg - 1)]
    dst_valid = (a % TOP_K) * T + a // TOP_K
    cend_ext = jnp.concatenate([cend, jnp.full((1,), n_asg, I32)])
    dst_pad = (n_asg + tstart - cend_ext[te_raw])[:, None] + col
    dst = jnp.where(col < n_valid[:, None], dst_valid, dst_pad).astype(I32)
    dump = n_tiles * tm + jnp.arange(tm, dtype=I32)
    dst_ext = jnp.concatenate([dump[None, :], dst], axis=0)
    te_ext = jnp.concatenate([te, te[-1:]])
    return te_ext, dst_ext


def _final_kernel(x1_ref, y0_ref, y1_ref, y2_ref, y3_ref, tw_ref, p_ref, np_ref, wg_ref, wp_ref, nfin_ref, o_ref):
    tw = tw_ref[...]
    x2 = x1_ref[...]
    for kk, y_ref in enumerate((y0_ref, y1_ref, y2_ref, y3_ref)):
        x2 = x2 + tw[:, kk:kk + 1] * _from_slabs(y_ref, x2.shape[0])
    h3 = (x2 * lax.rsqrt(jnp.mean(x2 * x2, axis=-1, keepdims=True) + EPS) * np_ref[...]).astype(BF16)
    gate = _sigmoid(jnp.dot(h3, wg_ref[...], preferred_element_type=F32))
    pe = jnp.dot(p_ref[...].astype(BF16), wp_ref[...], preferred_element_type=F32)
    x3 = x2 + gate * pe
    o_ref[...] = x3 * lax.rsqrt(jnp.mean(x3 * x3, axis=-1, keepdims=True) + EPS) * nfin_ref[...]


def _final(x1, ybuf, tw, p2, n_ple, wg, wp, n_fin, tm=TM_FINAL):
    T, D = x1.shape
    nb = T // tm
    row = lambda w: pl.BlockSpec((tm, w), lambda i: (i, 0))
    const = lambda a: pl.BlockSpec(a.shape, lambda i: (0, 0))
    yspec = lambda kk: pl.BlockSpec((tm * SLAB, LANES), lambda i: (kk * nb + i, 0))
    return pl.pallas_call(
        _final_kernel,
        grid=(nb,),
        in_specs=[row(D)] + [yspec(kk) for kk in range(TOP_K)] + [row(LANES), row(p2.shape[1]),
                  const(n_ple), const(wg), const(wp), const(n_fin)],
        out_specs=row(D),
        out_shape=jax.ShapeDtypeStruct((T, D), F32),
        compiler_params=_cparams(("arbitrary",), VMEM_MIB["final"]),
        name="final",
    )(x1, ybuf, ybuf, ybuf, ybuf, tw, p2, n_ple, wg, wp, n_fin)


def kernel(x, p, positions, norm_mix, w_in, gla_fgate_up, gla_fgate_bias, gla_out_norm, w_branch_gla, w_branch_attn, w_out, norm_ffn, router_w, router_b, expert_w1, expert_b1, expert_w2, expert_b2, norm_ple, ple_gate_w, ple_proj, norm_final):
    B, S, D = x.shape
    T = B * S
    assert w_in.shape[0] == 1, "single-layer block: the final norm is fused into the layer's last kernel"
    x2 = x.reshape(T, D)
    pos2 = positions.reshape(T, 1)

    w = w_in[0]
    w_cat = jnp.concatenate([
        w[:, :3072],
        jnp.pad(w[:, 3072:3104], ((0, 0), (0, LANES - 2 * GLA_RANK))),
        w[:, 3104:3104 + 768] * (ATT_HD ** -0.5),
        w[:, 3104 + 768:],
    ], axis=1).astype(BF16)
    inv_freq = ROPE_THETA ** (-jnp.arange(0, ROT_DIM, 2, dtype=F32) / ROT_DIM)
    lane_f = (jnp.arange(LANES) % ATT_HD) % (ROT_DIM // 2)
    tab = jnp.zeros((8, LANES), F32).at[0].set(inv_freq[lane_f])
    u_pad = jnp.zeros((2, LANES, GLA_HEADS * GLA_DK), F32)
    u_pad = u_pad.at[0, :GLA_RANK].set(gla_fgate_up[0, 0]).at[1, GLA_RANK:2 * GLA_RANK].set(gla_fgate_up[0, 1])
    u_pad = u_pad.astype(BF16)
    bias = gla_fgate_bias[0].reshape(2, 1, GLA_HEADS * GLA_DK)

    qg, kg, vg, rg, al, ga, gb, *att_in = _inproj(x2, pos2, norm_mix[0].reshape(1, D), tab, w_cat, B)
    r3 = lambda t: t.reshape(B, S, t.shape[-1])
    o_f, o_b = _gla(r3(qg), r3(kg), r3(vg), r3(al), u_pad, bias)
    atts, lses = [], []
    for gi, (window, _) in enumerate(ATT_GROUPS):
        o_g, lse_g = _attn_group(*att_in[3 * gi:3 * gi + 3], window)
        atts.append(o_g)
        lses.append(lse_g)

    rw = jnp.pad(router_w[0], ((0, 0), (0, LANES - N_EXPERTS)))
    rw_hi = rw.astype(BF16)
    rw = jnp.concatenate([rw_hi, rw_hi, (rw - rw_hi.astype(F32)).astype(BF16)], axis=0)
    rb = jnp.concatenate([router_b[0], jnp.full((LANES - N_EXPERTS,), -jnp.inf, F32)]).reshape(1, LANES)
    x1, h2, topi, topw, cnt = _merge(
        o_f.reshape(T, D), o_b.reshape(T, D), rg, gla_out_norm[0].reshape(1, D), atts, lses, ga, gb, x2,
        w_branch_gla[0].astype(BF16), w_branch_attn[0].astype(BF16), w_out[0].astype(BF16),
        norm_ffn[0].reshape(1, D), rw, rb)

    tm_e = TM_EXPERT
    assert T & (T - 1) == 0, "token count must be a power of two (row index is masked out of the slot code)"
    te, dst2d = _route(topi[:, :TOP_K], cnt[0, :N_EXPERTS].astype(I32), tm_e)
    E, _, F = expert_w1[0].shape
    b1p = expert_b1[0].reshape(E, F // (2 * LANES), LANES, 2).transpose(0, 1, 3, 2).reshape(E, 1, F)
    ybuf = _experts(te, dst2d, h2, expert_w1[0], b1p, expert_w2[0], expert_b2[0][:, None, :], tm_e)

    out = _final(x1, ybuf, topw, p[0].reshape(T, -1), norm_ple[0].reshape(1, D), ple_gate_w[0].astype(BF16),
                 ple_proj[0].astype(BF16), norm_final.reshape(1, D))
    return out.reshape(B, S, D)
```

```python
import functools

import jax
import jax.numpy as jnp
from jax import lax
from jax.experimental import pallas as pl
from jax.experimental.pallas import tpu as pltpu

F32 = jnp.float32
BF16 = jnp.bfloat16
I32 = jnp.int32

EPS = 1e-6
GLA_HEADS = 4
GLA_DK = 128
GLA_DV = 256
GLA_RANK = 16
GLA_TAU = 16.0
GLA_CHUNK = 64
ATT_GROUPS = ((128, 1), (512, 4), (2048, 16))
ATT_HPG = 4
ATT_HD = 64
ATT_GW = ATT_HPG * ATT_HD
ROT_DIM = 16
ROPE_THETA = 500000.0
N_EXPERTS = 32
TOP_K = 4
SWIGLU_ALPHA = 1.702
SWIGLU_LIMIT = 7.0

LANES = 128
MIB = 1024 * 1024

TM_INPROJ = 512
TM_MERGE = 256
TM_FINAL = 256
TM_EXPERT = 256
GLA_BLOCK_CHUNKS = 16
ATT_BLOCK_ROWS = 1024
ATT_SUB_ROWS = 128
VMEM_MIB = {"inproj": 56, "gla": 48, "attn": 40, "merge": 48, "experts": 56, "final": 48}

SEG_WIDTHS = (512, 512, 1024, 1024, LANES, 768, 768, 768, 1024, 1024)
SEG_OFFS = tuple(sum(SEG_WIDTHS[:i]) for i in range(len(SEG_WIDTHS) + 1))


def _cparams(sem, vmem_mib):
    return pltpu.CompilerParams(dimension_semantics=sem, vmem_limit_bytes=vmem_mib * MIB)


def _inproj_kernel(x_ref, pos_ref, g_ref, tab_ref, w_ref, *refs):
    (qg_ref, kg_ref, vg_ref, rg_ref, al_ref, ga_ref, gb_ref), att_refs, ysc = refs[:7], refs[7:16], refs[16]
    x = x_ref[...]
    h = (x * lax.rsqrt(jnp.mean(x * x, axis=-1, keepdims=True) + EPS) * g_ref[...]).astype(BF16)

    def proj(seg):
        return jnp.dot(h, w_ref[:, SEG_OFFS[seg]:SEG_OFFS[seg + 1]], preferred_element_type=F32)

    qg_ref[...] = proj(0).astype(BF16)
    kg_ref[...] = proj(1).astype(BF16)
    vg_ref[...] = proj(2).astype(BF16)
    rg_ref[...] = proj(3).astype(BF16)
    al_ref[...] = proj(4).astype(BF16)
    ga_ref[...] = proj(8).astype(BF16)
    gb_ref[...] = proj(9).astype(BF16)

    ang = pos_ref[...].astype(F32) * tab_ref[0:1, :]
    cs = jnp.cos(ang)
    sn = jnp.sin(ang)
    lane = lax.broadcasted_iota(I32, (1, LANES), 1) % ATT_HD
    c_mul = jnp.where(lane < ROT_DIM, cs, 1.0)
    s_next = jnp.where(lane < ROT_DIM // 2, -sn, 0.0)
    s_prev = jnp.where((lane >= ROT_DIM // 2) & (lane < ROT_DIM), sn, 0.0)

    tm = ysc.shape[1]
    tiles_per_group = ATT_GW // LANES

    def emit(seg, which, rotary):
        y = proj(seg)
        for j in range(SEG_WIDTHS[seg] // LANES):
            t = y[:, j * LANES:(j + 1) * LANES]
            if rotary:
                t = (t * c_mul + pltpu.roll(t, LANES - ROT_DIM // 2, axis=1) * s_next
                     + pltpu.roll(t, ROT_DIM // 2, axis=1) * s_prev)
            ysc[j] = t
        for gi, (_, d) in enumerate(ATT_GROUPS):
            o_ref = att_refs[3 * gi + which]
            for r in range(d):
                for j in range(tiles_per_group):
                    o_ref[r, :, j * LANES:(j + 1) * LANES] = ysc[
                        gi * tiles_per_group + j, pl.ds(r, tm // d, stride=d), :].astype(BF16)

    emit(5, 0, True)
    emit(6, 1, True)
    emit(7, 2, False)


def _inproj(x2, pos2, gain, tab, w_cat, B, tm=TM_INPROJ):
    T, D = x2.shape
    S = T // B
    nb = S // tm
    row_widths = [SEG_WIDTHS[s] for s in (0, 1, 2, 3, 4, 8, 9)]
    outs = [jax.ShapeDtypeStruct((T, w), BF16) for w in row_widths]
    out_specs = [pl.BlockSpec((tm, w), lambda i: (i, 0)) for w in row_widths]
    for _, d in ATT_GROUPS:
        for _ in range(3):
            outs.append(jax.ShapeDtypeStruct((B, d, S // d, ATT_GW), BF16))
            out_specs.append(pl.BlockSpec((None, d, tm // d, ATT_GW), lambda i: (i // nb, 0, i % nb, 0)))
    return pl.pallas_call(
        _inproj_kernel,
        grid=(T // tm,),
        in_specs=[
            pl.BlockSpec((tm, D), lambda i: (i, 0)),
            pl.BlockSpec((tm, 1), lambda i: (i, 0)),
            pl.BlockSpec((1, D), lambda i: (0, 0)),
            pl.BlockSpec((8, LANES), lambda i: (0, 0)),
            pl.BlockSpec((D, SEG_OFFS[-1]), lambda i: (0, 0), pipeline_mode=pl.Buffered(1)),
        ],
        out_specs=out_specs,
        out_shape=outs,
        scratch_shapes=[pltpu.VMEM((SEG_WIDTHS[5] // LANES, tm, LANES), F32)],
        compiler_params=_cparams(("arbitrary",), VMEM_MIB["inproj"]),
        name="inproj",
    )(x2, pos2, gain, tab, w_cat)


def _split3(a):
    a1 = a.astype(BF16)
    r1 = a - a1.astype(F32)
    a2 = r1.astype(BF16)
    a3 = (r1 - a2.astype(F32)).astype(BF16)
    return a1, a2, a3


GLA_GROUP = 4


def _gla_group(q_ref, k_ref, v_ref, o_ref, g_ref, st_ref, r0, backward):
    C, G = GLA_CHUNK, GLA_GROUP
    R = C * G
    HK = GLA_HEADS * GLA_DK
    ri = lax.broadcasted_iota(I32, (R, R), 0)
    ci = lax.broadcasted_iota(I32, (R, R), 1)
    same = (ri // C) == (ci // C)
    if backward:
        tri = (same & (ci >= ri)).astype(BF16)
        keep = same & (ci > ri)
        ref_off, last_off = C // 2, 0
        order = list(range(G - 1, -1, -1))
    else:
        tri = (same & (ci <= ri)).astype(BF16)
        keep = same & (ci <= ri)
        ref_off, last_off = C // 2 - 1, C - 1
        order = list(range(G))
    pos_of = {c: p for p, c in enumerate(order)}
    rows = pl.ds(r0, R)

    def per_chunk(vals):
        return jnp.concatenate([jnp.broadcast_to(v, (C, HK)) for v in vals], axis=0)

    g1, g2, g3 = _split3(g_ref[rows, :])
    b = (jnp.dot(tri, g1, preferred_element_type=F32) + jnp.dot(tri, g2, preferred_element_type=F32)
         + jnp.dot(tri, g3, preferred_element_type=F32))
    b_last = [b[c * C + last_off:c * C + last_off + 1] for c in range(G)]
    b_mid = [b[c * C + ref_off:c * C + ref_off + 1] for c in range(G)]
    E = [b_last[order[0]]]
    for p in range(1, G):
        E.append(E[-1] + b_last[order[p]])
    zero = jnp.zeros((1, HK), F32)
    one = jnp.ones((1, HK), F32)

    q = q_ref[rows, :].astype(F32) * (GLA_DK ** -0.5)
    k = k_ref[rows, :].astype(F32)
    bm = per_chunk(b_mid)
    qa = (q * jnp.exp(b - bm)).astype(BF16)
    ka = (k * jnp.exp(bm - b)).astype(BF16)
    qi = q * jnp.exp(b)
    q_in = (qi * per_chunk([jnp.exp(E[pos_of[c] - 1]) if pos_of[c] > 0 else one for c in range(G)])).astype(BF16)
    q_x = [(qi * per_chunk([jnp.exp(E[pos_of[c] - 1] - E[pp]) if pos_of[c] > pp else zero for c in range(G)])
            ).astype(BF16) for pp in range(G - 1)]
    ks = k * jnp.exp(per_chunk(b_last) - b)
    scale_rows = [jnp.exp(E[G - 1] - E[p]) for p in range(G)] + [jnp.exp(E[G - 1])] + [zero] * (8 - G - 1)
    scale_cols = jnp.concatenate(scale_rows, axis=0).T
    lane_chunk = lax.broadcasted_iota(I32, (1, R), 1) // C

    for h in range(GLA_HEADS):
        hk = slice(h * GLA_DK, (h + 1) * GLA_DK)
        hv = slice(h * GLA_DV, (h + 1) * GLA_DV)
        v = v_ref[rows, hv]
        state = st_ref[h]
        s = lax.dot_general(qa[:, hk], ka[:, hk], (((1,), (1,)), ((), ())), preferred_element_type=F32)
        s = jnp.where(keep, s, 0.0).astype(BF16)
        ks_t = ks[:, hk].T
        cols = scale_cols[hk]
        to_end = jnp.zeros((GLA_DK, R), F32)
        for p in range(G):
            to_end = jnp.where(lane_chunk == order[p], cols[:, p:p + 1], to_end)
        lhs_kv = jnp.concatenate([jnp.where(lane_chunk == order[p], ks_t, 0.0) for p in range(G - 1)]
                                 + [ks_t * to_end], axis=0).astype(BF16)
        kv = jnp.dot(lhs_kv, v, preferred_element_type=F32)
        n_x = (G - 1) * GLA_DK
        o = jnp.dot(jnp.concatenate([s, q_in[:, hk]] + [x[:, hk] for x in q_x], axis=-1),
                    jnp.concatenate([v, state.astype(BF16), kv[:n_x].astype(BF16)], axis=0),
                    preferred_element_type=F32)
        o_ref[rows, hv] = o.astype(BF16)
        st_ref[h] = cols[:, G:G + 1] * state + kv[n_x:]


def _gla_kernel(qf_ref, kf_ref, vf_ref, af_ref, qb_ref, kb_ref, vb_ref, ab_ref, u_ref, bias_ref,
                of_ref, ob_ref, st_ref, g_scr, *, n_chunks):
    @pl.when(pl.program_id(1) == 0)
    def _():
        st_ref[...] = jnp.zeros_like(st_ref)

    for d, a_ref in enumerate((af_ref, ab_ref)):
        z = jnp.dot(a_ref[...], u_ref[d], preferred_element_type=F32) + bias_ref[d]
        g_scr[d] = (jnp.minimum(z, 0.0) - jnp.log(1.0 + jnp.exp(-jnp.abs(z)))) * (1.0 / GLA_TAU)

    n_groups = n_chunks // GLA_GROUP
    group_rows = GLA_GROUP * GLA_CHUNK

    def body(it, carry):
        rf = pl.multiple_of(it * group_rows, group_rows)
        rb = pl.multiple_of((n_groups - 1 - it) * group_rows, group_rows)
        _gla_group(qf_ref, kf_ref, vf_ref, of_ref, g_scr.at[0], st_ref.at[0], rf, False)
        _gla_group(qb_ref, kb_ref, vb_ref, ob_ref, g_scr.at[1], st_ref.at[1], rb, True)
        return carry

    lax.fori_loop(0, n_groups, body, 0)


def _gla(qg, kg, vg, al, u_pad, bias, n_chunks=GLA_BLOCK_CHUNKS):
    B, S, _ = qg.shape
    rows = n_chunks * GLA_CHUNK
    NB = S // rows
    fwd = lambda b, n: (b, n, 0)
    bwd = lambda b, n: (b, NB - 1 - n, 0)
    specs = lambda im: [pl.BlockSpec((None, rows, t.shape[-1]), im) for t in (qg, kg, vg, al)]
    out = jax.ShapeDtypeStruct((B, S, GLA_HEADS * GLA_DV), BF16)
    return pl.pallas_call(
        functools.partial(_gla_kernel, n_chunks=n_chunks),
        grid=(B, NB),
        in_specs=specs(fwd) + specs(bwd) + [
            pl.BlockSpec(u_pad.shape, lambda b, n: (0, 0, 0)),
            pl.BlockSpec(bias.shape, lambda b, n: (0, 0, 0)),
        ],
        out_specs=[pl.BlockSpec((None, rows, GLA_HEADS * GLA_DV), fwd),
                   pl.BlockSpec((None, rows, GLA_HEADS * GLA_DV), bwd)],
        out_shape=[out, out],
        scratch_shapes=[pltpu.VMEM((2, GLA_HEADS, GLA_DK, GLA_DV), F32),
                        pltpu.VMEM((2, rows, GLA_HEADS * GLA_DK), F32)],
        compiler_params=_cparams(("arbitrary", "arbitrary"), VMEM_MIB["gla"]),
        name="gla",
    )(qg, kg, vg, al, qg, kg, vg, al, u_pad, bias)


def _attn_kernel(q_ref, k_ref, v_ref, o_ref, lse_ref, *, sub, win, half):
    L = k_ref.shape[0]
    lq = q_ref.shape[0]
    base = pl.program_id(2) * lq

    def body(t, carry):
        r0 = pl.multiple_of(t * sub, sub)
        rows = pl.ds(r0, sub)
        q0 = base + r0
        k0 = pl.multiple_of(jnp.clip(q0 - half, 0, L - win), half)
        qpos = q0 + lax.broadcasted_iota(I32, (sub, 1), 0)
        kpos = k0 + lax.broadcasted_iota(I32, (1, win), 1)
        valid = jnp.abs(qpos - kpos) <= half
        q = q_ref[rows, :]
        k = k_ref[pl.ds(k0, win), :]
        v = v_ref[pl.ds(k0, win), :]
        own = ((lax.broadcasted_iota(I32, (ATT_HPG * win, ATT_GW), 0) // win)
               == (lax.broadcasted_iota(I32, (ATT_HPG * win, ATT_GW), 1) // ATT_HD))
        k_heads = jnp.where(own, jnp.concatenate([k] * ATT_HPG, axis=0), jnp.zeros((), BF16))
        v_heads = jnp.where(own, jnp.concatenate([v] * ATT_HPG, axis=0), jnp.zeros((), BF16))
        s_all = lax.dot_general(q, k_heads, (((1,), (1,)), ((), ())), preferred_element_type=F32)
        ps, ls, lses = [], [], []
        for h in range(ATT_HPG):
            s = jnp.where(valid, s_all[:, h * win:(h + 1) * win], -jnp.inf)
            m = jnp.max(s, axis=-1, keepdims=True)
            pr = jnp.exp(s - m)
            l = jnp.sum(pr, axis=-1, keepdims=True)
            ps.append(pr.astype(BF16))
            ls.append(jnp.broadcast_to(l, (sub, ATT_HD)))
            lses.append(jnp.broadcast_to(m + jnp.log(l), (sub, ATT_HD)))
        o = jnp.dot(jnp.concatenate(ps, axis=-1), v_heads, preferred_element_type=F32)
        o_ref[rows, :] = (o / jnp.concatenate(ls, axis=-1)).astype(BF16)
        lse_ref[rows, :] = jnp.concatenate(lses, axis=-1)
        return carry

    lax.fori_loop(0, lq // sub, body, 0)


def _attn_group(q, k, v, window, lq=ATT_BLOCK_ROWS, sub=ATT_SUB_ROWS):
    B, d, L, _ = q.shape
    half = window // (2 * d)
    lq = min(lq, L)
    sub = min(sub, lq)
    win = min(sub + 2 * half, L)
    qmap = lambda b, r, i: (b, r, i, 0)
    kmap = lambda b, r, i: (b, r, 0, 0)
    return pl.pallas_call(
        functools.partial(_attn_kernel, sub=sub, win=win, half=half),
        grid=(B, d, L // lq),
        in_specs=[pl.BlockSpec((None, None, lq, ATT_GW), qmap),
                  pl.BlockSpec((None, None, L, ATT_GW), kmap),
                  pl.BlockSpec((None, None, L, ATT_GW), kmap)],
        out_specs=[pl.BlockSpec((None, None, lq, ATT_GW), qmap), pl.BlockSpec((None, None, lq, ATT_GW), qmap)],
        out_shape=[jax.ShapeDtypeStruct(q.shape, BF16), jax.ShapeDtypeStruct(q.shape, F32)],
        compiler_params=_cparams(("arbitrary", "arbitrary", "arbitrary"), VMEM_MIB["attn"]),
        name=f"attn_d{d}",
    )(q, k, v)


def _sigmoid(t):
    return 0.5 * jnp.tanh(0.5 * t) + 0.5


def _merge_kernel(of_ref, ob_ref, rg_ref, gn_ref, a0_ref, a1_ref, a2_ref, l0_ref, l1_ref, l2_ref,
                  ga_ref, gb_ref, x_ref, wa_ref, wb_ref, wo_ref, nf_ref, rw_ref, rb_ref,
                  x1_ref, h2_ref, ti_ref, tw_ref, cnt_ref, *scratch):
    def token_rows(blk_ref, scr):
        d, n, w = blk_ref.shape
        if d == 1:
            return blk_ref[0].astype(F32)
        for r in range(d):
            for j in range(w // LANES):
                scr[j, pl.ds(r, n, stride=d), :] = blk_ref[r, :, j * LANES:(j + 1) * LANES].astype(F32)
        return jnp.concatenate([scr[j] for j in range(w // LANES)], axis=-1)

    o = of_ref[...].astype(F32) + ob_ref[...].astype(F32)
    parts = []
    for h in range(GLA_HEADS):
        oh = o[:, h * GLA_DV:(h + 1) * GLA_DV]
        parts.append(oh * lax.rsqrt(jnp.mean(oh * oh, axis=-1, keepdims=True) + EPS))
    r = rg_ref[...].astype(F32)
    y_gla = jnp.concatenate(parts, axis=-1) * gn_ref[...] * (r * _sigmoid(r))

    l0, l1, l2 = (token_rows(l_ref, scr) for l_ref, scr in zip((l0_ref, l1_ref, l2_ref), scratch[0:3]))
    a0, a1, a2 = (token_rows(a_ref, scr) for a_ref, scr in zip((a0_ref, a1_ref, a2_ref), scratch[3:6]))
    m = jnp.maximum(jnp.maximum(l0, l1), l2)
    e0, e1, e2 = jnp.exp(l0 - m), jnp.exp(l1 - m), jnp.exp(l2 - m)
    y_att = (e0 * a0 + e1 * a1 + e2 * a2) / (e0 + e1 + e2)

    t_gla = jnp.dot(y_gla.astype(BF16), wa_ref[...], preferred_element_type=F32)
    t_att = jnp.dot(y_att.astype(BF16), wb_ref[...], preferred_element_type=F32)
    merged = _sigmoid(ga_ref[...].astype(F32)) * t_gla + _sigmoid(gb_ref[...].astype(F32)) * t_att
    x1 = x_ref[...] + jnp.dot(merged.astype(BF16), wo_ref[...], preferred_element_type=F32)
    x1_ref[...] = x1
    h2 = x1 * lax.rsqrt(jnp.mean(x1 * x1, axis=-1, keepdims=True) + EPS) * nf_ref[...]
    _to_slabs(h2_ref, h2)

    h_hi = h2.astype(BF16)
    h_lo = (h2 - h_hi.astype(F32)).astype(BF16)
    logits = jnp.dot(jnp.concatenate([h_hi, h_lo, h_hi], axis=-1), rw_ref[...],
                     preferred_element_type=F32) + rb_ref[...]
    lane = lax.broadcasted_iota(I32, logits.shape, 1).astype(F32)
    vals, idxs = [], []
    for _ in range(TOP_K):
        mx = jnp.max(logits, axis=-1, keepdims=True)
        ix = jnp.min(jnp.where(logits == mx, lane, float(LANES)), axis=-1, keepdims=True)
        vals.append(mx)
        idxs.append(ix)
        logits = jnp.where(lane == ix, -jnp.inf, logits)
    es = [jnp.exp(vk - vals[0]) for vk in vals]
    den = es[0] + es[1] + es[2] + es[3]
    ti = jnp.zeros(lane.shape, F32)
    tw = jnp.zeros(lane.shape, F32)
    for kk in range(TOP_K):
        ti = jnp.where(lane == float(kk), idxs[kk], ti)
        tw = jnp.where(lane == float(kk), es[kk] / den, tw)
    ti_ref[...] = ti.astype(I32)
    tw_ref[...] = tw

    @pl.when(pl.program_id(0) == 0)
    def _():
        cnt_ref[...] = jnp.zeros_like(cnt_ref)

    hits = sum((lane == ix).astype(F32) for ix in idxs)
    cnt_ref[...] += jnp.sum(hits, axis=0, keepdims=True)


def _merge(o_f, o_b, rg, gn, atts, lses, ga, gb, x2, wa, wb, wo, nf, rw, rb, tm=TM_MERGE):
    T, D = x2.shape
    B = atts[0].shape[0]
    nb = T // B // tm
    row = lambda w: pl.BlockSpec((tm, w), lambda i: (i, 0))
    const = lambda a: pl.BlockSpec(a.shape, lambda i: (0, 0))
    res = lambda a: pl.BlockSpec((None, a.shape[1], tm // a.shape[1], ATT_GW), lambda i: (i // nb, 0, i % nb, 0))
    return pl.pallas_call(
        _merge_kernel,
        grid=(T // tm,),
        in_specs=[row(D), row(D), row(D), const(gn)] + [res(a) for a in atts] + [res(a) for a in lses]
                 + [row(D), row(D), row(D), const(wa), const(wb), const(wo), const(nf), const(rw), const(rb)],
        out_specs=[row(D), pl.BlockSpec((tm * SLAB, LANES), lambda i: (i, 0)), row(LANES), row(LANES),
                   pl.BlockSpec((1, LANES), lambda i: (0, 0))],
        out_shape=[jax.ShapeDtypeStruct((T, D), F32), jax.ShapeDtypeStruct((T * SLAB, LANES), F32),
                   jax.ShapeDtypeStruct((T, LANES), I32), jax.ShapeDtypeStruct((T, LANES), F32),
                   jax.ShapeDtypeStruct((1, LANES), F32)],
        scratch_shapes=[pltpu.VMEM((ATT_GW // LANES, tm, LANES), F32)] * 6,
        compiler_params=_cparams(("arbitrary",), VMEM_MIB["merge"]),
        name="merge_router",
    )(o_f, o_b, rg, gn, *atts, *lses, ga, gb, x2, wa, wb, wo, nf, rw, rb)


SLAB = 8


def _to_slabs(ref, val):
    rows = val.shape[0]
    for s in range(SLAB):
        ref[pl.ds(s, rows, stride=SLAB), :] = val[:, s * LANES:(s + 1) * LANES]


def _from_slabs(ref, rows):
    return jnp.concatenate([ref[pl.ds(s, rows, stride=SLAB), :] for s in range(SLAB)], axis=-1)


K_PHASES = 2
DMA_SHARES = (0, 3, 3, 2)
N_XBUF = 3


def _experts_kernel(te_ref, dst_ref, h_hbm, w1_ref, b1_ref, w2_ref, b2_ref,
                    y_hbm, buf, w1p, w2b, sem_g, sem_s, *, tm, n_tok):
    i = pl.program_id(0)
    last = pl.num_programs(0) - 1
    slot = i % 2
    xslot = i % N_XBUF
    D = w2b.shape[0]

    def gather_start(row, s, r0, r1):
        for r in range(r0, r1):
            tok = dst_ref[row, r] & (n_tok - 1)
            pltpu.make_async_copy(h_hbm.at[pl.ds(pl.multiple_of(tok * SLAB, SLAB), SLAB), :],
                                  buf.at[s, pl.ds(r * SLAB, SLAB), :], sem_g.at[s]).start()

    def gather_wait(s):
        pltpu.make_async_copy(h_hbm.at[pl.ds(0, tm * SLAB), :], buf.at[s], sem_g.at[s]).wait()

    def scatter_start(row, s, r0, r1):
        for r in range(r0, r1):
            d = dst_ref[row, r]
            pltpu.make_async_copy(buf.at[N_XBUF + s, pl.ds(r * SLAB, SLAB), :],
                                  y_hbm.at[pl.ds(pl.multiple_of(d * SLAB, SLAB), SLAB), :], sem_s.at[s]).start()

    def scatter_wait(s):
        pltpu.make_async_copy(buf.at[N_XBUF + s], y_hbm.at[pl.ds(0, tm * SLAB), :], sem_s.at[s]).wait()

    @pl.when(i == 0)
    def _():
        buf[N_XBUF + 1] = jnp.zeros(buf.shape[1:], F32)
        gather_start(1, 0, 0, tm)
        gather_start(jnp.minimum(2, last), 1, 0, tm)

    @pl.when((i == 0) | (te_ref[i] != te_ref[jnp.maximum(i - 1, 0)]))
    def _():
        kk = lax.broadcasted_iota(I32, (2 * LANES, 2 * LANES), 0)
        nn = lax.broadcasted_iota(I32, (2 * LANES, 2 * LANES), 1)
        perm = (kk == jnp.where(nn < LANES, 2 * nn, 2 * (nn - LANES) + 1)).astype(BF16)
        for c in range(w1p.shape[1] // (2 * LANES)):
            cs = slice(c * 2 * LANES, (c + 1) * 2 * LANES)
            w1p[:, cs] = jnp.dot(w1_ref[:, cs].astype(BF16), perm, preferred_element_type=F32).astype(BF16)
        w2b[...] = w2_ref[...].astype(BF16)

    gather_wait(xslot)
    nxt = jnp.minimum(i + 3, last)
    nxt_slot = (i + 2) % N_XBUF
    bounds = [0]
    for share in DMA_SHARES:
        bounds.append(bounds[-1] + share * tm // sum(DMA_SHARES))

    def issue(phase):
        gather_start(nxt, nxt_slot, bounds[phase], bounds[phase + 1])
        scatter_start(i, 1 - slot, bounds[phase], bounds[phase + 1])

    slabs_per = SLAB // K_PHASES
    hid = b1_ref[...]
    for j in range(K_PHASES):
        issue(j)
        xj = jnp.concatenate([buf[xslot, pl.ds(s, tm, stride=SLAB), :]
                              for s in range(j * slabs_per, (j + 1) * slabs_per)], axis=-1).astype(BF16)
        kw = slabs_per * LANES
        hid = hid + jnp.dot(xj, w1p[j * kw:(j + 1) * kw, :], preferred_element_type=F32)
    acts = []
    for c in range(D // LANES):
        gate = jnp.minimum(hid[:, 2 * c * LANES:(2 * c + 1) * LANES], SWIGLU_LIMIT)
        up = jnp.clip(hid[:, (2 * c + 1) * LANES:(2 * c + 2) * LANES], -SWIGLU_LIMIT, SWIGLU_LIMIT)
        acts.append(((up + 1.0) * (gate * _sigmoid(gate * SWIGLU_ALPHA))).astype(BF16))
    act = jnp.concatenate(acts, axis=-1)

    @pl.when(i > 0)
    def _():
        scatter_wait(slot)

    n_out = len(DMA_SHARES) - K_PHASES
    part = D // n_out
    res = buf.at[N_XBUF + slot]
    for j in range(n_out):
        issue(K_PHASES + j)
        out = (jnp.dot(act, w2b[:, j * part:(j + 1) * part], preferred_element_type=F32)
               + b2_ref[:, j * part:(j + 1) * part])
        for t in range(part // LANES):
            res[pl.ds(j * part // LANES + t, tm, stride=SLAB), :] = out[:, t * LANES:(t + 1) * LANES]

    @pl.when(i == last)
    def _():
        scatter_wait(1 - slot)
        gather_wait((i + 1) % N_XBUF)
        gather_wait((i + 2) % N_XBUF)


def _experts(te_ext, dst_ext, h2s, w1, b1p, w2, b2, tm):
    T = h2s.shape[0] // SLAB
    n_steps = dst_ext.shape[0]
    _, D, F = w1.shape
    by_expert = lambda r, c: pl.BlockSpec((None, r, c), lambda i, te, dst: (te[i], 0, 0))
    return pl.pallas_call(
        functools.partial(_experts_kernel, tm=tm, n_tok=T),
        grid_spec=pltpu.PrefetchScalarGridSpec(
            num_scalar_prefetch=2,
            grid=(n_steps,),
            in_specs=[pl.BlockSpec(memory_space=pl.ANY), by_expert(D, F), by_expert(1, F),
                      by_expert(F // 2, D), by_expert(1, D)],
            out_specs=pl.BlockSpec(memory_space=pl.ANY),
            scratch_shapes=[pltpu.VMEM((N_XBUF + 2, tm * SLAB, LANES), F32),
                            pltpu.VMEM((D, F), BF16), pltpu.VMEM((F // 2, D), BF16),
                            pltpu.SemaphoreType.DMA((N_XBUF,)), pltpu.SemaphoreType.DMA((2,))],
        ),
        out_shape=jax.ShapeDtypeStruct((n_steps * tm * SLAB, LANES), F32),
        compiler_params=pltpu.CompilerParams(dimension_semantics=("arbitrary",),
                                             vmem_limit_bytes=VMEM_MIB["experts"] * MIB,
                                             disable_bounds_checks=True),
        name="experts",
    )(te_ext, dst_ext, h2s, w1, b1p, w2, b2)


def _route(topi, counts, tm):
    T = topi.shape[0]
    E = N_EXPERTS
    n_asg = T * TOP_K
    n_tiles = n_asg // tm + E
    e_flat = topi.reshape(n_asg)
    assert n_asg & (n_asg - 1) == 0 and E * n_asg < 2 ** 31
    order = lax.sort(e_flat * n_asg + jnp.arange(n_asg, dtype=I32)) & (n_asg - 1)
    cend = jnp.cumsum(counts)
    cstart = cend - counts
    padded = (counts + tm - 1) // tm * tm
    pend = jnp.cumsum(padded)
    pstart = pend - padded
    n_used = pend[-1] // tm
    tstart = jnp.arange(n_tiles, dtype=I32) * tm
    te_raw = jnp.sum(tstart[:, None] >= pend[None, :], axis=1, dtype=I32)
    last_e = jnp.sum((n_used - 1) * tm >= pend, dtype=I32)
    te = jnp.where(te_raw < E, te_raw, last_e)
    esc = jnp.minimum(te_raw, E - 1)
    j0 = tstart - pstart[esc]
    n_valid = jnp.where(te_raw < E, jnp.clip(counts[esc] - j0, 0, tm), 0)
    col = jnp.arange(tm, dtype=I32)[None, :]
    a = order[jnp.clip((cstart[esc] + j0)[:, None] + col, 0, n_asg - 1)]
    dst_valid = (a % TOP_K) * T + a // TOP_K
    cend_ext = jnp.concatenate([cend, jnp.full((1,), n_asg, I32)])
    dst_pad = (n_asg + tstart - cend_ext[te_raw])[:, None] + col
    dst = jnp.where(col < n_valid[:, None], dst_valid, dst_pad).astype(I32)
    dump = n_tiles * tm + jnp.arange(tm, dtype=I32)
    dst_ext = jnp.concatenate([dump[None, :], dst], axis=0)
    te_ext = jnp.concatenate([te, te[-1:]])
    return te_ext, dst_ext


def _final_kernel(x1_ref, y0_ref, y1_ref, y2_ref, y3_ref, tw_ref, p_ref, np_ref, wg_ref, wp_ref, nfin_ref, o_ref):
    tw = tw_ref[...]
    x2 = x1_ref[...]
    for kk, y_ref in enumerate((y0_ref, y1_ref, y2_ref, y3_ref)):
        x2 = x2 + tw[:, kk:kk + 1] * _from_slabs(y_ref, x2.shape[0])
    h3 = (x2 * lax.rsqrt(jnp.mean(x2 * x2, axis=-1, keepdims=True) + EPS) * np_ref[...]).astype(BF16)
    gate = _sigmoid(jnp.dot(h3, wg_ref[...], preferred_element_type=F32))
    pe = jnp.dot(p_ref[...].astype(BF16), wp_ref[...], preferred_element_type=F32)
    x3 = x2 + gate * pe
    o_ref[...] = x3 * lax.rsqrt(jnp.mean(x3 * x3, axis=-1, keepdims=True) + EPS) * nfin_ref[...]


def _final(x1, ybuf, tw, p2, n_ple, wg, wp, n_fin, tm=TM_FINAL):
    T, D = x1.shape
    nb = T // tm
    row = lambda w: pl.BlockSpec((tm, w), lambda i: (i, 0))
    const = lambda a: pl.BlockSpec(a.shape, lambda i: (0, 0))
    yspec = lambda kk: pl.BlockSpec((tm * SLAB, LANES), lambda i: (kk * nb + i, 0))
    return pl.pallas_call(
        _final_kernel,
        grid=(nb,),
        in_specs=[row(D)] + [yspec(kk) for kk in range(TOP_K)] + [row(LANES), row(p2.shape[1]),
                  const(n_ple), const(wg), const(wp), const(n_fin)],
        out_specs=row(D),
        out_shape=jax.ShapeDtypeStruct((T, D), F32),
        compiler_params=_cparams(("arbitrary",), VMEM_MIB["final"]),
        name="final",
    )(x1, ybuf, ybuf, ybuf, ybuf, tw, p2, n_ple, wg, wp, n_fin)


def kernel(x, p, positions, norm_mix, w_in, gla_fgate_up, gla_fgate_bias, gla_out_norm, w_branch_gla, w_branch_attn, w_out, norm_ffn, router_w, router_b, expert_w1, expert_b1, expert_w2, expert_b2, norm_ple, ple_gate_w, ple_proj, norm_final):
    B, S, D = x.shape
    T = B * S
    assert w_in.shape[0] == 1, "single-layer block: the final norm is fused into the layer's last kernel"
    x2 = x.reshape(T, D)
    pos2 = positions.reshape(T, 1)

    w = w_in[0]
    w_cat = jnp.concatenate([
        w[:, :3072],
        jnp.pad(w[:, 3072:3104], ((0, 0), (0, LANES - 2 * GLA_RANK))),
        w[:, 3104:3104 + 768] * (ATT_HD ** -0.5),
        w[:, 3104 + 768:],
    ], axis=1).astype(BF16)
    inv_freq = ROPE_THETA ** (-jnp.arange(0, ROT_DIM, 2, dtype=F32) / ROT_DIM)
    lane_f = (jnp.arange(LANES) % ATT_HD) % (ROT_DIM // 2)
    tab = jnp.zeros((8, LANES), F32).at[0].set(inv_freq[lane_f])
    u_pad = jnp.zeros((2, LANES, GLA_HEADS * GLA_DK), F32)
    u_pad = u_pad.at[0, :GLA_RANK].set(gla_fgate_up[0, 0]).at[1, GLA_RANK:2 * GLA_RANK].set(gla_fgate_up[0, 1])
    u_pad = u_pad.astype(BF16)
    bias = gla_fgate_bias[0].reshape(2, 1, GLA_HEADS * GLA_DK)

    qg, kg, vg, rg, al, ga, gb, *att_in = _inproj(x2, pos2, norm_mix[0].reshape(1, D), tab, w_cat, B)
    r3 = lambda t: t.reshape(B, S, t.shape[-1])
    o_f, o_b = _gla(r3(qg), r3(kg), r3(vg), r3(al), u_pad, bias)
    atts, lses = [], []
    for gi, (window, _) in enumerate(ATT_GROUPS):
        o_g, lse_g = _attn_group(*att_in[3 * gi:3 * gi + 3], window)
        atts.append(o_g)
        lses.append(lse_g)

    rw = jnp.pad(router_w[0], ((0, 0), (0, LANES - N_EXPERTS)))
    rw_hi = rw.astype(BF16)
    rw = jnp.concatenate([rw_hi, rw_hi, (rw - rw_hi.astype(F32)).astype(BF16)], axis=0)
    rb = jnp.concatenate([router_b[0], jnp.full((LANES - N_EXPERTS,), -jnp.inf, F32)]).reshape(1, LANES)
    x1, h2, topi, topw, cnt = _merge(
        o_f.reshape(T, D), o_b.reshape(T, D), rg, gla_out_norm[0].reshape(1, D), atts, lses, ga, gb, x2,
        w_branch_gla[0].astype(BF16), w_branch_attn[0].astype(BF16), w_out[0].astype(BF16),
        norm_ffn[0].reshape(1, D), rw, rb)

    tm_e = TM_EXPERT
    assert T & (T - 1) == 0, "token count must be a power of two (row index is masked out of the slot code)"
    te, dst2d = _route(topi[:, :TOP_K], cnt[0, :N_EXPERTS].astype(I32), tm_e)
    E, _, F = expert_w1[0].shape
    b1p = expert_b1[0].reshape(E, F // (2 * LANES), LANES, 2).transpose(0, 1, 3, 2).reshape(E, 1, F)
    ybuf = _experts(te, dst2d, h2, expert_w1[0], b1p, expert_w2[0], expert_b2[0][:, None, :], tm_e)

    out = _final(x1, ybuf, topw, p[0].reshape(T, -1), norm_ple[0].reshape(1, D), ple_gate_w[0].astype(BF16),
                 ple_proj[0].astype(BF16), norm_final.reshape(1, D))
    return out.reshape(B, S, D)
```

```python
import functools

import jax
import jax.numpy as jnp
from jax import lax
from jax.experimental import pallas as pl
from jax.experimental.pallas import tpu as pltpu

F32 = jnp.float32
BF16 = jnp.bfloat16
I32 = jnp.int32

EPS = 1e-6
GLA_HEADS = 4
GLA_DK = 128
GLA_DV = 256
GLA_RANK = 16
GLA_TAU = 16.0
GLA_CHUNK = 64
ATT_GROUPS = ((128, 1), (512, 4), (2048, 16))
ATT_HPG = 4
ATT_HD = 64
ATT_GW = ATT_HPG * ATT_HD
ROT_DIM = 16
ROPE_THETA = 500000.0
N_EXPERTS = 32
TOP_K = 4
SWIGLU_ALPHA = 1.702
SWIGLU_LIMIT = 7.0

LANES = 128
MIB = 1024 * 1024

TM_INPROJ = 512
TM_MERGE = 256
TM_FINAL = 256
TM_EXPERT = 256
GLA_BLOCK_CHUNKS = 16
ATT_BLOCK_ROWS = 1024
ATT_SUB_ROWS = 128
VMEM_MIB = {"inproj": 56, "gla": 48, "attn": 40, "merge": 48, "experts": 56, "final": 48}

SEG_WIDTHS = (512, 512, 1024, 1024, LANES, 768, 768, 768, 1024, 1024)
SEG_OFFS = tuple(sum(SEG_WIDTHS[:i]) for i in range(len(SEG_WIDTHS) + 1))


def _cparams(sem, vmem_mib):
    return pltpu.CompilerParams(dimension_semantics=sem, vmem_limit_bytes=vmem_mib * MIB)


def _inproj_kernel(x_ref, pos_ref, g_ref, tab_ref, w_ref, *refs):
    (qg_ref, kg_ref, vg_ref, rg_ref, al_ref, ga_ref, gb_ref), att_refs, ysc = refs[:7], refs[7:16], refs[16]
    x = x_ref[...]
    h = (x * lax.rsqrt(jnp.mean(x * x, axis=-1, keepdims=True) + EPS) * g_ref[...]).astype(BF16)

    def proj(seg):
        return jnp.dot(h, w_ref[:, SEG_OFFS[seg]:SEG_OFFS[seg + 1]], preferred_element_type=F32)

    qg_ref[...] = proj(0).astype(BF16)
    kg_ref[...] = proj(1).astype(BF16)
    vg_ref[...] = proj(2).astype(BF16)
    rg_ref[...] = proj(3).astype(BF16)
    al_ref[...] = proj(4).astype(BF16)
    ga_ref[...] = proj(8).astype(BF16)
    gb_ref[...] = proj(9).astype(BF16)

    ang = pos_ref[...].astype(F32) * tab_ref[0:1, :]
    cs = jnp.cos(ang)
    sn = jnp.sin(ang)
    lane = lax.broadcasted_iota(I32, (1, LANES), 1) % ATT_HD
    c_mul = jnp.where(lane < ROT_DIM, cs, 1.0)
    s_next = jnp.where(lane < ROT_DIM // 2, -sn, 0.0)
    s_prev = jnp.where((lane >= ROT_DIM // 2) & (lane < ROT_DIM), sn, 0.0)

    tm = ysc.shape[1]
    tiles_per_group = ATT_GW // LANES

    def emit(seg, which, rotary):
        y = proj(seg)
        for j in range(SEG_WIDTHS[seg] // LANES):
            t = y[:, j * LANES:(j + 1) * LANES]
            if rotary:
                t = (t * c_mul + pltpu.roll(t, LANES - ROT_DIM // 2, axis=1) * s_next
                     + pltpu.roll(t, ROT_DIM // 2, axis=1) * s_prev)
            ysc[j] = t
        for gi, (_, d) in enumerate(ATT_GROUPS):
            o_ref = att_refs[3 * gi + which]
            for r in range(d):
                for j in range(tiles_per_group):
                    o_ref[r, :, j * LANES:(j + 1) * LANES] = ysc[
                        gi * tiles_per_group + j, pl.ds(r, tm // d, stride=d), :].astype(BF16)

    emit(5, 0, True)
    emit(6, 1, True)
    emit(7, 2, False)


def _inproj(x2, pos2, gain, tab, w_cat, B, tm=TM_INPROJ):
    T, D = x2.shape
    S = T // B
    nb = S // tm
    row_widths = [SEG_WIDTHS[s] for s in (0, 1, 2, 3, 4, 8, 9)]
    outs = [jax.ShapeDtypeStruct((T, w), BF16) for w in row_widths]
    out_specs = [pl.BlockSpec((tm, w), lambda i: (i, 0)) for w in row_widths]
    for _, d in ATT_GROUPS:
        for _ in range(3):
            outs.append(jax.ShapeDtypeStruct((B, d, S // d, ATT_GW), BF16))
            out_specs.append(pl.BlockSpec((None, d, tm // d, ATT_GW), lambda i: (i // nb, 0, i % nb, 0)))
    return pl.pallas_call(
        _inproj_kernel,
        grid=(T // tm,),
        in_specs=[
            pl.BlockSpec((tm, D), lambda i: (i, 0)),
            pl.BlockSpec((tm, 1), lambda i: (i, 0)),
            pl.BlockSpec((1, D), lambda i: (0, 0)),
            pl.BlockSpec((8, LANES), lambda i: (0, 0)),
            pl.BlockSpec((D, SEG_OFFS[-1]), lambda i: (0, 0), pipeline_mode=pl.Buffered(1)),
        ],
        out_specs=out_specs,
        out_shape=outs,
        scratch_shapes=[pltpu.VMEM((SEG_WIDTHS[5] // LANES, tm, LANES), F32)],
        compiler_params=_cparams(("arbitrary",), VMEM_MIB["inproj"]),
        name="inproj",
    )(x2, pos2, gain, tab, w_cat)


def _split3(a):
    a1 = a.astype(BF16)
    r1 = a - a1.astype(F32)
    a2 = r1.astype(BF16)
    a3 = (r1 - a2.astype(F32)).astype(BF16)
    return a1, a2, a3


GLA_GROUP = 4


def _gla_group(q_ref, k_ref, v_ref, o_ref, g_ref, st_ref, r0, backward):
    C, G = GLA_CHUNK, GLA_GROUP
    R = C * G
    HK = GLA_HEADS * GLA_DK
    ri = lax.broadcasted_iota(I32, (R, R), 0)
    ci = lax.broadcasted_iota(I32, (R, R), 1)
    same = (ri // C) == (ci // C)
    if backward:
        tri = (same & (ci >= ri)).astype(BF16)
        keep = same & (ci > ri)
        ref_off, last_off = C // 2, 0
        order = list(range(G - 1, -1, -1))
    else:
        tri = (same & (ci <= ri)).astype(BF16)
        keep = same & (ci <= ri)
        ref_off, last_off = C // 2 - 1, C - 1
        order = list(range(G))
    pos_of = {c: p for p, c in enumerate(order)}
    rows = pl.ds(r0, R)

    def per_chunk(vals):
        return jnp.concatenate([jnp.broadcast_to(v, (C, HK)) for v in vals], axis=0)

    g1, g2, g3 = _split3(g_ref[rows, :])
    b = (jnp.dot(tri, g1, preferred_element_type=F32) + jnp.dot(tri, g2, preferred_element_type=F32)
         + jnp.dot(tri, g3, preferred_element_type=F32))
    b_last = [b[c * C + last_off:c * C + last_off + 1] for c in range(G)]
    b_mid = [b[c * C + ref_off:c * C + ref_off + 1] for c in range(G)]
    E = [b_last[order[0]]]
    for p in range(1, G):
        E.append(E[-1] + b_last[order[p]])
    zero = jnp.zeros((1, HK), F32)
    one = jnp.ones((1, HK), F32)

    q = q_ref[rows, :].astype(F32) * (GLA_DK ** -0.5)
    k = k_ref[rows, :].astype(F32)
    bm = per_chunk(b_mid)
    qa = (q * jnp.exp(b - bm)).astype(BF16)
    ka = (k * jnp.exp(bm - b)).astype(BF16)
    qi = q * jnp.exp(b)
    q_in = (qi * per_chunk([jnp.exp(E[pos_of[c] - 1]) if pos_of[c] > 0 else one for c in range(G)])).astype(BF16)
    q_x = [(qi * per_chunk([jnp.exp(E[pos_of[c] - 1] - E[pp]) if pos_of[c] > pp else zero for c in range(G)])
            ).astype(BF16) for pp in range(G - 1)]
    ks = k * jnp.exp(per_chunk(b_last) - b)
    scale_rows = [jnp.exp(E[G - 1] - E[p]) for p in range(G)] + [jnp.exp(E[G - 1])] + [zero] * (8 - G - 1)
    scale_cols = jnp.concatenate(scale_rows, axis=0).T
    lane_chunk = lax.broadcasted_iota(I32, (1, R), 1) // C

    for h in range(GLA_HEADS):
        hk = slice(h * GLA_DK, (h + 1) * GLA_DK)
        hv = slice(h * GLA_DV, (h + 1) * GLA_DV)
        v = v_ref[rows, hv]
        state = st_ref[h]
        s = lax.dot_general(qa[:, hk], ka[:, hk], (((1,), (1,)), ((), ())), preferred_element_type=F32)
        s = jnp.where(keep, s, 0.0).astype(BF16)
        ks_t = ks[:, hk].T
        cols = scale_cols[hk]
        to_end = jnp.zeros((GLA_DK, R), F32)
        for p in range(G):
            to_end = jnp.where(lane_chunk == order[p], cols[:, p:p + 1], to_end)
        lhs_kv = jnp.concatenate([jnp.where(lane_chunk == order[p], ks_t, 0.0) for p in range(G - 1)]
                                 + [ks_t * to_end], axis=0).astype(BF16)
        kv = jnp.dot(lhs_kv, v, preferred_element_type=F32)
        n_x = (G - 1) * GLA_DK
        o = jnp.dot(jnp.concatenate([s, q_in[:, hk]] + [x[:, hk] for x in q_x], axis=-1),
                    jnp.concatenate([v, state.astype(BF16), kv[:n_x].astype(BF16)], axis=0),
                    preferred_element_type=F32)
        o_ref[rows, hv] = o.astype(BF16)
        st_ref[h] = cols[:, G:G + 1] * state + kv[n_x:]


def _gla_kernel(qf_ref, kf_ref, vf_ref, af_ref, qb_ref, kb_ref, vb_ref, ab_ref, u_ref, bias_ref,
                of_ref, ob_ref, st_ref, g_scr, *, n_chunks):
    @pl.when(pl.program_id(1) == 0)
    def _():
        st_ref[...] = jnp.zeros_like(st_ref)

    for d, a_ref in enumerate((af_ref, ab_ref)):
        z = jnp.dot(a_ref[...], u_ref[d], preferred_element_type=F32) + bias_ref[d]
        g_scr[d] = (jnp.minimum(z, 0.0) - jnp.log(1.0 + jnp.exp(-jnp.abs(z)))) * (1.0 / GLA_TAU)

    n_groups = n_chunks // GLA_GROUP
    group_rows = GLA_GROUP * GLA_CHUNK

    def body(it, carry):
        rf = pl.multiple_of(it * group_rows, group_rows)
        rb = pl.multiple_of((n_groups - 1 - it) * group_rows, group_rows)
        _gla_group(qf_ref, kf_ref, vf_ref, of_ref, g_scr.at[0], st_ref.at[0], rf, False)
        _gla_group(qb_ref, kb_ref, vb_ref, ob_ref, g_scr.at[1], st_ref.at[1], rb, True)
        return carry

    lax.fori_loop(0, n_groups, body, 0)


def _gla(qg, kg, vg, al, u_pad, bias, n_chunks=GLA_BLOCK_CHUNKS):
    B, S, _ = qg.shape
    rows = n_chunks * GLA_CHUNK
    NB = S // rows
    fwd = lambda b, n: (b, n, 0)
    bwd = lambda b, n: (b, NB - 1 - n, 0)
    specs = lambda im: [pl.BlockSpec((None, rows, t.shape[-1]), im) for t in (qg, kg, vg, al)]
    out = jax.ShapeDtypeStruct((B, S, GLA_HEADS * GLA_DV), BF16)
    return pl.pallas_call(
        functools.partial(_gla_kernel, n_chunks=n_chunks),
        grid=(B, NB),
        in_specs=specs(fwd) + specs(bwd) + [
            pl.BlockSpec(u_pad.shape, lambda b, n: (0, 0, 0)),
            pl.BlockSpec(bias.shape, lambda b, n: (0, 0, 0)),
        ],
        out_specs=[pl.BlockSpec((None, rows, GLA_HEADS * GLA_DV), fwd),
                   pl.BlockSpec((None, rows, GLA_HEADS * GLA_DV), bwd)],
        out_shape=[out, out],
        scratch_shapes=[pltpu.VMEM((2, GLA_HEADS, GLA_DK, GLA_DV), F32),
                        pltpu.VMEM((2, rows, GLA_HEADS * GLA_DK), F32)],
        compiler_params=_cparams(("arbitrary", "arbitrary"), VMEM_MIB["gla"]),
        name="gla",
    )(qg, kg, vg, al, qg, kg, vg, al, u_pad, bias)


def _attn_kernel(q_ref, k_ref, v_ref, o_ref, lse_ref, *, sub, win, half):
    L = k_ref.shape[0]
    lq = q_ref.shape[0]
    base = pl.program_id(2) * lq

    def body(t, carry):
        r0 = pl.multiple_of(t * sub, sub)
        rows = pl.ds(r0, sub)
        q0 = base + r0
        k0 = pl.multiple_of(jnp.clip(q0 - half, 0, L - win), half)
        qpos = q0 + lax.broadcasted_iota(I32, (sub, 1), 0)
        kpos = k0 + lax.broadcasted_iota(I32, (1, win), 1)
        valid = jnp.abs(qpos - kpos) <= half
        q = q_ref[rows, :]
        k = k_ref[pl.ds(k0, win), :]
        v = v_ref[pl.ds(k0, win), :]
        own = ((lax.broadcasted_iota(I32, (ATT_HPG * win, ATT_GW), 0) // win)
               == (lax.broadcasted_iota(I32, (ATT_HPG * win, ATT_GW), 1) // ATT_HD))
        k_heads = jnp.where(own, jnp.concatenate([k] * ATT_HPG, axis=0), jnp.zeros((), BF16))
        v_heads = jnp.where(own, jnp.concatenate([v] * ATT_HPG, axis=0), jnp.zeros((), BF16))
        s_all = lax.dot_general(q, k_heads, (((1,), (1,)), ((), ())), preferred_element_type=F32)
        ps, ls, lses = [], [], []
        for h in range(ATT_HPG):
            s = jnp.where(valid, s_all[:, h * win:(h + 1) * win], -jnp.inf)
            m = jnp.max(s, axis=-1, keepdims=True)
            pr = jnp.exp(s - m)
            l = jnp.sum(pr, axis=-1, keepdims=True)
            ps.append(pr.astype(BF16))
            ls.append(jnp.broadcast_to(l, (sub, ATT_HD)))
            lses.append(jnp.broadcast_to(m + jnp.log(l), (sub, ATT_HD)))
        o = jnp.dot(jnp.concatenate(ps, axis=-1), v_heads, preferred_element_type=F32)
        o_ref[rows, :] = (o / jnp.concatenate(ls, axis=-1)).astype(BF16)
        lse_ref[rows, :] = jnp.concatenate(lses, axis=-1)
        return carry

    lax.fori_loop(0, lq // sub, body, 0)


def _attn_group(q, k, v, window, lq=ATT_BLOCK_ROWS, sub=ATT_SUB_ROWS):
    B, d, L, _ = q.shape
    half = window // (2 * d)
    lq = min(lq, L)
    sub = min(sub, lq)
    win = min(sub + 2 * half, L)
    qmap = lambda b, r, i: (b, r, i, 0)
    kmap = lambda b, r, i: (b, r, 0, 0)
    return pl.pallas_call(
        functools.partial(_attn_kernel, sub=sub, win=win, half=half),
        grid=(B, d, L // lq),
        in_specs=[pl.BlockSpec((None, None, lq, ATT_GW), qmap),
                  pl.BlockSpec((None, None, L, ATT_GW), kmap),
                  pl.BlockSpec((None, None, L, ATT_GW), kmap)],
        out_specs=[pl.BlockSpec((None, None, lq, ATT_GW), qmap), pl.BlockSpec((None, None, lq, ATT_GW), qmap)],
        out_shape=[jax.ShapeDtypeStruct(q.shape, BF16), jax.ShapeDtypeStruct(q.shape, F32)],
        compiler_params=_cparams(("arbitrary", "arbitrary", "arbitrary"), VMEM_MIB["attn"]),
        name=f"attn_d{d}",
    )(q, k, v)


def _sigmoid(t):
    return 0.5 * jnp.tanh(0.5 * t) + 0.5


def _merge_kernel(of_ref, ob_ref, rg_ref, gn_ref, a0_ref, a1_ref, a2_ref, l0_ref, l1_ref, l2_ref,
                  ga_ref, gb_ref, x_ref, wa_ref, wb_ref, wo_ref, nf_ref, rw_ref, rb_ref,
                  x1_ref, h2_ref, ti_ref, tw_ref, cnt_ref, *scratch):
    def token_rows(blk_ref, scr):
        d, n, w = blk_ref.shape
        if d == 1:
            return blk_ref[0].astype(F32)
        for r in range(d):
            for j in range(w // LANES):
                scr[j, pl.ds(r, n, stride=d), :] = blk_ref[r, :, j * LANES:(j + 1) * LANES].astype(F32)
        return jnp.concatenate([scr[j] for j in range(w // LANES)], axis=-1)

    o = of_ref[...].astype(F32) + ob_ref[...].astype(F32)
    parts = []
    for h in range(GLA_HEADS):
        oh = o[:, h * GLA_DV:(h + 1) * GLA_DV]
        parts.append(oh * lax.rsqrt(jnp.mean(oh * oh, axis=-1, keepdims=True) + EPS))
    r = rg_ref[...].astype(F32)
    y_gla = jnp.concatenate(parts, axis=-1) * gn_ref[...] * (r * _sigmoid(r))

    l0, l1, l2 = (token_rows(l_ref, scr) for l_ref, scr in zip((l0_ref, l1_ref, l2_ref), scratch[0:3]))
    a0, a1, a2 = (token_rows(a_ref, scr) for a_ref, scr in zip((a0_ref, a1_ref, a2_ref), scratch[3:6]))
    m = jnp.maximum(jnp.maximum(l0, l1), l2)
    e0, e1, e2 = jnp.exp(l0 - m), jnp.exp(l1 - m), jnp.exp(l2 - m)
    y_att = (e0 * a0 + e1 * a1 + e2 * a2) / (e0 + e1 + e2)

    t_gla = jnp.dot(y_gla.astype(BF16), wa_ref[...], preferred_element_type=F32)
    t_att = jnp.dot(y_att.astype(BF16), wb_ref[...], preferred_element_type=F32)
    merged = _sigmoid(ga_ref[...].astype(F32)) * t_gla + _sigmoid(gb_ref[...].astype(F32)) * t_att
    x1 = x_ref[...] + jnp.dot(merged.astype(BF16), wo_ref[...], preferred_element_type=F32)
    x1_ref[...] = x1
    h2 = x1 * lax.rsqrt(jnp.mean(x1 * x1, axis=-1, keepdims=True) + EPS) * nf_ref[...]
    _to_slabs(h2_ref, h2)

    h_hi = h2.astype(BF16)
    h_lo = (h2 - h_hi.astype(F32)).astype(BF16)
    logits = jnp.dot(jnp.concatenate([h_hi, h_lo, h_hi], axis=-1), rw_ref[...],
                     preferred_element_type=F32) + rb_ref[...]
    lane = lax.broadcasted_iota(I32, logits.shape, 1).astype(F32)
    vals, idxs = [], []
    for _ in range(TOP_K):
        mx = jnp.max(logits, axis=-1, keepdims=True)
        ix = jnp.min(jnp.where(logits == mx, lane, float(LANES)), axis=-1, keepdims=True)
        vals.append(mx)
        idxs.append(ix)
        logits = jnp.where(lane == ix, -jnp.inf, logits)
    es = [jnp.exp(vk - vals[0]) for vk in vals]
    den = es[0] + es[1] + es[2] + es[3]
    ti = jnp.zeros(lane.shape, F32)
    tw = jnp.zeros(lane.shape, F32)
    for kk in range(TOP_K):
        ti = jnp.where(lane == float(kk), idxs[kk], ti)
        tw = jnp.where(lane == float(kk), es[kk] / den, tw)
    ti_ref[...] = ti.astype(I32)
    tw_ref[...] = tw

    @pl.when(pl.program_id(0) == 0)
    def _():
        cnt_ref[...] = jnp.zeros_like(cnt_ref)

    hits = sum((lane == ix).astype(F32) for ix in idxs)
    cnt_ref[...] += jnp.sum(hits, axis=0, keepdims=True)


def _merge(o_f, o_b, rg, gn, atts, lses, ga, gb, x2, wa, wb, wo, nf, rw, rb, tm=TM_MERGE):
    T, D = x2.shape
    B = atts[0].shape[0]
    nb = T // B // tm
    row = lambda w: pl.BlockSpec((tm, w), lambda i: (i, 0))
    const = lambda a: pl.BlockSpec(a.shape, lambda i: (0, 0))
    res = lambda a: pl.BlockSpec((None, a.shape[1], tm // a.shape[1], ATT_GW), lambda i: (i // nb, 0, i % nb, 0))
    return pl.pallas_call(
        _merge_kernel,
        grid=(T // tm,),
        in_specs=[row(D), row(D), row(D), const(gn)] + [res(a) for a in atts] + [res(a) for a in lses]
                 + [row(D), row(D), row(D), const(wa), const(wb), const(wo), const(nf), const(rw), const(rb)],
        out_specs=[row(D), pl.BlockSpec((tm * SLAB, LANES), lambda i: (i, 0)), row(LANES), row(LANES),
                   pl.BlockSpec((1, LANES), lambda i: (0, 0))],
        out_shape=[jax.ShapeDtypeStruct((T, D), F32), jax.ShapeDtypeStruct((T * SLAB, LANES), F32),
                   jax.ShapeDtypeStruct((T, LANES), I32), jax.ShapeDtypeStruct((T, LANES), F32),
                   jax.ShapeDtypeStruct((1, LANES), F32)],
        scratch_shapes=[pltpu.VMEM((ATT_GW // LANES, tm, LANES), F32)] * 6,
        compiler_params=_cparams(("arbitrary",), VMEM_MIB["merge"]),
        name="merge_router",
    )(o_f, o_b, rg, gn, *atts, *lses, ga, gb, x2, wa, wb, wo, nf, rw, rb)


SLAB = 8


def _to_slabs(ref, val):
    rows = val.shape[0]
    for s in range(SLAB):
        ref[pl.ds(s, rows, stride=SLAB), :] = val[:, s * LANES:(s + 1) * LANES]


def _from_slabs(ref, rows):
    return jnp.concatenate([ref[pl.ds(s, rows, stride=SLAB), :] for s in range(SLAB)], axis=-1)


K_PHASES = 2
DMA_SHARES = (0, 3, 3, 2)
N_XBUF = 3


def _experts_kernel(te_ref, dst_ref, h_hbm, w1_ref, b1_ref, w2_ref, b2_ref,
                    y_hbm, buf, w1p, w2b, sem_g, sem_s, *, tm, n_tok):
    i = pl.program_id(0)
    last = pl.num_programs(0) - 1
    slot = i % 2
    xslot = i % N_XBUF
    D = w2b.shape[0]

    def gather_start(row, s, r0, r1):
        for r in range(r0, r1):
            tok = dst_ref[row, r] & (n_tok - 1)
            pltpu.make_async_copy(h_hbm.at[pl.ds(pl.multiple_of(tok * SLAB, SLAB), SLAB), :],
                                  buf.at[s, pl.ds(r * SLAB, SLAB), :], sem_g.at[s]).start()

    def gather_wait(s):
        pltpu.make_async_copy(h_hbm.at[pl.ds(0, tm * SLAB), :], buf.at[s], sem_g.at[s]).wait()

    def scatter_start(row, s, r0, r1):
        for r in range(r0, r1):
            d = dst_ref[row, r]
            pltpu.make_async_copy(buf.at[N_XBUF + s, pl.ds(r * SLAB, SLAB), :],
                                  y_hbm.at[pl.ds(pl.multiple_of(d * SLAB, SLAB), SLAB), :], sem_s.at[s]).start()

    def scatter_wait(s):
        pltpu.make_async_copy(buf.at[N_XBUF + s], y_hbm.at[pl.ds(0, tm * SLAB), :], sem_s.at[s]).wait()

    @pl.when(i == 0)
    def _():
        buf[N_XBUF + 1] = jnp.zeros(buf.shape[1:], F32)
        gather_start(1, 0, 0, tm)
        gather_start(jnp.minimum(2, last), 1, 0, tm)

    @pl.when((i == 0) | (te_ref[i] != te_ref[jnp.maximum(i - 1, 0)]))
    def _():
        kk = lax.broadcasted_iota(I32, (2 * LANES, 2 * LANES), 0)
        nn = lax.broadcasted_iota(I32, (2 * LANES, 2 * LANES), 1)
        perm = (kk == jnp.where(nn < LANES, 2 * nn, 2 * (nn - LANES) + 1)).astype(BF16)
        for c in range(w1p.shape[1] // (2 * LANES)):
            cs = slice(c * 2 * LANES, (c + 1) * 2 * LANES)
            w1p[:, cs] = jnp.dot(w1_ref[:, cs].astype(BF16), perm, preferred_element_type=F32).astype(BF16)
        w2b[...] = w2_ref[...].astype(BF16)

    gather_wait(xslot)
    nxt = jnp.minimum(i + 3, last)
    nxt_slot = (i + 2) % N_XBUF
    bounds = [0]
    for share in DMA_SHARES:
        bounds.append(bounds[-1] + share * tm // sum(DMA_SHARES))

    def issue(phase):
        gather_start(nxt, nxt_slot, bounds[phase], bounds[phase + 1])
        scatter_start(i, 1 - slot, bounds[phase], bounds[phase + 1])

    slabs_per = SLAB // K_PHASES
    hid = b1_ref[...]
    for j in range(K_PHASES):
        issue(j)
        xj = jnp.concatenate([buf[xslot, pl.ds(s, tm, stride=SLAB), :]
                              for s in range(j * slabs_per, (j + 1) * slabs_per)], axis=-1).astype(BF16)
        kw = slabs_per * LANES
        hid = hid + jnp.dot(xj, w1p[j * kw:(j + 1) * kw, :], preferred_element_type=F32)
    acts = []
    for c in range(D // LANES):
        gate = jnp.minimum(hid[:, 2 * c * LANES:(2 * c + 1) * LANES], SWIGLU_LIMIT)
        up = jnp.clip(hid[:, (2 * c + 1) * LANES:(2 * c + 2) * LANES], -SWIGLU_LIMIT, SWIGLU_LIMIT)
        acts.append(((up + 1.0) * (gate * _sigmoid(gate * SWIGLU_ALPHA))).astype(BF16))
    act = jnp.concatenate(acts, axis=-1)

    @pl.when(i > 0)
    def _():
        scatter_wait(slot)

    n_out = len(DMA_SHARES) - K_PHASES
    part = D // n_out
    res = buf.at[N_XBUF + slot]
    for j in range(n_out):
        issue(K_PHASES + j)
        out = (jnp.dot(act, w2b[:, j * part:(j + 1) * part], preferred_element_type=F32)
               + b2_ref[:, j * part:(j + 1) * part])
        for t in range(part // LANES):
            res[pl.ds(j * part // LANES + t, tm, stride=SLAB), :] = out[:, t * LANES:(t + 1) * LANES]

    @pl.when(i == last)
    def _():
        scatter_wait(1 - slot)
        gather_wait((i + 1) % N_XBUF)
        gather_wait((i + 2) % N_XBUF)


def _experts(te_ext, dst_ext, h2s, w1, b1p, w2, b2, tm):
    T = h2s.shape[0] // SLAB
    n_steps = dst_ext.shape[0]
    _, D, F = w1.shape
    by_expert = lambda r, c: pl.BlockSpec((None, r, c), lambda i, te, dst: (te[i], 0, 0))
    return pl.pallas_call(
        functools.partial(_experts_kernel, tm=tm, n_tok=T),
        grid_spec=pltpu.PrefetchScalarGridSpec(
            num_scalar_prefetch=2,
            grid=(n_steps,),
            in_specs=[pl.BlockSpec(memory_space=pl.ANY), by_expert(D, F), by_expert(1, F),
                      by_expert(F // 2, D), by_expert(1, D)],
            out_specs=pl.BlockSpec(memory_space=pl.ANY),
            scratch_shapes=[pltpu.VMEM((N_XBUF + 2, tm * SLAB, LANES), F32),
                            pltpu.VMEM((D, F), BF16), pltpu.VMEM((F // 2, D), BF16),
                            pltpu.SemaphoreType.DMA((N_XBUF,)), pltpu.SemaphoreType.DMA((2,))],
        ),
        out_shape=jax.ShapeDtypeStruct((n_steps * tm * SLAB, LANES), F32),
        compiler_params=pltpu.CompilerParams(dimension_semantics=("arbitrary",),
                                             vmem_limit_bytes=VMEM_MIB["experts"] * MIB,
                                             disable_bounds_checks=True),
        name="experts",
    )(te_ext, dst_ext, h2s, w1, b1p, w2, b2)


def _route(topi, counts, tm):
    T = topi.shape[0]
    E = N_EXPERTS
    n_asg = T * TOP_K
    n_tiles = n_asg // tm + E
    e_flat = topi.reshape(n_asg)
    _, order = lax.sort((e_flat, jnp.arange(n_asg, dtype=I32)), num_keys=1, is_stable=True)
    cend = jnp.cumsum(counts)
    cstart = cend - counts
    padded = (counts + tm - 1) // tm * tm
    pend = jnp.cumsum(padded)
    pstart = pend - padded
    n_used = pend[-1] // tm
    tstart = jnp.arange(n_tiles, dtype=I32) * tm
    te_raw = jnp.sum(tstart[:, None] >= pend[None, :], axis=1, dtype=I32)
    last_e = jnp.sum((n_used - 1) * tm >= pend, dtype=I32)
    te = jnp.where(te_raw < E, te_raw, last_e)
    esc = jnp.minimum(te_raw, E - 1)
    j0 = tstart - pstart[esc]
    n_valid = jnp.where(te_raw < E, jnp.clip(counts[esc] - j0, 0, tm), 0)
    col = jnp.arange(tm, dtype=I32)[None, :]
    a = order[jnp.clip((cstart[esc] + j0)[:, None] + col, 0, n_asg - 1)]
    dst_valid = (a % TOP_K) * T + a // TOP_K
    cend_ext = jnp.concatenate([cend, jnp.full((1,), n_asg, I32)])
    dst_pad = (n_asg + tstart - cend_ext[te_raw])[:, None] + col
    dst = jnp.where(col < n_valid[:, None], dst_valid, dst_pad).astype(I32)
    dump = n_tiles * tm + jnp.arange(tm, dtype=I32)
    dst_ext = jnp.concatenate([dump[None, :], dst], axis=0)
    te_ext = jnp.concatenate([te, te[-1:]])
    return te_ext, dst_ext


def _final_kernel(x1_ref, y0_ref, y1_ref, y2_ref, y3_ref, tw_ref, p_ref, np_ref, wg_ref, wp_ref, nfin_ref, o_ref):
    tw = tw_ref[...]
    x2 = x1_ref[...]
    for kk, y_ref in enumerate((y0_ref, y1_ref, y2_ref, y3_ref)):
        x2 = x2 + tw[:, kk:kk + 1] * _from_slabs(y_ref, x2.shape[0])
    h3 = (x2 * lax.rsqrt(jnp.mean(x2 * x2, axis=-1, keepdims=True) + EPS) * np_ref[...]).astype(BF16)
    gate = _sigmoid(jnp.dot(h3, wg_ref[...], preferred_element_type=F32))
    pe = jnp.dot(p_ref[...].astype(BF16), wp_ref[...], preferred_element_type=F32)
    x3 = x2 + gate * pe
    o_ref[...] = x3 * lax.rsqrt(jnp.mean(x3 * x3, axis=-1, keepdims=True) + EPS) * nfin_ref[...]


def _final(x1, ybuf, tw, p2, n_ple, wg, wp, n_fin, tm=TM_FINAL):
    T, D = x1.shape
    nb = T // tm
    row = lambda w: pl.BlockSpec((tm, w), lambda i: (i, 0))
    const = lambda a: pl.BlockSpec(a.shape, lambda i: (0, 0))
    yspec = lambda kk: pl.BlockSpec((tm * SLAB, LANES), lambda i: (kk * nb + i, 0))
    return pl.pallas_call(
        _final_kernel,
        grid=(nb,),
        in_specs=[row(D)] + [yspec(kk) for kk in range(TOP_K)] + [row(LANES), row(p2.shape[1]),
                  const(n_ple), const(wg), const(wp), const(n_fin)],
        out_specs=row(D),
        out_shape=jax.ShapeDtypeStruct((T, D), F32),
        compiler_params=_cparams(("arbitrary",), VMEM_MIB["final"]),
        name="final",
    )(x1, ybuf, ybuf, ybuf, ybuf, tw, p2, n_ple, wg, wp, n_fin)


def kernel(x, p, positions, norm_mix, w_in, gla_fgate_up, gla_fgate_bias, gla_out_norm, w_branch_gla, w_branch_attn, w_out, norm_ffn, router_w, router_b, expert_w1, expert_b1, expert_w2, expert_b2, norm_ple, ple_gate_w, ple_proj, norm_final):
    B, S, D = x.shape
    T = B * S
    assert w_in.shape[0] == 1, "single-layer block: the final norm is fused into the layer's last kernel"
    x2 = x.reshape(T, D)
    pos2 = positions.reshape(T, 1)

    w = w_in[0]
    w_cat = jnp.concatenate([
        w[:, :3072],
        jnp.pad(w[:, 3072:3104], ((0, 0), (0, LANES - 2 * GLA_RANK))),
        w[:, 3104:3104 + 768] * (ATT_HD ** -0.5),
        w[:, 3104 + 768:],
    ], axis=1).astype(BF16)
    inv_freq = ROPE_THETA ** (-jnp.arange(0, ROT_DIM, 2, dtype=F32) / ROT_DIM)
    lane_f = (jnp.arange(LANES) % ATT_HD) % (ROT_DIM // 2)
    tab = jnp.zeros((8, LANES), F32).at[0].set(inv_freq[lane_f])
    u_pad = jnp.zeros((2, LANES, GLA_HEADS * GLA_DK), F32)
    u_pad = u_pad.at[0, :GLA_RANK].set(gla_fgate_up[0, 0]).at[1, GLA_RANK:2 * GLA_RANK].set(gla_fgate_up[0, 1])
    u_pad = u_pad.astype(BF16)
    bias = gla_fgate_bias[0].reshape(2, 1, GLA_HEADS * GLA_DK)

    qg, kg, vg, rg, al, ga, gb, *att_in = _inproj(x2, pos2, norm_mix[0].reshape(1, D), tab, w_cat, B)
    r3 = lambda t: t.reshape(B, S, t.shape[-1])
    o_f, o_b = _gla(r3(qg), r3(kg), r3(vg), r3(al), u_pad, bias)
    atts, lses = [], []
    for gi, (window, _) in enumerate(ATT_GROUPS):
        o_g, lse_g = _attn_group(*att_in[3 * gi:3 * gi + 3], window)
        atts.append(o_g)
        lses.append(lse_g)

    rw = jnp.pad(router_w[0], ((0, 0), (0, LANES - N_EXPERTS)))
    rw_hi = rw.astype(BF16)
    rw = jnp.concatenate([rw_hi, rw_hi, (rw - rw_hi.astype(F32)).astype(BF16)], axis=0)
    rb = jnp.concatenate([router_b[0], jnp.full((LANES - N_EXPERTS,), -jnp.inf, F32)]).reshape(1, LANES)
    x1, h2, topi, topw, cnt = _merge(
        o_f.reshape(T, D), o_b.reshape(T, D), rg, gla_out_norm[0].reshape(1, D), atts, lses, ga, gb, x2,
        w_branch_gla[0].astype(BF16), w_branch_attn[0].astype(BF16), w_out[0].astype(BF16),
        norm_ffn[0].reshape(1, D), rw, rb)

    tm_e = TM_EXPERT
    assert T & (T - 1) == 0, "token count must be a power of two (row index is masked out of the slot code)"
    te, dst2d = _route(topi[:, :TOP_K], cnt[0, :N_EXPERTS].astype(I32), tm_e)
    E, _, F = expert_w1[0].shape
    b1p = expert_b1[0].reshape(E, F // (2 * LANES), LANES, 2).transpose(0, 1, 3, 2).reshape(E, 1, F)
    ybuf = _experts(te, dst2d, h2, expert_w1[0], b1p, expert_w2[0], expert_b2[0][:, None, :], tm_e)

    out = _final(x1, ybuf, topw, p[0].reshape(T, -1), norm_ple[0].reshape(1, D), ple_gate_w[0].astype(BF16),
                 ple_proj[0].astype(BF16), norm_final.reshape(1, D))
    return out.reshape(B, S, D)
```
